```python
import jax, jax.numpy as jnp
from jax import lax
import numpy as np

D_MODEL = 1024
BATCH = 16
SEQ = 2048
DEPTH = 1

CHUNK = 64
GLA_HEADS = 4
GLA_DK = D_MODEL // 2 // GLA_HEADS
GLA_DV = D_MODEL // GLA_HEADS
GLA_K_WIDTH = GLA_HEADS * GLA_DK
GLA_V_WIDTH = GLA_HEADS * GLA_DV
GK_RANK = 16
GATE_NORMALIZER = 16.0
ATT_HEADS = 16
ATT_DH = 64
ATT_WIDTH = ATT_HEADS * ATT_DH
LEFT_CHUNKS = 8
PAD = LEFT_CHUNKS * CHUNK
BAND = PAD + CHUNK
MAX_REL = 256
N_BRANCH = 2
RMS_EPS = 1e-6

IN_SPLITS = [GLA_K_WIDTH, GLA_K_WIDTH, GLA_V_WIDTH, GLA_V_WIDTH, GK_RANK,
             ATT_WIDTH, ATT_WIDTH, ATT_WIDTH, ATT_WIDTH, N_BRANCH * D_MODEL]
IN_COLS = int(sum(IN_SPLITS))

kernel_name = "hybrid_gla_chunkattn_gated_block"


def rms_norm(x, g):
    xf = x.astype(jnp.float32)
    y = xf * lax.rsqrt(jnp.mean(xf * xf, axis=-1, keepdims=True) + RMS_EPS)
    return (y * g.astype(jnp.float32)).astype(x.dtype)


def gla_branch(q, k, v, gk_code, g_path, gk_up, gk_bias, gla_norm_g, w_o_gla):
    B, T, _ = q.shape
    NC = T // CHUNK
    q = q.reshape(B, NC, CHUNK, GLA_HEADS, GLA_DK) * (GLA_DK ** -0.5)
    k = k.reshape(B, NC, CHUNK, GLA_HEADS, GLA_DK)
    v = v.reshape(B, NC, CHUNK, GLA_HEADS, GLA_DV)
    gk = jax.nn.log_sigmoid((gk_code @ gk_up + gk_bias).astype(jnp.float32)) / GATE_NORMALIZER
    gk = gk.reshape(B, NC, CHUNK, GLA_HEADS, GLA_DK)
    a_cum = jnp.cumsum(gk, axis=2)
    a_end = a_cum[:, :, -1]
    k_dec = (k.astype(jnp.float32) * jnp.exp(a_end[:, :, None] - a_cum)).astype(k.dtype)
    upd = jnp.einsum('bnchk,bnchv->bnhkv', k_dec, v)
    decay = jnp.exp(a_end).astype(upd.dtype)

    def step(state, inp):
        d, u = inp
        state = d[..., None] * state + u
        return state, state

    init = jnp.zeros((B, GLA_HEADS, GLA_DK, GLA_DV), upd.dtype)
    _, s_all = lax.scan(step, init, (jnp.moveaxis(decay, 1, 0), jnp.moveaxis(upd, 1, 0)))
    s_all = jnp.moveaxis(s_all, 0, 1)
    o = jnp.einsum('bnchk,bnhkv->bnchv', q, s_all)
    o = rms_norm(o, gla_norm_g).reshape(B, T, GLA_V_WIDTH)
    return (o * jax.nn.silu(g_path)) @ w_o_gla


def chunk_attention_branch(q, k, v, g_path, rel_bias, w_o_att):
    B, T, _ = q.shape
    NC = T // CHUNK
    q = q.reshape(B, T, ATT_HEADS, ATT_DH)
    k = k.reshape(B, T, ATT_HEADS, ATT_DH)
    v = v.reshape(B, T, ATT_HEADS, ATT_DH)
    k_pad = jnp.pad(k, ((0, 0), (PAD, 0), (0, 0), (0, 0)))
    v_pad = jnp.pad(v, ((0, 0), (PAD, 0), (0, 0), (0, 0)))
    dist = np.arange(CHUNK)[:, None] + PAD - np.arange(BAND)[None, :]
    rel_idx = np.clip(dist, -MAX_REL, MAX_REL) + MAX_REL
    bias = rel_bias[:, rel_idx].astype(jnp.float32)
    band_pos = jnp.arange(BAND)
    scale = ATT_DH ** -0.5

    def attend(c):
        qb = lax.dynamic_slice_in_dim(q, c * CHUNK, CHUNK, axis=1)
        kb = lax.dynamic_slice_in_dim(k_pad, c * CHUNK, BAND, axis=1)
        vb = lax.dynamic_slice_in_dim(v_pad, c * CHUNK, BAND, axis=1)
        s = jnp.einsum('bqhd,bkhd->bhqk', qb, kb).astype(jnp.float32) * scale + bias
        valid = band_pos >= PAD - c * CHUNK
        s = jnp.where(valid, s, -1e30)
        p = jax.nn.softmax(s, axis=-1).astype(vb.dtype)
        return jnp.einsum('bhqk,bkhd->bqhd', p, vb)

    o = lax.map(attend, jnp.arange(NC))
    o = jnp.transpose(o, (1, 0, 2, 3, 4)).reshape(B, T, ATT_WIDTH)
    return (o * jax.nn.silu(g_path)) @ w_o_att


def setup_inputs(seed: int = 0) -> dict:
    key = jax.random.key(seed)
    ks = jax.random.split(key, 12)
    f32 = jnp.float32
    return {
        "x": jax.random.normal(ks[0], (BATCH, SEQ, D_MODEL), f32),
        "norm_pre_g": 1.0 + 0.05 * jax.random.normal(ks[1], (D_MODEL,), f32),
        "w_in": jax.random.normal(ks[2], (D_MODEL, IN_COLS), f32) * D_MODEL ** -0.5,
        "gk_up": jax.random.normal(ks[3], (GK_RANK, GLA_K_WIDTH), f32) * GK_RANK ** -0.5,
        "gk_bias": 0.02 * jax.random.normal(ks[4], (GLA_K_WIDTH,), f32),
        "gla_norm_g": 1.0 + 0.05 * jax.random.normal(ks[5], (GLA_DV,), f32),
        "rel_bias": 0.1 * jax.random.normal(ks[6], (ATT_HEADS, 2 * MAX_REL + 1), f32),
        "w_o_gla": jax.random.normal(ks[7], (GLA_V_WIDTH, D_MODEL), f32) * GLA_V_WIDTH ** -0.5,
        "w_o_att": jax.random.normal(ks[8], (ATT_WIDTH, D_MODEL), f32) * ATT_WIDTH ** -0.5,
        "merge_bias": 0.02 * jax.random.normal(ks[9], (N_BRANCH * D_MODEL,), f32),
        "w_out": jax.random.normal(ks[10], (D_MODEL, D_MODEL), f32) * D_MODEL ** -0.5,
        "norm_post_g": 1.0 + 0.05 * jax.random.normal(ks[11], (D_MODEL,), f32),
    }


def reference(x, norm_pre_g, w_in, gk_up, gk_bias, gla_norm_g, rel_bias,
              w_o_gla, w_o_att, merge_bias, w_out, norm_post_g):
    splits = [int(s) for s in np.cumsum(IN_SPLITS)[:-1]]
    for _layer in range(DEPTH):
        h = rms_norm(x, norm_pre_g)
        proj = h @ w_in
        (q_g, k_g, v_g, g_g, gk_code,
         q_a, k_a, v_a, g_a, gate_logits) = jnp.split(proj, splits, axis=-1)
        y_gla = gla_branch(q_g, k_g, v_g, gk_code, g_g, gk_up, gk_bias, gla_norm_g, w_o_gla)
        y_att = chunk_attention_branch(q_a, k_a, v_a, g_a, rel_bias, w_o_att)
        gates = jax.nn.sigmoid(gate_logits + merge_bias)
        gate_gla, gate_att = jnp.split(gates, 2, axis=-1)
        merged = gate_gla * y_gla + gate_att * y_att
        y = merged @ w_out
        x = x + rms_norm(y, norm_post_g)
    return x
```

```python
import functools

import numpy as np
import jax
import jax.numpy as jnp
from jax import lax
from jax.experimental import pallas as pl
from jax.experimental.pallas import tpu as pltpu

D_MODEL = 1024
CHUNK = 64
GLA_HEADS = 4
GLA_DK = 128
GLA_DV = 256
GLA_K_WIDTH = GLA_HEADS * GLA_DK
GLA_V_WIDTH = GLA_HEADS * GLA_DV
GK_RANK = 16
GATE_NORMALIZER = 16.0
ATT_HEADS = 16
ATT_DH = 64
ATT_WIDTH = ATT_HEADS * ATT_DH
LEFT_CHUNKS = 8
MAX_REL = 256
RMS_EPS = 1e-6
NEG_INF = -1e30

LANES = 128

QA_OFF = 0
KA_OFF = 1024
VA_OFF = 2048
GA_OFF = 3072
GATE_OFF = 4096
VG_OFF = 6144
GG_OFF = 7168
QG_OFF = 8192
KG_OFF = 8704
PROJ_COLS = 9216
GK_PAD = LANES

GROUP = 4
GROUP_ROWS = GROUP * CHUNK
WIN_CHUNKS = GROUP + LEFT_CHUNKS
WIN_ROWS = WIN_CHUNKS * CHUNK
HEADS_PER_STEP = LANES // ATT_DH

INPROJ_TM = 1024
INPROJ_TN = 1024
GLA_ROWS = 512
MERGE_TM = 512


def _sigmoid(x):
    return 1.0 / (1.0 + jnp.exp(-x))


def _inproj_kernel(x_ref, g_ref, w_ref, wgk_ref, proj_ref, gk_ref, h_ref):
    @pl.when(pl.program_id(1) == 0)
    def _():
        x = x_ref[...]
        ms = jnp.mean(x * x, axis=-1, keepdims=True)
        h = (x * lax.rsqrt(ms + RMS_EPS) * g_ref[...]).astype(jnp.bfloat16)
        h_ref[...] = h
        gk_ref[...] = jnp.dot(
            h, wgk_ref[...], preferred_element_type=jnp.float32
        ).astype(jnp.bfloat16)

    proj_ref[...] = jnp.dot(
        h_ref[...], w_ref[...], preferred_element_type=jnp.float32
    ).astype(jnp.bfloat16)


def _inproj(x2, g_pre, w_main, w_gk):
    n = x2.shape[0]
    return pl.pallas_call(
        _inproj_kernel,
        grid=(n // INPROJ_TM, PROJ_COLS // INPROJ_TN),
        in_specs=[
            pl.BlockSpec((INPROJ_TM, D_MODEL), lambda i, j: (i, 0)),
            pl.BlockSpec((1, D_MODEL), lambda i, j: (0, 0)),
            pl.BlockSpec((D_MODEL, INPROJ_TN), lambda i, j: (0, j)),
            pl.BlockSpec((D_MODEL, GK_PAD), lambda i, j: (0, 0)),
        ],
        out_specs=[
            pl.BlockSpec((INPROJ_TM, INPROJ_TN), lambda i, j: (i, j)),
            pl.BlockSpec((INPROJ_TM, GK_PAD), lambda i, j: (i, 0)),
        ],
        out_shape=[
            jax.ShapeDtypeStruct((n, PROJ_COLS), jnp.bfloat16),
            jax.ShapeDtypeStruct((n, GK_PAD), jnp.bfloat16),
        ],
        scratch_shapes=[pltpu.VMEM((INPROJ_TM, D_MODEL), jnp.bfloat16)],
        compiler_params=pltpu.CompilerParams(
            dimension_semantics=("arbitrary", "arbitrary"),
            vmem_limit_bytes=48 * 1024 * 1024,
        ),
        name="inproj",
    )(x2, g_pre, w_main, w_gk)


def _chunk_cumsum(x, row_in_chunk):
    shift = 1
    while shift < CHUNK:
        rolled = pltpu.roll(x, shift, axis=0)
        x = x + jnp.where(row_in_chunk >= shift, rolled, 0.0)
        shift *= 2
    return x


def _gla_kernel(q_ref, k_ref, v_ref, g_ref, code_ref, up_ref, bias_ref,
                gn_ref, z_ref, state_ref):
    n_chunks = GLA_ROWS // CHUNK

    @pl.when(pl.program_id(1) == 0)
    def _():
        state_ref[...] = jnp.zeros_like(state_ref)

    logits = jnp.dot(code_ref[...], up_ref[...],
                     preferred_element_type=jnp.float32) + bias_ref[...]
    gk = (jnp.minimum(logits, 0.0)
          - jnp.log1p(jnp.exp(-jnp.abs(logits)))) * (1.0 / GATE_NORMALIZER)
    row = lax.broadcasted_iota(jnp.int32, gk.shape, 0)
    a_cum = _chunk_cumsum(gk, jnp.bitwise_and(row, CHUNK - 1))
    a_cum3 = a_cum.reshape(n_chunks, CHUNK, GLA_K_WIDTH)
    a_end = a_cum3[:, CHUNK - 1, :]
    k_dec = (k_ref[...].astype(jnp.float32).reshape(n_chunks, CHUNK, GLA_K_WIDTH)
             * jnp.exp(a_end[:, None, :] - a_cum3)).astype(jnp.bfloat16)
    decay_t = jnp.exp(a_end).T
    gn = gn_ref[...]
    q_scale = GLA_DK ** -0.5

    for c in range(n_chunks):
        rows = slice(c * CHUNK, (c + 1) * CHUNK)
        for h in range(GLA_HEADS):
            kcols = slice(h * GLA_DK, (h + 1) * GLA_DK)
            vcols = slice(h * GLA_DV, (h + 1) * GLA_DV)
            upd = lax.dot_general(
                k_dec[c][:, kcols], v_ref[rows, vcols],
                (((0,), (0,)), ((), ())),
                preferred_element_type=jnp.float32)
            state = decay_t[kcols, c:c + 1] * state_ref[h] + upd
            state_ref[h] = state
            o = jnp.dot(q_ref[rows, kcols], state.astype(jnp.bfloat16),
                        preferred_element_type=jnp.float32) * q_scale
            ms = jnp.mean(o * o, axis=-1, keepdims=True)
            o = o * lax.rsqrt(ms + RMS_EPS) * gn
            gate = g_ref[rows, vcols].astype(jnp.float32)
            z_ref[rows, vcols] = (o * gate * _sigmoid(gate)).astype(jnp.bfloat16)


def _gla(proj, gk_code, up_pad, gk_bias, gla_norm_g, batch, seq):
    n = proj.shape[0]
    steps = seq // GLA_ROWS
    row_map = lambda off, width: (lambda b, t: (b * steps + t, off // width))
    return pl.pallas_call(
        _gla_kernel,
        grid=(batch, steps),
        in_specs=[
            pl.BlockSpec((GLA_ROWS, GLA_K_WIDTH), row_map(QG_OFF, GLA_K_WIDTH)),
            pl.BlockSpec((GLA_ROWS, GLA_K_WIDTH), row_map(KG_OFF, GLA_K_WIDTH)),
            pl.BlockSpec((GLA_ROWS, GLA_V_WIDTH), row_map(VG_OFF, GLA_V_WIDTH)),
            pl.BlockSpec((GLA_ROWS, GLA_V_WIDTH), row_map(GG_OFF, GLA_V_WIDTH)),
            pl.BlockSpec((GLA_ROWS, GK_PAD), lambda b, t: (b * steps + t, 0)),
            pl.BlockSpec((GK_PAD, GLA_K_WIDTH), lambda b, t: (0, 0)),
            pl.BlockSpec((1, GLA_K_WIDTH), lambda b, t: (0, 0)),
            pl.BlockSpec((1, GLA_DV), lambda b, t: (0, 0)),
        ],
        out_specs=pl.BlockSpec((GLA_ROWS, GLA_V_WIDTH),
                               lambda b, t: (b * steps + t, 0)),
        out_shape=jax.ShapeDtypeStruct((n, GLA_V_WIDTH), jnp.bfloat16),
        scratch_shapes=[pltpu.VMEM((GLA_HEADS, GLA_DK, GLA_DV), jnp.float32)],
        compiler_params=pltpu.CompilerParams(
            dimension_semantics=("arbitrary", "arbitrary"),
            vmem_limit_bytes=48 * 1024 * 1024,
        ),
        name="gla",
    )(proj, proj, proj, proj, gk_code, up_pad, gk_bias, gla_norm_g)


def _attn_group(q_ref, k_ref, v_ref, g_ref, tbl, z_ref, q_start, k_start, n_keys):
    q2 = q_ref[pl.ds(q_start, GROUP_ROWS), :]
    lane = lax.broadcasted_iota(jnp.int32, q2.shape, 1)
    first = lane < ATT_DH
    zero = jnp.zeros_like(q2)
    qs = jnp.concatenate(
        [jnp.where(first, q2, zero), jnp.where(first, zero, q2)], axis=0)
    kw = k_ref[pl.ds(k_start, n_keys), :]
    vw = v_ref[pl.ds(k_start, n_keys), :]
    s = lax.dot_general(qs, kw, (((1,), (1,)), ((), ())),
                        preferred_element_type=jnp.float32)
    s = s + tbl
    m = jnp.max(s, axis=-1, keepdims=True)
    p = jnp.exp(s - m)
    l = jnp.sum(p, axis=-1, keepdims=True)
    o2 = jnp.dot(p.astype(jnp.bfloat16), vw,
                 preferred_element_type=jnp.float32) / l
    o = jnp.where(first, o2[:GROUP_ROWS], o2[GROUP_ROWS:])
    gate = g_ref[pl.ds(q_start, GROUP_ROWS), :].astype(jnp.float32)
    z_ref[pl.ds(q_start, GROUP_ROWS), :] = (
        o * gate * _sigmoid(gate)).astype(jnp.bfloat16)


def _attn_kernel(q_ref, k_ref, v_ref, g_ref, tbl_ref, z_ref, *, seq):
    n_groups = seq // GROUP_ROWS
    lead = LEFT_CHUNKS // GROUP
    for g in range(lead):
        n_keys = (g + 1) * GROUP_ROWS
        tbl = tbl_ref[:, :, WIN_ROWS - n_keys:].reshape(
            HEADS_PER_STEP * GROUP_ROWS, n_keys)
        _attn_group(q_ref, k_ref, v_ref, g_ref, tbl, z_ref,
                    g * GROUP_ROWS, 0, n_keys)

    def body(g, carry):
        q_start = pl.multiple_of(g * GROUP_ROWS, GROUP_ROWS)
        k_start = pl.multiple_of((g - lead) * GROUP_ROWS, GROUP_ROWS)
        tbl = tbl_ref[...].reshape(HEADS_PER_STEP * GROUP_ROWS, WIN_ROWS)
        _attn_group(q_ref, k_ref, v_ref, g_ref, tbl, z_ref,
                    q_start, k_start, WIN_ROWS)
        return carry

    lax.fori_loop(lead, n_groups, body, 0)


def _attn(proj, table, batch, seq):
    n = proj.shape[0]
    pairs = ATT_HEADS // HEADS_PER_STEP
    col_map = lambda off: (lambda b, p: (b, off // LANES + p))
    return pl.pallas_call(
        functools.partial(_attn_kernel, seq=seq),
        grid=(batch, pairs),
        in_specs=[
            pl.BlockSpec((seq, LANES), col_map(QA_OFF)),
            pl.BlockSpec((seq, LANES), col_map(KA_OFF)),
            pl.BlockSpec((seq, LANES), col_map(VA_OFF)),
            pl.BlockSpec((seq, LANES), col_map(GA_OFF)),
            pl.BlockSpec((HEADS_PER_STEP, GROUP_ROWS, WIN_ROWS),
                         lambda b, p: (p, 0, 0)),
        ],
        out_specs=pl.BlockSpec((seq, LANES), lambda b, p: (b, p)),
        out_shape=jax.ShapeDtypeStruct((n, ATT_WIDTH), jnp.bfloat16),
        compiler_params=pltpu.CompilerParams(
            dimension_semantics=("arbitrary", "arbitrary"),
            vmem_limit_bytes=48 * 1024 * 1024,
        ),
        name="attn",
    )(proj, proj, proj, proj, table)


def _bias_table(rel_bias):
    i = np.arange(GROUP_ROWS)[:, None]
    j = np.arange(WIN_ROWS)[None, :]
    dist = i - j + LEFT_CHUNKS * CHUNK
    rel_idx = np.clip(dist, -MAX_REL, MAX_REL) + MAX_REL
    dc = j // CHUNK - i // CHUNK
    valid = (dc >= 0) & (dc <= LEFT_CHUNKS)
    return jnp.where(valid[None], rel_bias[:, rel_idx].astype(jnp.float32), NEG_INF)


def _merge_kernel(zg_ref, za_ref, lg_ref, la_ref, x_ref, wog_ref, woa_ref,
                  wout_ref, mb_ref, gp_ref, o_ref):
    yg = jnp.dot(zg_ref[...], wog_ref[...], preferred_element_type=jnp.float32)
    ya = jnp.dot(za_ref[...], woa_ref[...], preferred_element_type=jnp.float32)
    gate_g = _sigmoid(lg_ref[...].astype(jnp.float32) + mb_ref[0:1, :])
    gate_a = _sigmoid(la_ref[...].astype(jnp.float32) + mb_ref[1:2, :])
    merged = (gate_g * yg + gate_a * ya).astype(jnp.bfloat16)
    y = jnp.dot(merged, wout_ref[...], preferred_element_type=jnp.float32)
    ms = jnp.mean(y * y, axis=-1, keepdims=True)
    o_ref[...] = x_ref[...] + y * lax.rsqrt(ms + RMS_EPS) * gp_ref[...]


def _merge(z_gla, z_att, proj, x2, w_o_gla, w_o_att, w_out, merge_bias2, g_post):
    n = x2.shape[0]
    row = lambda i: (i, 0)
    const = lambda i: (0, 0)
    return pl.pallas_call(
        _merge_kernel,
        grid=(n // MERGE_TM,),
        in_specs=[
            pl.BlockSpec((MERGE_TM, D_MODEL), row),
            pl.BlockSpec((MERGE_TM, D_MODEL), row),
            pl.BlockSpec((MERGE_TM, D_MODEL), lambda i: (i, GATE_OFF // D_MODEL)),
            pl.BlockSpec((MERGE_TM, D_MODEL), lambda i: (i, GATE_OFF // D_MODEL + 1)),
            pl.BlockSpec((MERGE_TM, D_MODEL), row),
            pl.BlockSpec((D_MODEL, D_MODEL), const),
            pl.BlockSpec((D_MODEL, D_MODEL), const),
            pl.BlockSpec((D_MODEL, D_MODEL), const),
            pl.BlockSpec((2, D_MODEL), const),
            pl.BlockSpec((1, D_MODEL), const),
        ],
        out_specs=pl.BlockSpec((MERGE_TM, D_MODEL), row),
        out_shape=jax.ShapeDtypeStruct((n, D_MODEL), jnp.float32),
        compiler_params=pltpu.CompilerParams(
            dimension_semantics=("arbitrary",),
            vmem_limit_bytes=48 * 1024 * 1024,
        ),
        name="merge",
    )(z_gla, z_att, proj, proj, x2, w_o_gla, w_o_att, w_out, merge_bias2, g_post)


def kernel(x, norm_pre_g, w_in, gk_up, gk_bias, gla_norm_g, rel_bias, w_o_gla,
           w_o_att, merge_bias, w_out, norm_post_g):
    batch, seq, d = x.shape
    assert d == D_MODEL and seq % GLA_ROWS == 0 and seq % GROUP_ROWS == 0
    assert (batch * seq) % INPROJ_TM == 0
    bf16 = jnp.bfloat16

    splits = np.cumsum([GLA_K_WIDTH, GLA_K_WIDTH, GLA_V_WIDTH, GLA_V_WIDTH, GK_RANK,
                        ATT_WIDTH, ATT_WIDTH, ATT_WIDTH, ATT_WIDTH])
    (w_qg, w_kg, w_vg, w_gg, w_code, w_qa, w_ka, w_va, w_ga, w_gate) = jnp.split(
        w_in, [int(s) for s in splits], axis=1)
    w_main = jnp.concatenate(
        [w_qa * (ATT_DH ** -0.5), w_ka, w_va, w_ga, w_gate, w_vg, w_gg, w_qg, w_kg],
        axis=1).astype(bf16)
    w_gk = jnp.pad(w_code, ((0, 0), (0, GK_PAD - GK_RANK))).astype(bf16)
    up_pad = jnp.pad(gk_up, ((0, GK_PAD - GK_RANK), (0, 0))).astype(bf16)

    x2 = x.reshape(batch * seq, D_MODEL)
    proj, gk_code = _inproj(x2, norm_pre_g.reshape(1, D_MODEL), w_main, w_gk)
    z_gla = _gla(proj, gk_code, up_pad, gk_bias.reshape(1, GLA_K_WIDTH),
                 gla_norm_g.reshape(1, GLA_DV), batch, seq)
    z_att = _attn(proj, _bias_table(rel_bias), batch, seq)
    out = _merge(z_gla, z_att, proj, x2, w_o_gla.astype(bf16), w_o_att.astype(bf16),
                 w_out.astype(bf16), merge_bias.reshape(2, D_MODEL),
                 norm_post_g.reshape(1, D_MODEL))
    return out.reshape(batch, seq, D_MODEL)
```

```python
import functools

import numpy as np
import jax
import jax.numpy as jnp
from jax import lax
from jax.experimental import pallas as pl
from jax.experimental.pallas import tpu as pltpu

D_MODEL = 1024
CHUNK = 64
GLA_HEADS = 4
GLA_DK = 128
GLA_DV = 256
GLA_K_WIDTH = GLA_HEADS * GLA_DK
GLA_V_WIDTH = GLA_HEADS * GLA_DV
GK_RANK = 16
GATE_NORMALIZER = 16.0
ATT_HEADS = 16
ATT_DH = 64
ATT_WIDTH = ATT_HEADS * ATT_DH
LEFT_CHUNKS = 8
MAX_REL = 256
RMS_EPS = 1e-6
NEG_INF = -1e30

LANES = 128
BF16_SUBLANES = 16
LOG2_E = float(np.log2(np.e))

QA_OFF = 0
KA_OFF = 1024
GA_OFF = 2048
GATE_OFF = 3072
VG_OFF = 5120
GG_OFF = 6144
QG_OFF = 7168
KG_OFF = 7680
PROJ_COLS = 8192
GK_PAD = LANES

GROUP = 4
GROUP_ROWS = GROUP * CHUNK
WIN_CHUNKS = GROUP + LEFT_CHUNKS
WIN_ROWS = WIN_CHUNKS * CHUNK
HEADS_PER_STEP = LANES // ATT_DH
PAIR_COLS = HEADS_PER_STEP * GROUP_ROWS

INPROJ_TM = 512
INPROJ_TN = 1024
GLA_ROWS = 512
MERGE_TM = 512
VMEM_LIMIT = 56 * 1024 * 1024

_NT = (((1,), (1,)), ((), ()))


def _sigmoid(x):
    return 1.0 / (1.0 + jnp.exp(-x))


def _inproj_kernel(x_ref, g_ref, w_ref, wvt_ref, wgk_ref, proj_ref, vt_ref, gk_ref):
    x = x_ref[...]
    ms = jnp.mean(x * x, axis=-1, keepdims=True)
    h = (x * lax.rsqrt(ms + RMS_EPS) * g_ref[...]).astype(jnp.bfloat16)
    gk_ref[...] = jnp.dot(
        h, wgk_ref[...], preferred_element_type=jnp.float32).astype(jnp.bfloat16)
    for j in range(PROJ_COLS // INPROJ_TN):
        cols = slice(j * INPROJ_TN, (j + 1) * INPROJ_TN)
        proj_ref[:, cols] = jnp.dot(
            h, w_ref[:, cols], preferred_element_type=jnp.float32
        ).astype(jnp.bfloat16)
    vt_ref[...] = lax.dot_general(
        wvt_ref[...], h, _NT, preferred_element_type=jnp.float32
    ).astype(jnp.bfloat16)


def _inproj(x2, g_pre, w_main, w_vt, w_gk):
    n = x2.shape[0]
    const = lambda i: (0, 0)
    resident = pl.Buffered(1)
    return pl.pallas_call(
        _inproj_kernel,
        grid=(n // INPROJ_TM,),
        in_specs=[
            pl.BlockSpec((INPROJ_TM, D_MODEL), lambda i: (i, 0)),
            pl.BlockSpec((1, D_MODEL), const),
            pl.BlockSpec((D_MODEL, PROJ_COLS), const, pipeline_mode=resident),
            pl.BlockSpec((ATT_WIDTH, D_MODEL), const, pipeline_mode=resident),
            pl.BlockSpec((D_MODEL, GK_PAD), const, pipeline_mode=resident),
        ],
        out_specs=[
            pl.BlockSpec((INPROJ_TM, PROJ_COLS), lambda i: (i, 0)),
            pl.BlockSpec((ATT_WIDTH, INPROJ_TM), lambda i: (0, i)),
            pl.BlockSpec((INPROJ_TM, GK_PAD), lambda i: (i, 0)),
        ],
        out_shape=[
            jax.ShapeDtypeStruct((n, PROJ_COLS), jnp.bfloat16),
            jax.ShapeDtypeStruct((ATT_WIDTH, n), jnp.bfloat16),
            jax.ShapeDtypeStruct((n, GK_PAD), jnp.bfloat16),
        ],
        compiler_params=pltpu.CompilerParams(
            dimension_semantics=("arbitrary",),
            vmem_limit_bytes=VMEM_LIMIT,
        ),
        name="inproj",
    )(x2, g_pre, w_main, w_vt, w_gk)


def _chunk_cumsum(x, row_in_chunk):
    shift = 1
    while shift < CHUNK:
        rolled = pltpu.roll(x, shift, axis=0)
        x = x + jnp.where(row_in_chunk >= shift, rolled, 0.0)
        shift *= 2
    return x


def _gla_kernel(q_ref, k_ref, v_ref, g_ref, code_ref, up_ref, bias_ref,
                gn_ref, z_ref, state_ref):
    n_chunks = GLA_ROWS // CHUNK

    @pl.when(pl.program_id(1) == 0)
    def _():
        state_ref[...] = jnp.zeros_like(state_ref)

    logits = jnp.dot(code_ref[...], up_ref[...],
                     preferred_element_type=jnp.float32) + bias_ref[...]
    gk = (jnp.minimum(logits, 0.0)
          - jnp.log1p(jnp.exp(-jnp.abs(logits)))) * (1.0 / GATE_NORMALIZER)
    row = lax.broadcasted_iota(jnp.int32, gk.shape, 0)
    a_cum = _chunk_cumsum(gk, jnp.bitwise_and(row, CHUNK - 1))
    a_cum3 = a_cum.reshape(n_chunks, CHUNK, GLA_K_WIDTH)
    a_end = a_cum3[:, CHUNK - 1, :]
    k_dec = (k_ref[...].astype(jnp.float32).reshape(n_chunks, CHUNK, GLA_K_WIDTH)
             * jnp.exp(a_end[:, None, :] - a_cum3)).astype(jnp.bfloat16)
    decay_t = jnp.exp(a_end).T
    gn = gn_ref[...]
    q_scale = GLA_DK ** -0.5

    for c in range(n_chunks):
        rows = slice(c * CHUNK, (c + 1) * CHUNK)
        for h in range(GLA_HEADS):
            kcols = slice(h * GLA_DK, (h + 1) * GLA_DK)
            vcols = slice(h * GLA_DV, (h + 1) * GLA_DV)
            upd = lax.dot_general(
                k_dec[c][:, kcols], v_ref[rows, vcols],
                (((0,), (0,)), ((), ())),
                preferred_element_type=jnp.float32)
            state = decay_t[kcols, c:c + 1] * state_ref[h] + upd
            state_ref[h] = state
            o = jnp.dot(q_ref[rows, kcols], state.astype(jnp.bfloat16),
                        preferred_element_type=jnp.float32) * q_scale
            ms = jnp.mean(o * o, axis=-1, keepdims=True)
            o = o * lax.rsqrt(ms + RMS_EPS) * gn
            gate = g_ref[rows, vcols].astype(jnp.float32)
            z_ref[rows, vcols] = (o * gate * _sigmoid(gate)).astype(jnp.bfloat16)


def _gla(proj, gk_code, up_pad, gk_bias, gla_norm_g, batch, seq):
    n = proj.shape[0]
    steps = seq // GLA_ROWS
    row_map = lambda off, width: (lambda b, t: (b * steps + t, off // width))
    return pl.pallas_call(
        _gla_kernel,
        grid=(batch, steps),
        in_specs=[
            pl.BlockSpec((GLA_ROWS, GLA_K_WIDTH), row_map(QG_OFF, GLA_K_WIDTH)),
            pl.BlockSpec((GLA_ROWS, GLA_K_WIDTH), row_map(KG_OFF, GLA_K_WIDTH)),
            pl.BlockSpec((GLA_ROWS, GLA_V_WIDTH), row_map(VG_OFF, GLA_V_WIDTH)),
            pl.BlockSpec((GLA_ROWS, GLA_V_WIDTH), row_map(GG_OFF, GLA_V_WIDTH)),
            pl.BlockSpec((GLA_ROWS, GK_PAD), lambda b, t: (b * steps + t, 0)),
            pl.BlockSpec((GK_PAD, GLA_K_WIDTH), lambda b, t: (0, 0)),
            pl.BlockSpec((1, GLA_K_WIDTH), lambda b, t: (0, 0)),
            pl.BlockSpec((1, GLA_DV), lambda b, t: (0, 0)),
        ],
        out_specs=pl.BlockSpec((GLA_ROWS, GLA_V_WIDTH),
                               lambda b, t: (b * steps + t, 0)),
        out_shape=jax.ShapeDtypeStruct((n, GLA_V_WIDTH), jnp.bfloat16),
        scratch_shapes=[pltpu.VMEM((GLA_HEADS, GLA_DK, GLA_DV), jnp.float32)],
        compiler_params=pltpu.CompilerParams(
            dimension_semantics=("arbitrary", "arbitrary"),
            vmem_limit_bytes=VMEM_LIMIT,
        ),
        name="gla",
    )(proj, proj, proj, proj, gk_code, up_pad, gk_bias, gla_norm_g)


def _key_window(g):
    lead = LEFT_CHUNKS // GROUP
    k_start = max(g - lead, 0) * GROUP_ROWS
    return k_start, (g + 1) * GROUP_ROWS - k_start


def _attn_scores(q_ref, k_ref, tbl_ref, g):
    k_start, n_keys = _key_window(g)
    q2 = q_ref[g * GROUP_ROWS:(g + 1) * GROUP_ROWS, :]
    lane = lax.broadcasted_iota(jnp.int32, q2.shape, 1)
    first = lane < ATT_DH
    zero = jnp.zeros_like(q2)
    qs = jnp.concatenate(
        [jnp.where(first, q2, zero), jnp.where(first, zero, q2)], axis=0)
    kw = k_ref[k_start:k_start + n_keys, :]
    s = lax.dot_general(kw, qs, _NT, preferred_element_type=jnp.float32)
    s = s + tbl_ref[0, WIN_ROWS - n_keys:, :]
    return s, jnp.max(s, axis=0, keepdims=True)


def _attn_output(vt_ref, g_ref, z_ref, p, g):
    k_start, n_keys = _key_window(g)
    vt_aug = jnp.concatenate(
        [vt_ref[:, k_start:k_start + n_keys],
         jnp.ones((BF16_SUBLANES, n_keys), jnp.bfloat16)], axis=0)
    ot = jnp.dot(vt_aug, p, preferred_element_type=jnp.float32)
    r = 1.0 / ot[LANES:LANES + 1, :]
    ot = jnp.concatenate(
        [ot[:ATT_DH, :GROUP_ROWS] * r[:, :GROUP_ROWS],
         ot[ATT_DH:LANES, GROUP_ROWS:] * r[:, GROUP_ROWS:]], axis=0)
    o = ot.T
    rows = slice(g * GROUP_ROWS, (g + 1) * GROUP_ROWS)
    gate = g_ref[rows, :].astype(jnp.float32)
    z_ref[rows, :] = (o * gate * _sigmoid(gate)).astype(jnp.bfloat16)


def _attn_kernel(q_ref, k_ref, vt_ref, g_ref, tbl_ref, z_ref, *, seq):
    n_groups = seq // GROUP_ROWS
    s, m = _attn_scores(q_ref, k_ref, tbl_ref, 0)
    for g in range(n_groups):
        if g + 1 < n_groups:
            s_next, m_next = _attn_scores(q_ref, k_ref, tbl_ref, g + 1)
        p = jnp.exp2(s - m).astype(jnp.bfloat16)
        _attn_output(vt_ref, g_ref, z_ref, p, g)
        if g + 1 < n_groups:
            s, m = s_next, m_next


def _attn(proj, v_t, table, batch, seq):
    n = proj.shape[0]
    pairs = ATT_HEADS // HEADS_PER_STEP
    col_map = lambda off: (lambda b, p: (b, off // LANES + p))
    return pl.pallas_call(
        functools.partial(_attn_kernel, seq=seq),
        grid=(batch, pairs),
        in_specs=[
            pl.BlockSpec((seq, LANES), col_map(QA_OFF)),
            pl.BlockSpec((seq, LANES), col_map(KA_OFF)),
            pl.BlockSpec((LANES, seq), lambda b, p: (p, b)),
            pl.BlockSpec((seq, LANES), col_map(GA_OFF)),
            pl.BlockSpec((1, WIN_ROWS, PAIR_COLS), lambda b, p: (p, 0, 0)),
        ],
        out_specs=pl.BlockSpec((seq, LANES), lambda b, p: (b, p)),
        out_shape=jax.ShapeDtypeStruct((n, ATT_WIDTH), jnp.bfloat16),
        compiler_params=pltpu.CompilerParams(
            dimension_semantics=("arbitrary", "arbitrary"),
            vmem_limit_bytes=VMEM_LIMIT,
        ),
        name="attn",
    )(proj, proj, v_t, proj, table)


def _bias_table(rel_bias):
    heads = rel_bias.shape[0]
    period = GROUP_ROWS + WIN_ROWS
    n_far = period - (2 * MAX_REL + 1)
    u = jnp.concatenate(
        [jnp.broadcast_to(rel_bias[:, 2 * MAX_REL:], (heads, n_far)),
         rel_bias[:, ::-1]], axis=1).astype(jnp.float32)
    skew = jnp.tile(u, (1, GROUP_ROWS))[:, :GROUP_ROWS * (period - 1)]
    skew = skew.reshape(heads, GROUP_ROWS, period - 1)
    toep = skew[:, :, GROUP_ROWS - 1:GROUP_ROWS - 1 + WIN_ROWS]
    i = np.arange(GROUP_ROWS)[:, None]
    j = np.arange(WIN_ROWS)[None, :]
    dc = j // CHUNK - i // CHUNK
    valid = (dc >= 0) & (dc <= LEFT_CHUNKS)
    tbl = jnp.where(valid[None], toep * LOG2_E, NEG_INF)
    tbl = tbl.reshape(heads // HEADS_PER_STEP, HEADS_PER_STEP, GROUP_ROWS, WIN_ROWS)
    return tbl.transpose(0, 3, 1, 2).reshape(
        heads // HEADS_PER_STEP, WIN_ROWS, PAIR_COLS)


def _merge_kernel(zg_ref, za_ref, lg_ref, la_ref, x_ref, wog_ref, woa_ref,
                  wout_ref, mb_ref, gp_ref, o_ref):
    yg = jnp.dot(zg_ref[...], wog_ref[...], preferred_element_type=jnp.float32)
    ya = jnp.dot(za_ref[...], woa_ref[...], preferred_element_type=jnp.float32)
    gate_g = _sigmoid(lg_ref[...].astype(jnp.float32) + mb_ref[0:1, :])
    gate_a = _sigmoid(la_ref[...].astype(jnp.float32) + mb_ref[1:2, :])
    merged = (gate_g * yg + gate_a * ya).astype(jnp.bfloat16)
    y = jnp.dot(merged, wout_ref[...], preferred_element_type=jnp.float32)
    ms = jnp.mean(y * y, axis=-1, keepdims=True)
    o_ref[...] = x_ref[...] + y * lax.rsqrt(ms + RMS_EPS) * gp_ref[...]


def _merge(z_gla, z_att, proj, x2, w_o_gla, w_o_att, w_out, merge_bias2, g_post):
    n = x2.shape[0]
    row = lambda i: (i, 0)
    const = lambda i: (0, 0)
    return pl.pallas_call(
        _merge_kernel,
        grid=(n // MERGE_TM,),
        in_specs=[
            pl.BlockSpec((MERGE_TM, D_MODEL), row),
            pl.BlockSpec((MERGE_TM, D_MODEL), row),
            pl.BlockSpec((MERGE_TM, D_MODEL), lambda i: (i, GATE_OFF // D_MODEL)),
            pl.BlockSpec((MERGE_TM, D_MODEL), lambda i: (i, GATE_OFF // D_MODEL + 1)),
            pl.BlockSpec((MERGE_TM, D_MODEL), row),
            pl.BlockSpec((D_MODEL, D_MODEL), const),
            pl.BlockSpec((D_MODEL, D_MODEL), const),
            pl.BlockSpec((D_MODEL, D_MODEL), const),
            pl.BlockSpec((2, D_MODEL), const),
            pl.BlockSpec((1, D_MODEL), const),
        ],
        out_specs=pl.BlockSpec((MERGE_TM, D_MODEL), row),
        out_shape=jax.ShapeDtypeStruct((n, D_MODEL), jnp.float32),
        compiler_params=pltpu.CompilerParams(
            dimension_semantics=("arbitrary",),
            vmem_limit_bytes=VMEM_LIMIT,
        ),
        name="merge",
    )(z_gla, z_att, proj, proj, x2, w_o_gla, w_o_att, w_out, merge_bias2, g_post)


def kernel(x, norm_pre_g, w_in, gk_up, gk_bias, gla_norm_g, rel_bias, w_o_gla,
           w_o_att, merge_bias, w_out, norm_post_g):
    batch, seq, d = x.shape
    assert d == D_MODEL and seq % GLA_ROWS == 0 and seq % GROUP_ROWS == 0
    assert (batch * seq) % INPROJ_TM == 0 and seq % INPROJ_TM == 0
    assert rel_bias.shape == (ATT_HEADS, 2 * MAX_REL + 1)
    bf16 = jnp.bfloat16

    splits = np.cumsum([GLA_K_WIDTH, GLA_K_WIDTH, GLA_V_WIDTH, GLA_V_WIDTH, GK_RANK,
                        ATT_WIDTH, ATT_WIDTH, ATT_WIDTH, ATT_WIDTH])
    (w_qg, w_kg, w_vg, w_gg, w_code, w_qa, w_ka, w_va, w_ga, w_gate) = jnp.split(
        w_in, [int(s) for s in splits], axis=1)
    w_main = jnp.concatenate(
        [w_qa * (ATT_DH ** -0.5 * LOG2_E), w_ka, w_ga, w_gate, w_vg, w_gg, w_qg, w_kg],
        axis=1).astype(bf16)
    w_vt = w_va.T.astype(bf16)
    w_gk = jnp.pad(w_code, ((0, 0), (0, GK_PAD - GK_RANK))).astype(bf16)
    up_pad = jnp.pad(gk_up, ((0, GK_PAD - GK_RANK), (0, 0))).astype(bf16)

    x2 = x.reshape(batch * seq, D_MODEL)
    proj, v_t, gk_code = _inproj(x2, norm_pre_g.reshape(1, D_MODEL), w_main, w_vt, w_gk)
    z_gla = _gla(proj, gk_code, up_pad, gk_bias.reshape(1, GLA_K_WIDTH),
                 gla_norm_g.reshape(1, GLA_DV), batch, seq)
    z_att = _attn(proj, v_t, _bias_table(rel_bias), batch, seq)
    out = _merge(z_gla, z_att, proj, x2, w_o_gla.astype(bf16), w_o_att.astype(bf16),
                 w_out.astype(bf16), merge_bias.reshape(2, D_MODEL),
                 norm_post_g.reshape(1, D_MODEL))
    return out.reshape(batch, seq, D_MODEL)
```

```python
import functools

import numpy as np
import jax
import jax.numpy as jnp
from jax import lax
from jax.experimental import pallas as pl
from jax.experimental.pallas import tpu as pltpu

D_MODEL = 1024
CHUNK = 64
GLA_HEADS = 4
GLA_DK = 128
GLA_DV = 256
GLA_K_WIDTH = GLA_HEADS * GLA_DK
GLA_V_WIDTH = GLA_HEADS * GLA_DV
GK_RANK = 16
GATE_NORMALIZER = 16.0
ATT_HEADS = 16
ATT_DH = 64
ATT_WIDTH = ATT_HEADS * ATT_DH
LEFT_CHUNKS = 8
MAX_REL = 256
RMS_EPS = 1e-6
NEG_INF = -1e30

LANES = 128
BF16_SUBLANES = 16
LOG2_E = float(np.log2(np.e))
LN_2 = float(np.log(2.0))

QA_OFF = 0
KA_OFF = 1024
GA_OFF = 2048
GATE_OFF = 3072
VG_OFF = 5120
GG_OFF = 6144
QG_OFF = 7168
KG_OFF = 7680
PROJ_COLS = 8192
GK_PAD = LANES

GROUP = 4
GROUP_ROWS = GROUP * CHUNK
WIN_CHUNKS = GROUP + LEFT_CHUNKS
WIN_ROWS = WIN_CHUNKS * CHUNK
HEADS_PER_STEP = LANES // ATT_DH
PAIR_COLS = HEADS_PER_STEP * GROUP_ROWS

INPROJ_TM = 512
INPROJ_TN = 1024
GLA_ROWS = 512
MERGE_TM = 512
VMEM_LIMIT = 56 * 1024 * 1024

_NT = (((1,), (1,)), ((), ()))


def _sigmoid(x):
    return 1.0 / (1.0 + jnp.exp2(x * (-LOG2_E)))


def _inproj_kernel(x_ref, g_ref, w_ref, wvt_ref, wgk_ref, proj_ref, vt_ref, gk_ref):
    x = x_ref[...]
    ms = jnp.mean(x * x, axis=-1, keepdims=True)
    h = (x * lax.rsqrt(ms + RMS_EPS) * g_ref[...]).astype(jnp.bfloat16)
    gk_ref[...] = jnp.dot(
        h, wgk_ref[...], preferred_element_type=jnp.float32).astype(jnp.bfloat16)
    for j in range(PROJ_COLS // INPROJ_TN):
        cols = slice(j * INPROJ_TN, (j + 1) * INPROJ_TN)
        proj_ref[:, cols] = jnp.dot(
            h, w_ref[:, cols], preferred_element_type=jnp.float32
        ).astype(jnp.bfloat16)
    vt_ref[...] = lax.dot_general(
        wvt_ref[...], h, _NT, preferred_element_type=jnp.float32
    ).astype(jnp.bfloat16)


def _inproj(x2, g_pre, w_main, w_vt, w_gk):
    n = x2.shape[0]
    const = lambda i: (0, 0)
    resident = pl.Buffered(1)
    return pl.pallas_call(
        _inproj_kernel,
        grid=(n // INPROJ_TM,),
        in_specs=[
            pl.BlockSpec((INPROJ_TM, D_MODEL), lambda i: (i, 0)),
            pl.BlockSpec((1, D_MODEL), const),
            pl.BlockSpec((D_MODEL, PROJ_COLS), const, pipeline_mode=resident),
            pl.BlockSpec((ATT_WIDTH, D_MODEL), const, pipeline_mode=resident),
            pl.BlockSpec((D_MODEL, GK_PAD), const, pipeline_mode=resident),
        ],
        out_specs=[
            pl.BlockSpec((INPROJ_TM, PROJ_COLS), lambda i: (i, 0)),
            pl.BlockSpec((ATT_WIDTH, INPROJ_TM), lambda i: (0, i)),
            pl.BlockSpec((INPROJ_TM, GK_PAD), lambda i: (i, 0)),
        ],
        out_shape=[
            jax.ShapeDtypeStruct((n, PROJ_COLS), jnp.bfloat16),
            jax.ShapeDtypeStruct((ATT_WIDTH, n), jnp.bfloat16),
            jax.ShapeDtypeStruct((n, GK_PAD), jnp.bfloat16),
        ],
        compiler_params=pltpu.CompilerParams(
            dimension_semantics=("arbitrary",),
            vmem_limit_bytes=VMEM_LIMIT,
        ),
        name="inproj",
    )(x2, g_pre, w_main, w_vt, w_gk)


def _chunk_cumsum(x):
    n_chunks = x.shape[0] // CHUNK
    hi = x.astype(jnp.bfloat16)
    lo = (x - hi.astype(jnp.float32)).astype(jnp.bfloat16)
    r = lax.broadcasted_iota(jnp.int32, (CHUNK, 2 * CHUNK), 0)
    c = lax.broadcasted_iota(jnp.int32, (CHUNK, 2 * CHUNK), 1)
    tri = (r >= jnp.bitwise_and(c, CHUNK - 1)).astype(jnp.bfloat16)
    out = []
    for n in range(n_chunks):
        rows = slice(n * CHUNK, (n + 1) * CHUNK)
        terms = jnp.concatenate([hi[rows], lo[rows]], axis=0)
        out.append(jnp.dot(tri, terms, preferred_element_type=jnp.float32))
    return jnp.stack(out, axis=0)


def _gla_kernel(q_ref, k_ref, v_ref, g_ref, code_ref, up_ref, bias_ref,
                z_ref, state_ref):
    n_chunks = GLA_ROWS // CHUNK

    @pl.when(pl.program_id(1) == 0)
    def _():
        state_ref[...] = jnp.zeros_like(state_ref)

    logits = jnp.dot(code_ref[...], up_ref[...],
                     preferred_element_type=jnp.float32) + bias_ref[...]
    gk = (jnp.minimum(logits, 0.0) * (1.0 / GATE_NORMALIZER)
          - jnp.log2(1.0 + jnp.exp2(jnp.abs(logits) * (-LOG2_E)))
          * (LN_2 / GATE_NORMALIZER))
    a_cum3 = _chunk_cumsum(gk)
    a_end = a_cum3[:, CHUNK - 1, :]
    k_dec = (k_ref[...].astype(jnp.float32).reshape(n_chunks, CHUNK, GLA_K_WIDTH)
             * jnp.exp(a_end[:, None, :] - a_cum3)).astype(jnp.bfloat16)
    decay_t = jnp.exp(a_end).T
    kcols = [slice(h * GLA_DK, (h + 1) * GLA_DK) for h in range(GLA_HEADS)]
    vcols = [slice(h * GLA_DV, (h + 1) * GLA_DV) for h in range(GLA_HEADS)]

    def chunk_rows(c):
        return slice(c * CHUNK, (c + 1) * CHUNK)

    def increments(c):
        return [lax.dot_general(
            k_dec[c][:, kcols[h]], v_ref[chunk_rows(c), vcols[h]],
            (((0,), (0,)), ((), ())), preferred_element_type=jnp.float32)
            for h in range(GLA_HEADS)]

    def finish(c, reads):
        for h in range(GLA_HEADS):
            o = reads[h]
            ms = jnp.mean(o * o, axis=-1, keepdims=True)
            factor = lax.rsqrt(ms + RMS_EPS * GLA_DK)
            gate = g_ref[chunk_rows(c), vcols[h]].astype(jnp.float32)
            z_ref[chunk_rows(c), vcols[h]] = (
                (o * factor) * (gate * _sigmoid(gate))).astype(jnp.bfloat16)

    upd_next = increments(0)
    reads_prev = None
    for c in range(n_chunks):
        upd = upd_next
        if c + 1 < n_chunks:
            upd_next = increments(c + 1)
        reads = []
        for h in range(GLA_HEADS):
            state = decay_t[kcols[h], c:c + 1] * state_ref[h] + upd[h]
            state_ref[h] = state
            reads.append(jnp.dot(q_ref[chunk_rows(c), kcols[h]],
                                 state.astype(jnp.bfloat16),
                                 preferred_element_type=jnp.float32))
        if reads_prev is not None:
            finish(c - 1, reads_prev)
        reads_prev = reads
    finish(n_chunks - 1, reads_prev)


def _gla(proj, gk_code, up_pad, gk_bias, batch, seq):
    n = proj.shape[0]
    steps = seq // GLA_ROWS
    row_map = lambda off, width: (lambda b, t: (b * steps + t, off // width))
    return pl.pallas_call(
        _gla_kernel,
        grid=(batch, steps),
        in_specs=[
            pl.BlockSpec((GLA_ROWS, GLA_K_WIDTH), row_map(QG_OFF, GLA_K_WIDTH)),
            pl.BlockSpec((GLA_ROWS, GLA_K_WIDTH), row_map(KG_OFF, GLA_K_WIDTH)),
            pl.BlockSpec((GLA_ROWS, GLA_V_WIDTH), row_map(VG_OFF, GLA_V_WIDTH)),
            pl.BlockSpec((GLA_ROWS, GLA_V_WIDTH), row_map(GG_OFF, GLA_V_WIDTH)),
            pl.BlockSpec((GLA_ROWS, GK_PAD), lambda b, t: (b * steps + t, 0)),
            pl.BlockSpec((GK_PAD, GLA_K_WIDTH), lambda b, t: (0, 0)),
            pl.BlockSpec((1, GLA_K_WIDTH), lambda b, t: (0, 0)),
        ],
        out_specs=pl.BlockSpec((GLA_ROWS, GLA_V_WIDTH),
                               lambda b, t: (b * steps + t, 0)),
        out_shape=jax.ShapeDtypeStruct((n, GLA_V_WIDTH), jnp.bfloat16),
        scratch_shapes=[pltpu.VMEM((GLA_HEADS, GLA_DK, GLA_DV), jnp.float32)],
        compiler_params=pltpu.CompilerParams(
            dimension_semantics=("arbitrary", "arbitrary"),
            vmem_limit_bytes=VMEM_LIMIT,
        ),
        name="gla",
    )(proj, proj, proj, proj, gk_code, up_pad, gk_bias)


def _key_window(g):
    lead = LEFT_CHUNKS // GROUP
    k_start = max(g - lead, 0) * GROUP_ROWS
    return k_start, (g + 1) * GROUP_ROWS - k_start


def _attn_scores(q_ref, k_ref, tbl_ref, g):
    k_start, n_keys = _key_window(g)
    q2 = q_ref[g * GROUP_ROWS:(g + 1) * GROUP_ROWS, :]
    lane = lax.broadcasted_iota(jnp.int32, q2.shape, 1)
    first = lane < ATT_DH
    zero = jnp.zeros_like(q2)
    qs = jnp.concatenate(
        [jnp.where(first, q2, zero), jnp.where(first, zero, q2)], axis=0)
    kw = k_ref[k_start:k_start + n_keys, :]
    s = lax.dot_general(kw, qs, _NT, preferred_element_type=jnp.float32)
    s = s + tbl_ref[0, WIN_ROWS - n_keys:, :]
    return s, jnp.max(s, axis=0, keepdims=True)


def _attn_output(vt_ref, g_ref, z_ref, p, g):
    k_start, n_keys = _key_window(g)
    vt_aug = jnp.concatenate(
        [vt_ref[:, k_start:k_start + n_keys],
         jnp.ones((BF16_SUBLANES, n_keys), jnp.bfloat16)], axis=0)
    ot = jnp.dot(vt_aug, p, preferred_element_type=jnp.float32)
    r = 1.0 / ot[LANES:LANES + 1, :]
    ot = jnp.concatenate(
        [ot[:ATT_DH, :GROUP_ROWS] * r[:, :GROUP_ROWS],
         ot[ATT_DH:LANES, GROUP_ROWS:] * r[:, GROUP_ROWS:]], axis=0)
    o = ot.T
    rows = slice(g * GROUP_ROWS, (g + 1) * GROUP_ROWS)
    gate = g_ref[rows, :].astype(jnp.float32)
    z_ref[rows, :] = (o * gate * _sigmoid(gate)).astype(jnp.bfloat16)


def _attn_kernel(q_ref, k_ref, vt_ref, g_ref, tbl_ref, z_ref, *, seq):
    n_groups = seq // GROUP_ROWS
    s, m = _attn_scores(q_ref, k_ref, tbl_ref, 0)
    for g in range(n_groups):
        if g + 1 < n_groups:
            s_next, m_next = _attn_scores(q_ref, k_ref, tbl_ref, g + 1)
        p = jnp.exp2(s - m).astype(jnp.bfloat16)
        _attn_output(vt_ref, g_ref, z_ref, p, g)
        if g + 1 < n_groups:
            s, m = s_next, m_next


def _attn(proj, v_t, table, batch, seq):
    n = proj.shape[0]
    pairs = ATT_HEADS // HEADS_PER_STEP
    col_map = lambda off: (lambda b, p: (b, off // LANES + p))
    return pl.pallas_call(
        functools.partial(_attn_kernel, seq=seq),
        grid=(batch, pairs),
        in_specs=[
            pl.BlockSpec((seq, LANES), col_map(QA_OFF)),
            pl.BlockSpec((seq, LANES), col_map(KA_OFF)),
            pl.BlockSpec((LANES, seq), lambda b, p: (p, b)),
            pl.BlockSpec((seq, LANES), col_map(GA_OFF)),
            pl.BlockSpec((1, WIN_ROWS, PAIR_COLS), lambda b, p: (p, 0, 0)),
        ],
        out_specs=pl.BlockSpec((seq, LANES), lambda b, p: (b, p)),
        out_shape=jax.ShapeDtypeStruct((n, ATT_WIDTH), jnp.bfloat16),
        compiler_params=pltpu.CompilerParams(
            dimension_semantics=("arbitrary", "arbitrary"),
            vmem_limit_bytes=VMEM_LIMIT,
        ),
        name="attn",
    )(proj, proj, v_t, proj, table)


def _bias_table(rel_bias):
    heads = rel_bias.shape[0]
    period = GROUP_ROWS + WIN_ROWS
    n_far = period - (2 * MAX_REL + 1)
    u = jnp.concatenate(
        [jnp.broadcast_to(rel_bias[:, 2 * MAX_REL:], (heads, n_far)),
         rel_bias[:, ::-1]], axis=1).astype(jnp.float32)
    skew = jnp.tile(u, (1, GROUP_ROWS))[:, :GROUP_ROWS * (period - 1)]
    skew = skew.reshape(heads, GROUP_ROWS, period - 1)
    toep = skew[:, :, GROUP_ROWS - 1:GROUP_ROWS - 1 + WIN_ROWS]
    i = np.arange(GROUP_ROWS)[:, None]
    j = np.arange(WIN_ROWS)[None, :]
    dc = j // CHUNK - i // CHUNK
    valid = (dc >= 0) & (dc <= LEFT_CHUNKS)
    tbl = jnp.where(valid[None], toep * LOG2_E, NEG_INF)
    tbl = tbl.reshape(heads // HEADS_PER_STEP, HEADS_PER_STEP, GROUP_ROWS, WIN_ROWS)
    return tbl.transpose(0, 3, 1, 2).reshape(
        heads // HEADS_PER_STEP, WIN_ROWS, PAIR_COLS)


def _merge_kernel(zg_ref, za_ref, lg_ref, la_ref, x_ref, wog_ref, woa_ref,
                  wout_ref, mb_ref, gp_ref, o_ref):
    yg = jnp.dot(zg_ref[...], wog_ref[...], preferred_element_type=jnp.float32)
    ya = jnp.dot(za_ref[...], woa_ref[...], preferred_element_type=jnp.float32)
    gate_g = _sigmoid(lg_ref[...].astype(jnp.float32) + mb_ref[0:1, :])
    gate_a = _sigmoid(la_ref[...].astype(jnp.float32) + mb_ref[1:2, :])
    merged = (gate_g * yg + gate_a * ya).astype(jnp.bfloat16)
    y = jnp.dot(merged, wout_ref[...], preferred_element_type=jnp.float32)
    ms = jnp.mean(y * y, axis=-1, keepdims=True)
    o_ref[...] = x_ref[...] + y * lax.rsqrt(ms + RMS_EPS) * gp_ref[...]


def _merge(z_gla, z_att, proj, x2, w_o_gla, w_o_att, w_out, merge_bias2, g_post):
    n = x2.shape[0]
    row = lambda i: (i, 0)
    const = lambda i: (0, 0)
    return pl.pallas_call(
        _merge_kernel,
        grid=(n // MERGE_TM,),
        in_specs=[
            pl.BlockSpec((MERGE_TM, D_MODEL), row),
            pl.BlockSpec((MERGE_TM, D_MODEL), row),
            pl.BlockSpec((MERGE_TM, D_MODEL), lambda i: (i, GATE_OFF // D_MODEL)),
            pl.BlockSpec((MERGE_TM, D_MODEL), lambda i: (i, GATE_OFF // D_MODEL + 1)),
            pl.BlockSpec((MERGE_TM, D_MODEL), row),
            pl.BlockSpec((D_MODEL, D_MODEL), const),
            pl.BlockSpec((D_MODEL, D_MODEL), const),
            pl.BlockSpec((D_MODEL, D_MODEL), const),
            pl.BlockSpec((2, D_MODEL), const),
            pl.BlockSpec((1, D_MODEL), const),
        ],
        out_specs=pl.BlockSpec((MERGE_TM, D_MODEL), row),
        out_shape=jax.ShapeDtypeStruct((n, D_MODEL), jnp.float32),
        compiler_params=pltpu.CompilerParams(
            dimension_semantics=("arbitrary",),
            vmem_limit_bytes=VMEM_LIMIT,
        ),
        name="merge",
    )(z_gla, z_att, proj, proj, x2, w_o_gla, w_o_att, w_out, merge_bias2, g_post)


def kernel(x, norm_pre_g, w_in, gk_up, gk_bias, gla_norm_g, rel_bias, w_o_gla,
           w_o_att, merge_bias, w_out, norm_post_g):
    batch, seq, d = x.shape
    assert d == D_MODEL and seq % GLA_ROWS == 0 and seq % GROUP_ROWS == 0
    assert (batch * seq) % INPROJ_TM == 0 and seq % INPROJ_TM == 0
    assert rel_bias.shape == (ATT_HEADS, 2 * MAX_REL + 1)
    bf16 = jnp.bfloat16

    splits = np.cumsum([GLA_K_WIDTH, GLA_K_WIDTH, GLA_V_WIDTH, GLA_V_WIDTH, GK_RANK,
                        ATT_WIDTH, ATT_WIDTH, ATT_WIDTH, ATT_WIDTH])
    (w_qg, w_kg, w_vg, w_gg, w_code, w_qa, w_ka, w_va, w_ga, w_gate) = jnp.split(
        w_in, [int(s) for s in splits], axis=1)
    w_main = jnp.concatenate(
        [w_qa * (ATT_DH ** -0.5 * LOG2_E), w_ka, w_ga, w_gate, w_vg, w_gg, w_qg, w_kg],
        axis=1).astype(bf16)
    w_vt = w_va.T.astype(bf16)
    w_gk = jnp.pad(w_code, ((0, 0), (0, GK_PAD - GK_RANK))).astype(bf16)
    up_pad = jnp.pad(gk_up, ((0, GK_PAD - GK_RANK), (0, 0))).astype(bf16)

    x2 = x.reshape(batch * seq, D_MODEL)
    proj, v_t, gk_code = _inproj(x2, norm_pre_g.reshape(1, D_MODEL), w_main, w_vt, w_gk)
    z_gla = _gla(proj, gk_code, up_pad, gk_bias.reshape(1, GLA_K_WIDTH), batch, seq)
    z_att = _attn(proj, v_t, _bias_table(rel_bias), batch, seq)
    w_og = (jnp.tile(gla_norm_g, GLA_HEADS)[:, None] * w_o_gla).astype(bf16)
    out = _merge(z_gla, z_att, proj, x2, w_og, w_o_att.astype(bf16),
                 w_out.astype(bf16), merge_bias.reshape(2, D_MODEL),
                 norm_post_g.reshape(1, D_MODEL))
    return out.reshape(batch, seq, D_MODEL)
```

```python
import functools

import numpy as np
import jax
import jax.numpy as jnp
from jax import lax
from jax.experimental import pallas as pl
from jax.experimental.pallas import tpu as pltpu

D_MODEL = 1024
CHUNK = 64
GLA_HEADS = 4
GLA_DK = 128
GLA_DV = 256
GLA_K_WIDTH = GLA_HEADS * GLA_DK
GLA_V_WIDTH = GLA_HEADS * GLA_DV
GK_RANK = 16
GATE_NORMALIZER = 16.0
ATT_HEADS = 16
ATT_DH = 64
ATT_WIDTH = ATT_HEADS * ATT_DH
LEFT_CHUNKS = 8
MAX_REL = 256
RMS_EPS = 1e-6
NEG_INF = -1e30

LANES = 128
BF16_SUBLANES = 16
LOG2_E = float(np.log2(np.e))
LN_2 = float(np.log(2.0))

QA_OFF = 0
KA_OFF = 1024
GA_OFF = 2048
GATE_OFF = 3072
VG_OFF = 5120
GG_OFF = 6144
QG_OFF = 7168
KG_OFF = 7680
PROJ_COLS = 8192
GK_PAD = LANES

GROUP = 4
GROUP_ROWS = GROUP * CHUNK
WIN_CHUNKS = GROUP + LEFT_CHUNKS
WIN_ROWS = WIN_CHUNKS * CHUNK
HEADS_PER_STEP = LANES // ATT_DH
PAIR_COLS = HEADS_PER_STEP * GROUP_ROWS
TABLE_PERIOD = GROUP_ROWS + WIN_ROWS

INPROJ_TM = 512
INPROJ_TN = 1024
GLA_ROWS = 512
MERGE_TM = 512
VMEM_LIMIT = 56 * 1024 * 1024

_NT = (((1,), (1,)), ((), ()))


def _sigmoid(x):
    return 1.0 / (1.0 + jnp.exp2(x * (-LOG2_E)))


def _inproj_kernel(x_ref, g_ref, w_ref, wvt_ref, wgk_ref, proj_ref, vt_ref, gk_ref):
    x = x_ref[...]
    ms = jnp.mean(x * x, axis=-1, keepdims=True)
    h = (x * lax.rsqrt(ms + RMS_EPS) * g_ref[...]).astype(jnp.bfloat16)
    gk_ref[...] = jnp.dot(
        h, wgk_ref[...], preferred_element_type=jnp.float32).astype(jnp.bfloat16)
    for j in range(PROJ_COLS // INPROJ_TN):
        cols = slice(j * INPROJ_TN, (j + 1) * INPROJ_TN)
        proj_ref[:, cols] = jnp.dot(
            h, w_ref[:, cols], preferred_element_type=jnp.float32
        ).astype(jnp.bfloat16)
    vt_ref[...] = lax.dot_general(
        wvt_ref[...], h, _NT, preferred_element_type=jnp.float32
    ).astype(jnp.bfloat16)


def _inproj(x2, g_pre, w_main, w_vt, w_gk):
    n = x2.shape[0]
    const = lambda i: (0, 0)
    resident = pl.Buffered(1)
    return pl.pallas_call(
        _inproj_kernel,
        grid=(n // INPROJ_TM,),
        in_specs=[
            pl.BlockSpec((INPROJ_TM, D_MODEL), lambda i: (i, 0)),
            pl.BlockSpec((1, D_MODEL), const),
            pl.BlockSpec((D_MODEL, PROJ_COLS), const, pipeline_mode=resident),
            pl.BlockSpec((ATT_WIDTH, D_MODEL), const, pipeline_mode=resident),
            pl.BlockSpec((D_MODEL, GK_PAD), const, pipeline_mode=resident),
        ],
        out_specs=[
            pl.BlockSpec((INPROJ_TM, PROJ_COLS), lambda i: (i, 0)),
            pl.BlockSpec((ATT_WIDTH, INPROJ_TM), lambda i: (0, i)),
            pl.BlockSpec((INPROJ_TM, GK_PAD), lambda i: (i, 0)),
        ],
        out_shape=[
            jax.ShapeDtypeStruct((n, PROJ_COLS), jnp.bfloat16),
            jax.ShapeDtypeStruct((ATT_WIDTH, n), jnp.bfloat16),
            jax.ShapeDtypeStruct((n, GK_PAD), jnp.bfloat16),
        ],
        compiler_params=pltpu.CompilerParams(
            dimension_semantics=("arbitrary",),
            vmem_limit_bytes=VMEM_LIMIT,
        ),
        name="inproj",
    )(x2, g_pre, w_main, w_vt, w_gk)


def _chunk_cumsum(x):
    n_chunks = x.shape[0] // CHUNK
    hi = x.astype(jnp.bfloat16)
    lo = (x - hi.astype(jnp.float32)).astype(jnp.bfloat16)
    r = lax.broadcasted_iota(jnp.int32, (CHUNK, 2 * CHUNK), 0)
    c = lax.broadcasted_iota(jnp.int32, (CHUNK, 2 * CHUNK), 1)
    tri = (r >= jnp.bitwise_and(c, CHUNK - 1)).astype(jnp.bfloat16)
    out = []
    for n in range(n_chunks):
        rows = slice(n * CHUNK, (n + 1) * CHUNK)
        terms = jnp.concatenate([hi[rows], lo[rows]], axis=0)
        out.append(jnp.dot(tri, terms, preferred_element_type=jnp.float32))
    return jnp.stack(out, axis=0)


def _gla_kernel(q_ref, k_ref, v_ref, g_ref, code_ref, up_ref, bias_ref,
                z_ref, state_ref):
    n_chunks = GLA_ROWS // CHUNK

    @pl.when(pl.program_id(1) == 0)
    def _():
        state_ref[...] = jnp.zeros_like(state_ref)

    logits = jnp.dot(code_ref[...], up_ref[...],
                     preferred_element_type=jnp.float32) + bias_ref[...]
    gk = (jnp.minimum(logits, 0.0) * (1.0 / GATE_NORMALIZER)
          - jnp.log2(1.0 + jnp.exp2(jnp.abs(logits) * (-LOG2_E)))
          * (LN_2 / GATE_NORMALIZER))
    a_cum3 = _chunk_cumsum(gk)
    a_end = a_cum3[:, CHUNK - 1, :]
    k_dec = (k_ref[...].astype(jnp.float32).reshape(n_chunks, CHUNK, GLA_K_WIDTH)
             * jnp.exp(a_end[:, None, :] - a_cum3)).astype(jnp.bfloat16)
    decay_t = jnp.exp(a_end).T
    kcols = [slice(h * GLA_DK, (h + 1) * GLA_DK) for h in range(GLA_HEADS)]
    vcols = [slice(h * GLA_DV, (h + 1) * GLA_DV) for h in range(GLA_HEADS)]

    def chunk_rows(c):
        return slice(c * CHUNK, (c + 1) * CHUNK)

    def increments(c):
        return [lax.dot_general(
            k_dec[c][:, kcols[h]], v_ref[chunk_rows(c), vcols[h]],
            (((0,), (0,)), ((), ())), preferred_element_type=jnp.float32)
            for h in range(GLA_HEADS)]

    def finish(c, reads):
        for h in range(GLA_HEADS):
            o = reads[h]
            ms = jnp.mean(o * o, axis=-1, keepdims=True)
            factor = lax.rsqrt(ms + RMS_EPS * GLA_DK)
            gate = g_ref[chunk_rows(c), vcols[h]].astype(jnp.float32)
            z_ref[chunk_rows(c), vcols[h]] = (
                (o * factor) * (gate * _sigmoid(gate))).astype(jnp.bfloat16)

    upd_next = increments(0)
    reads_prev = None
    for c in range(n_chunks):
        upd = upd_next
        if c + 1 < n_chunks:
            upd_next = increments(c + 1)
        reads = []
        for h in range(GLA_HEADS):
            state = decay_t[kcols[h], c:c + 1] * state_ref[h] + upd[h]
            state_ref[h] = state
            reads.append(jnp.dot(q_ref[chunk_rows(c), kcols[h]],
                                 state.astype(jnp.bfloat16),
                                 preferred_element_type=jnp.float32))
        if reads_prev is not None:
            finish(c - 1, reads_prev)
        reads_prev = reads
    finish(n_chunks - 1, reads_prev)


def _gla(proj, gk_code, up_pad, gk_bias, batch, seq):
    n = proj.shape[0]
    steps = seq // GLA_ROWS
    row_map = lambda off, width: (lambda b, t: (b * steps + t, off // width))
    return pl.pallas_call(
        _gla_kernel,
        grid=(batch, steps),
        in_specs=[
            pl.BlockSpec((GLA_ROWS, GLA_K_WIDTH), row_map(QG_OFF, GLA_K_WIDTH)),
            pl.BlockSpec((GLA_ROWS, GLA_K_WIDTH), row_map(KG_OFF, GLA_K_WIDTH)),
            pl.BlockSpec((GLA_ROWS, GLA_V_WIDTH), row_map(VG_OFF, GLA_V_WIDTH)),
            pl.BlockSpec((GLA_ROWS, GLA_V_WIDTH), row_map(GG_OFF, GLA_V_WIDTH)),
            pl.BlockSpec((GLA_ROWS, GK_PAD), lambda b, t: (b * steps + t, 0)),
            pl.BlockSpec((GK_PAD, GLA_K_WIDTH), lambda b, t: (0, 0)),
            pl.BlockSpec((1, GLA_K_WIDTH), lambda b, t: (0, 0)),
        ],
        out_specs=pl.BlockSpec((GLA_ROWS, GLA_V_WIDTH),
                               lambda b, t: (b * steps + t, 0)),
        out_shape=jax.ShapeDtypeStruct((n, GLA_V_WIDTH), jnp.bfloat16),
        scratch_shapes=[pltpu.VMEM((GLA_HEADS, GLA_DK, GLA_DV), jnp.float32)],
        compiler_params=pltpu.CompilerParams(
            dimension_semantics=("arbitrary", "arbitrary"),
            vmem_limit_bytes=VMEM_LIMIT,
        ),
        name="gla",
    )(proj, proj, proj, proj, gk_code, up_pad, gk_bias)


def _key_window(g):
    lead = LEFT_CHUNKS // GROUP
    k_start = max(g - lead, 0) * GROUP_ROWS
    return k_start, (g + 1) * GROUP_ROWS - k_start


def _attn_scores(q_ref, k_ref, tbl_ref, g):
    k_start, n_keys = _key_window(g)
    q2 = q_ref[g * GROUP_ROWS:(g + 1) * GROUP_ROWS, :]
    lane = lax.broadcasted_iota(jnp.int32, q2.shape, 1)
    first = lane < ATT_DH
    zero = jnp.zeros_like(q2)
    qs = jnp.concatenate(
        [jnp.where(first, q2, zero), jnp.where(first, zero, q2)], axis=0)
    kw = k_ref[k_start:k_start + n_keys, :]
    s = lax.dot_general(kw, qs, _NT, preferred_element_type=jnp.float32)
    s = s + tbl_ref[0, WIN_ROWS - n_keys:, :]
    return s, jnp.max(s, axis=0, keepdims=True)


def _attn_output(vt_ref, g_ref, z_ref, p, g):
    k_start, n_keys = _key_window(g)
    vt_aug = jnp.concatenate(
        [vt_ref[:, k_start:k_start + n_keys],
         jnp.ones((BF16_SUBLANES, n_keys), jnp.bfloat16)], axis=0)
    ot = jnp.dot(vt_aug, p, preferred_element_type=jnp.float32)
    r = 1.0 / ot[LANES:LANES + 1, :]
    ot = jnp.concatenate(
        [ot[:ATT_DH, :GROUP_ROWS] * r[:, :GROUP_ROWS],
         ot[ATT_DH:LANES, GROUP_ROWS:] * r[:, GROUP_ROWS:]], axis=0)
    o = ot.T
    rows = slice(g * GROUP_ROWS, (g + 1) * GROUP_ROWS)
    gate = g_ref[rows, :].astype(jnp.float32)
    z_ref[rows, :] = (o * gate * _sigmoid(gate)).astype(jnp.bfloat16)


def _attn_kernel(q_ref, k_ref, vt_ref, g_ref, tbl_ref, z_ref, *, seq):
    n_groups = seq // GROUP_ROWS
    s, m = _attn_scores(q_ref, k_ref, tbl_ref, 0)
    for g in range(n_groups):
        if g + 1 < n_groups:
            s_next, m_next = _attn_scores(q_ref, k_ref, tbl_ref, g + 1)
        p = jnp.exp2(s - m).astype(jnp.bfloat16)
        _attn_output(vt_ref, g_ref, z_ref, p, g)
        if g + 1 < n_groups:
            s, m = s_next, m_next


def _attn(proj, v_t, table, batch, seq):
    n = proj.shape[0]
    pairs = ATT_HEADS // HEADS_PER_STEP
    col_map = lambda off: (lambda p, b: (b, off // LANES + p))
    return pl.pallas_call(
        functools.partial(_attn_kernel, seq=seq),
        grid=(pairs, batch),
        in_specs=[
            pl.BlockSpec((seq, LANES), col_map(QA_OFF)),
            pl.BlockSpec((seq, LANES), col_map(KA_OFF)),
            pl.BlockSpec((LANES, seq), lambda p, b: (p, b)),
            pl.BlockSpec((seq, LANES), col_map(GA_OFF)),
            pl.BlockSpec((1, WIN_ROWS, PAIR_COLS), lambda p, b: (p, 0, 0)),
        ],
        out_specs=pl.BlockSpec((seq, LANES), lambda p, b: (b, p)),
        out_shape=jax.ShapeDtypeStruct((n, ATT_WIDTH), jnp.bfloat16),
        compiler_params=pltpu.CompilerParams(
            dimension_semantics=("arbitrary", "arbitrary"),
            vmem_limit_bytes=VMEM_LIMIT,
        ),
        name="attn",
    )(proj, proj, v_t, proj, table)


def _bias_table(rel_bias):
    heads = rel_bias.shape[0]
    pairs = heads // HEADS_PER_STEP
    far = rel_bias[:, 2 * MAX_REL:]
    w = jnp.concatenate(
        [jnp.broadcast_to(far, (heads, GROUP_ROWS)), rel_bias,
         jnp.broadcast_to(far, (heads, TABLE_PERIOD - GROUP_ROWS - (2 * MAX_REL + 1)))],
        axis=1).astype(jnp.float32) * LOG2_E
    return pl.pallas_call(
        _bias_table_kernel,
        grid=(pairs,),
        in_specs=[pl.BlockSpec((HEADS_PER_STEP, 1, TABLE_PERIOD), lambda p: (p, 0, 0))],
        out_specs=pl.BlockSpec((1, WIN_ROWS, PAIR_COLS), lambda p: (p, 0, 0)),
        out_shape=jax.ShapeDtypeStruct((pairs, WIN_ROWS, PAIR_COLS), jnp.float32),
        compiler_params=pltpu.CompilerParams(
            dimension_semantics=("arbitrary",), vmem_limit_bytes=VMEM_LIMIT),
        name="bias_table",
    )(w.reshape(heads, 1, TABLE_PERIOD))


def _bias_table_kernel(w_ref, o_ref):
    j = lax.broadcasted_iota(jnp.int32, (WIN_ROWS, GROUP_ROWS), 0)
    i = lax.broadcasted_iota(jnp.int32, (WIN_ROWS, GROUP_ROWS), 1)
    dc = j // CHUNK - i // CHUNK
    valid = (dc >= 0) & (dc <= LEFT_CHUNKS)
    for hh in range(HEADS_PER_STEP):
        rows = jnp.broadcast_to(w_ref[hh], (WIN_ROWS, TABLE_PERIOD))
        toep = pltpu.roll(rows, 0, axis=1, stride=1, stride_axis=0)[:, :GROUP_ROWS]
        o_ref[0, :, hh * GROUP_ROWS:(hh + 1) * GROUP_ROWS] = jnp.where(valid, toep, NEG_INF)


def _merge_kernel(zg_ref, za_ref, lg_ref, la_ref, x_ref, wog_ref, woa_ref,
                  wout_ref, mb_ref, gp_ref, o_ref):
    yg = jnp.dot(zg_ref[...], wog_ref[...], preferred_element_type=jnp.float32)
    ya = jnp.dot(za_ref[...], woa_ref[...], preferred_element_type=jnp.float32)
    gate_g = _sigmoid(lg_ref[...].astype(jnp.float32) + mb_ref[0:1, :])
    gate_a = _sigmoid(la_ref[...].astype(jnp.float32) + mb_ref[1:2, :])
    merged = (gate_g * yg + gate_a * ya).astype(jnp.bfloat16)
    y = jnp.dot(merged, wout_ref[...], preferred_element_type=jnp.float32)
    ms = jnp.mean(y * y, axis=-1, keepdims=True)
    o_ref[...] = x_ref[...] + y * lax.rsqrt(ms + RMS_EPS) * gp_ref[...]


def _merge(z_gla, z_att, proj, x2, w_o_gla, w_o_att, w_out, merge_bias2, g_post):
    n = x2.shape[0]
    row = lambda i: (i, 0)
    const = lambda i: (0, 0)
    return pl.pallas_call(
        _merge_kernel,
        grid=(n // MERGE_TM,),
        in_specs=[
            pl.BlockSpec((MERGE_TM, D_MODEL), row),
            pl.BlockSpec((MERGE_TM, D_MODEL), row),
            pl.BlockSpec((MERGE_TM, D_MODEL), lambda i: (i, GATE_OFF // D_MODEL)),
            pl.BlockSpec((MERGE_TM, D_MODEL), lambda i: (i, GATE_OFF // D_MODEL + 1)),
            pl.BlockSpec((MERGE_TM, D_MODEL), row),
            pl.BlockSpec((D_MODEL, D_MODEL), const),
            pl.BlockSpec((D_MODEL, D_MODEL), const),
            pl.BlockSpec((D_MODEL, D_MODEL), const),
            pl.BlockSpec((2, D_MODEL), const),
            pl.BlockSpec((1, D_MODEL), const),
        ],
        out_specs=pl.BlockSpec((MERGE_TM, D_MODEL), row),
        out_shape=jax.ShapeDtypeStruct((n, D_MODEL), jnp.float32),
        compiler_params=pltpu.CompilerParams(
            dimension_semantics=("arbitrary",),
            vmem_limit_bytes=VMEM_LIMIT,
        ),
        name="merge",
    )(z_gla, z_att, proj, proj, x2, w_o_gla, w_o_att, w_out, merge_bias2, g_post)


def kernel(x, norm_pre_g, w_in, gk_up, gk_bias, gla_norm_g, rel_bias, w_o_gla,
           w_o_att, merge_bias, w_out, norm_post_g):
    batch, seq, d = x.shape
    assert d == D_MODEL and seq % GLA_ROWS == 0 and seq % GROUP_ROWS == 0
    assert (batch * seq) % INPROJ_TM == 0 and seq % INPROJ_TM == 0
    assert rel_bias.shape == (ATT_HEADS, 2 * MAX_REL + 1)
    bf16 = jnp.bfloat16

    splits = np.cumsum([GLA_K_WIDTH, GLA_K_WIDTH, GLA_V_WIDTH, GLA_V_WIDTH, GK_RANK,
                        ATT_WIDTH, ATT_WIDTH, ATT_WIDTH, ATT_WIDTH])
    (w_qg, w_kg, w_vg, w_gg, w_code, w_qa, w_ka, w_va, w_ga, w_gate) = jnp.split(
        w_in, [int(s) for s in splits], axis=1)
    w_main = jnp.concatenate(
        [w_qa * (ATT_DH ** -0.5 * LOG2_E), w_ka, w_ga, w_gate, w_vg, w_gg, w_qg, w_kg],
        axis=1).astype(bf16)
    w_vt = w_va.astype(bf16).T
    w_gk = jnp.pad(w_code, ((0, 0), (0, GK_PAD - GK_RANK))).astype(bf16)
    up_pad = jnp.pad(gk_up, ((0, GK_PAD - GK_RANK), (0, 0))).astype(bf16)

    x2 = x.reshape(batch * seq, D_MODEL)
    proj, v_t, gk_code = _inproj(x2, norm_pre_g.reshape(1, D_MODEL), w_main, w_vt, w_gk)
    z_gla = _gla(proj, gk_code, up_pad, gk_bias.reshape(1, GLA_K_WIDTH), batch, seq)
    z_att = _attn(proj, v_t, _bias_table(rel_bias), batch, seq)
    w_og = (jnp.tile(gla_norm_g, GLA_HEADS)[:, None] * w_o_gla).astype(bf16)
    out = _merge(z_gla, z_att, proj, x2, w_og, w_o_att.astype(bf16),
                 w_out.astype(bf16), merge_bias.reshape(2, D_MODEL),
                 norm_post_g.reshape(1, D_MODEL))
    return out.reshape(batch, seq, D_MODEL)
```

```python
import functools

import numpy as np
import jax
import jax.numpy as jnp
from jax import lax
from jax.experimental import pallas as pl
from jax.experimental.pallas import tpu as pltpu

D_MODEL = 1024
CHUNK = 64
GLA_HEADS = 4
GLA_DK = 128
GLA_DV = 256
GLA_K_WIDTH = GLA_HEADS * GLA_DK
GLA_V_WIDTH = GLA_HEADS * GLA_DV
GK_RANK = 16
GATE_NORMALIZER = 16.0
ATT_HEADS = 16
ATT_DH = 64
ATT_WIDTH = ATT_HEADS * ATT_DH
LEFT_CHUNKS = 8
MAX_REL = 256
RMS_EPS = 1e-6
NEG_INF = -1e30

LANES = 128
BF16_SUBLANES = 16
LOG2_E = float(np.log2(np.e))
LN_2 = float(np.log(2.0))

QA_OFF = 0
KA_OFF = 1024
GA_OFF = 2048
GATE_OFF = 3072
PROJ_COLS = 5120
VG_OFF = 5120
GG_OFF = 6144
QG_OFF = 7168
KG_OFF = 7680
W_COLS = 8192
GK_PAD = LANES

GROUP = 4
GROUP_ROWS = GROUP * CHUNK
WIN_CHUNKS = GROUP + LEFT_CHUNKS
WIN_ROWS = WIN_CHUNKS * CHUNK
HEADS_PER_STEP = LANES // ATT_DH
PAIR_COLS = HEADS_PER_STEP * GROUP_ROWS
TABLE_PERIOD = GROUP_ROWS + WIN_ROWS

FRONT_ROWS = 512
MERGE_TM = 512
VMEM_LIMIT = 56 * 1024 * 1024

_NT = (((1,), (1,)), ((), ()))


def _sigmoid(x):
    return 1.0 / (1.0 + jnp.exp2(x * (-LOG2_E)))


def _chunk_cumsum(x):
    n_chunks = x.shape[0] // CHUNK
    hi = x.astype(jnp.bfloat16)
    lo = (x - hi.astype(jnp.float32)).astype(jnp.bfloat16)
    r = lax.broadcasted_iota(jnp.int32, (CHUNK, 2 * CHUNK), 0)
    c = lax.broadcasted_iota(jnp.int32, (CHUNK, 2 * CHUNK), 1)
    tri = (r >= jnp.bitwise_and(c, CHUNK - 1)).astype(jnp.bfloat16)
    out = []
    for n in range(n_chunks):
        rows = slice(n * CHUNK, (n + 1) * CHUNK)
        terms = jnp.concatenate([hi[rows], lo[rows]], axis=0)
        out.append(jnp.dot(tri, terms, preferred_element_type=jnp.float32))
    return jnp.stack(out, axis=0)


def _front_kernel(x_ref, g_ref, w_ref, wvt_ref, wgk_ref, up_ref, bias_ref,
                  proj_ref, vt_ref, z_ref, state_ref, *, steps_per_seq):
    n_chunks = FRONT_ROWS // CHUNK

    @pl.when(pl.program_id(0) % steps_per_seq == 0)
    def _():
        state_ref[...] = jnp.zeros_like(state_ref)

    x = x_ref[...]
    ms = jnp.mean(x * x, axis=-1, keepdims=True)
    h = (x * lax.rsqrt(ms + RMS_EPS) * g_ref[...]).astype(jnp.bfloat16)

    def project(off, width):
        return jnp.dot(h, w_ref[:, off:off + width],
                       preferred_element_type=jnp.float32).astype(jnp.bfloat16)

    def emit_proj(off):
        def run():
            proj_ref[:, off:off + D_MODEL] = project(off, D_MODEL)
        return run

    def emit_vt():
        vt_ref[...] = lax.dot_general(
            wvt_ref[...], h, _NT, preferred_element_type=jnp.float32
        ).astype(jnp.bfloat16)

    filler = [emit_proj(off) for off in range(0, PROJ_COLS, D_MODEL)] + [emit_vt]

    code = jnp.dot(h, wgk_ref[...], preferred_element_type=jnp.float32).astype(jnp.bfloat16)
    v = project(VG_OFF, GLA_V_WIDTH)
    qk = project(QG_OFF, 2 * GLA_K_WIDTH)
    q, k = qk[:, :GLA_K_WIDTH], qk[:, GLA_K_WIDTH:]
    logits = jnp.dot(code, up_ref[...],
                     preferred_element_type=jnp.float32) + bias_ref[...]
    gate_all = project(GG_OFF, GLA_V_WIDTH)
    gk = (jnp.minimum(logits, 0.0) * (1.0 / GATE_NORMALIZER)
          - jnp.log2(1.0 + jnp.exp2(jnp.abs(logits) * (-LOG2_E)))
          * (LN_2 / GATE_NORMALIZER))
    a_cum3 = _chunk_cumsum(gk)
    filler.pop(0)()
    a_end = a_cum3[:, CHUNK - 1, :]
    k_dec = (k.astype(jnp.float32).reshape(n_chunks, CHUNK, GLA_K_WIDTH)
             * jnp.exp(a_end[:, None, :] - a_cum3)).astype(jnp.bfloat16)
    decay_t = jnp.exp(a_end).T
    kcols = [slice(hd * GLA_DK, (hd + 1) * GLA_DK) for hd in range(GLA_HEADS)]
    vcols = [slice(hd * GLA_DV, (hd + 1) * GLA_DV) for hd in range(GLA_HEADS)]

    def chunk_rows(c):
        return slice(c * CHUNK, (c + 1) * CHUNK)

    def increments(c):
        return [lax.dot_general(
            k_dec[c][:, kcols[hd]], v[chunk_rows(c), vcols[hd]],
            (((0,), (0,)), ((), ())), preferred_element_type=jnp.float32)
            for hd in range(GLA_HEADS)]

    def finish(c, reads):
        for hd in range(GLA_HEADS):
            o = reads[hd]
            ms_o = jnp.mean(o * o, axis=-1, keepdims=True)
            factor = lax.rsqrt(ms_o + RMS_EPS * GLA_DK)
            gate = gate_all[chunk_rows(c), vcols[hd]].astype(jnp.float32)
            z_ref[chunk_rows(c), vcols[hd]] = (
                (o * factor) * (gate * _sigmoid(gate))).astype(jnp.bfloat16)

    upd_next = increments(0)
    reads_prev = None
    for c in range(n_chunks):
        upd = upd_next
        if c + 1 < n_chunks:
            upd_next = increments(c + 1)
        if c % 2 == 0 and filler:
            filler.pop(0)()
        reads = []
        for hd in range(GLA_HEADS):
            state = decay_t[kcols[hd], c:c + 1] * state_ref[hd] + upd[hd]
            state_ref[hd] = state
            reads.append(jnp.dot(q[chunk_rows(c), kcols[hd]],
                                 state.astype(jnp.bfloat16),
                                 preferred_element_type=jnp.float32))
        if reads_prev is not None:
            finish(c - 1, reads_prev)
        reads_prev = reads
    finish(n_chunks - 1, reads_prev)
    while filler:
        filler.pop(0)()


def _front(x2, g_pre, w_main, w_vt, w_gk, up_pad, gk_bias, seq):
    n = x2.shape[0]
    const = lambda i: (0, 0)
    resident = pl.Buffered(1)
    return pl.pallas_call(
        functools.partial(_front_kernel, steps_per_seq=seq // FRONT_ROWS),
        grid=(n // FRONT_ROWS,),
        in_specs=[
            pl.BlockSpec((FRONT_ROWS, D_MODEL), lambda i: (i, 0)),
            pl.BlockSpec((1, D_MODEL), const),
            pl.BlockSpec((D_MODEL, W_COLS), const, pipeline_mode=resident),
            pl.BlockSpec((ATT_WIDTH, D_MODEL), const, pipeline_mode=resident),
            pl.BlockSpec((D_MODEL, GK_PAD), const, pipeline_mode=resident),
            pl.BlockSpec((GK_PAD, GLA_K_WIDTH), const, pipeline_mode=resident),
            pl.BlockSpec((1, GLA_K_WIDTH), const),
        ],
        out_specs=[
            pl.BlockSpec((FRONT_ROWS, PROJ_COLS), lambda i: (i, 0)),
            pl.BlockSpec((ATT_WIDTH, FRONT_ROWS), lambda i: (0, i)),
            pl.BlockSpec((FRONT_ROWS, GLA_V_WIDTH), lambda i: (i, 0)),
        ],
        out_shape=[
            jax.ShapeDtypeStruct((n, PROJ_COLS), jnp.bfloat16),
            jax.ShapeDtypeStruct((ATT_WIDTH, n), jnp.bfloat16),
            jax.ShapeDtypeStruct((n, GLA_V_WIDTH), jnp.bfloat16),
        ],
        scratch_shapes=[pltpu.VMEM((GLA_HEADS, GLA_DK, GLA_DV), jnp.float32)],
        compiler_params=pltpu.CompilerParams(
            dimension_semantics=("arbitrary",),
            vmem_limit_bytes=VMEM_LIMIT,
        ),
        name="front",
    )(x2, g_pre, w_main, w_vt, w_gk, up_pad, gk_bias)


def _key_window(g):
    lead = LEFT_CHUNKS // GROUP
    k_start = max(g - lead, 0) * GROUP_ROWS
    return k_start, (g + 1) * GROUP_ROWS - k_start


def _attn_scores(q_ref, k_ref, tbl_ref, g):
    k_start, n_keys = _key_window(g)
    q2 = q_ref[g * GROUP_ROWS:(g + 1) * GROUP_ROWS, :]
    lane = lax.broadcasted_iota(jnp.int32, q2.shape, 1)
    first = lane < ATT_DH
    zero = jnp.zeros_like(q2)
    qs = jnp.concatenate(
        [jnp.where(first, q2, zero), jnp.where(first, zero, q2)], axis=0)
    kw = k_ref[k_start:k_start + n_keys, :]
    s = lax.dot_general(kw, qs, _NT, preferred_element_type=jnp.float32)
    s = s + tbl_ref[0, WIN_ROWS - n_keys:, :]
    return s, jnp.max(s, axis=0, keepdims=True)


def _attn_output(vt_ref, g_ref, z_ref, p, g):
    k_start, n_keys = _key_window(g)
    vt_aug = jnp.concatenate(
        [vt_ref[:, k_start:k_start + n_keys],
         jnp.ones((BF16_SUBLANES, n_keys), jnp.bfloat16)], axis=0)
    ot = jnp.dot(vt_aug, p, preferred_element_type=jnp.float32)
    r = 1.0 / ot[LANES:LANES + 1, :]
    ot = jnp.concatenate(
        [ot[:ATT_DH, :GROUP_ROWS] * r[:, :GROUP_ROWS],
         ot[ATT_DH:LANES, GROUP_ROWS:] * r[:, GROUP_ROWS:]], axis=0)
    o = ot.T
    rows = slice(g * GROUP_ROWS, (g + 1) * GROUP_ROWS)
    gate = g_ref[rows, :].astype(jnp.float32)
    z_ref[rows, :] = (o * gate * _sigmoid(gate)).astype(jnp.bfloat16)


def _attn_kernel(q_ref, k_ref, vt_ref, g_ref, tbl_ref, z_ref, *, seq):
    n_groups = seq // GROUP_ROWS
    s, m = _attn_scores(q_ref, k_ref, tbl_ref, 0)
    for g in range(n_groups):
        if g + 1 < n_groups:
            s_next, m_next = _attn_scores(q_ref, k_ref, tbl_ref, g + 1)
        p = jnp.exp2(s - m).astype(jnp.bfloat16)
        _attn_output(vt_ref, g_ref, z_ref, p, g)
        if g + 1 < n_groups:
            s, m = s_next, m_next


def _attn(proj, v_t, table, batch, seq):
    n = proj.shape[0]
    pairs = ATT_HEADS // HEADS_PER_STEP
    col_map = lambda off: (lambda p, b: (b, off // LANES + p))
    return pl.pallas_call(
        functools.partial(_attn_kernel, seq=seq),
        grid=(pairs, batch),
        in_specs=[
            pl.BlockSpec((seq, LANES), col_map(QA_OFF)),
            pl.BlockSpec((seq, LANES), col_map(KA_OFF)),
            pl.BlockSpec((LANES, seq), lambda p, b: (p, b)),
            pl.BlockSpec((seq, LANES), col_map(GA_OFF)),
            pl.BlockSpec((1, WIN_ROWS, PAIR_COLS), lambda p, b: (p, 0, 0)),
        ],
        out_specs=pl.BlockSpec((seq, LANES), lambda p, b: (b, p)),
        out_shape=jax.ShapeDtypeStruct((n, ATT_WIDTH), jnp.bfloat16),
        compiler_params=pltpu.CompilerParams(
            dimension_semantics=("arbitrary", "arbitrary"),
            vmem_limit_bytes=VMEM_LIMIT,
        ),
        name="attn",
    )(proj, proj, v_t, proj, table)


def _bias_table(rel_bias):
    heads = rel_bias.shape[0]
    pairs = heads // HEADS_PER_STEP
    far = rel_bias[:, 2 * MAX_REL:]
    w = jnp.concatenate(
        [jnp.broadcast_to(far, (heads, GROUP_ROWS)), rel_bias,
         jnp.broadcast_to(far, (heads, TABLE_PERIOD - GROUP_ROWS - (2 * MAX_REL + 1)))],
        axis=1).astype(jnp.float32) * LOG2_E
    return pl.pallas_call(
        _bias_table_kernel,
        grid=(pairs,),
        in_specs=[pl.BlockSpec((HEADS_PER_STEP, 1, TABLE_PERIOD), lambda p: (p, 0, 0))],
        out_specs=pl.BlockSpec((1, WIN_ROWS, PAIR_COLS), lambda p: (p, 0, 0)),
        out_shape=jax.ShapeDtypeStruct((pairs, WIN_ROWS, PAIR_COLS), jnp.float32),
        compiler_params=pltpu.CompilerParams(
            dimension_semantics=("arbitrary",), vmem_limit_bytes=VMEM_LIMIT),
        name="bias_table",
    )(w.reshape(heads, 1, TABLE_PERIOD))


def _bias_table_kernel(w_ref, o_ref):
    j = lax.broadcasted_iota(jnp.int32, (WIN_ROWS, GROUP_ROWS), 0)
    i = lax.broadcasted_iota(jnp.int32, (WIN_ROWS, GROUP_ROWS), 1)
    dc = j // CHUNK - i // CHUNK
    valid = (dc >= 0) & (dc <= LEFT_CHUNKS)
    for hh in range(HEADS_PER_STEP):
        rows = jnp.broadcast_to(w_ref[hh], (WIN_ROWS, TABLE_PERIOD))
        toep = pltpu.roll(rows, 0, axis=1, stride=1, stride_axis=0)[:, :GROUP_ROWS]
        o_ref[0, :, hh * GROUP_ROWS:(hh + 1) * GROUP_ROWS] = jnp.where(valid, toep, NEG_INF)


def _merge_kernel(zg_ref, za_ref, lg_ref, la_ref, x_ref, wog_ref, woa_ref,
                  wout_ref, mb_ref, gp_ref, o_ref):
    yg = jnp.dot(zg_ref[...], wog_ref[...], preferred_element_type=jnp.float32)
    ya = jnp.dot(za_ref[...], woa_ref[...], preferred_element_type=jnp.float32)
    gate_g = _sigmoid(lg_ref[...].astype(jnp.float32) + mb_ref[0:1, :])
    gate_a = _sigmoid(la_ref[...].astype(jnp.float32) + mb_ref[1:2, :])
    merged = (gate_g * yg + gate_a * ya).astype(jnp.bfloat16)
    y = jnp.dot(merged, wout_ref[...], preferred_element_type=jnp.float32)
    ms = jnp.mean(y * y, axis=-1, keepdims=True)
    o_ref[...] = x_ref[...] + y * lax.rsqrt(ms + RMS_EPS) * gp_ref[...]


def _merge(z_gla, z_att, proj, x2, w_o_gla, w_o_att, w_out, merge_bias2, g_post):
    n = x2.shape[0]
    row = lambda i: (i, 0)
    const = lambda i: (0, 0)
    return pl.pallas_call(
        _merge_kernel,
        grid=(n // MERGE_TM,),
        in_specs=[
            pl.BlockSpec((MERGE_TM, D_MODEL), row),
            pl.BlockSpec((MERGE_TM, D_MODEL), row),
            pl.BlockSpec((MERGE_TM, D_MODEL), lambda i: (i, GATE_OFF // D_MODEL)),
            pl.BlockSpec((MERGE_TM, D_MODEL), lambda i: (i, GATE_OFF // D_MODEL + 1)),
            pl.BlockSpec((MERGE_TM, D_MODEL), row),
            pl.BlockSpec((D_MODEL, D_MODEL), const),
            pl.BlockSpec((D_MODEL, D_MODEL), const),
            pl.BlockSpec((D_MODEL, D_MODEL), const),
            pl.BlockSpec((2, D_MODEL), const),
            pl.BlockSpec((1, D_MODEL), const),
        ],
        out_specs=pl.BlockSpec((MERGE_TM, D_MODEL), row),
        out_shape=jax.ShapeDtypeStruct((n, D_MODEL), jnp.float32),
        compiler_params=pltpu.CompilerParams(
            dimension_semantics=("arbitrary",),
            vmem_limit_bytes=VMEM_LIMIT,
        ),
        name="merge",
    )(z_gla, z_att, proj, proj, x2, w_o_gla, w_o_att, w_out, merge_bias2, g_post)


def kernel(x, norm_pre_g, w_in, gk_up, gk_bias, gla_norm_g, rel_bias, w_o_gla,
           w_o_att, merge_bias, w_out, norm_post_g):
    batch, seq, d = x.shape
    assert d == D_MODEL and seq % FRONT_ROWS == 0 and seq % GROUP_ROWS == 0
    assert (batch * seq) % MERGE_TM == 0
    assert rel_bias.shape == (ATT_HEADS, 2 * MAX_REL + 1)
    bf16 = jnp.bfloat16

    splits = np.cumsum([GLA_K_WIDTH, GLA_K_WIDTH, GLA_V_WIDTH, GLA_V_WIDTH, GK_RANK,
                        ATT_WIDTH, ATT_WIDTH, ATT_WIDTH, ATT_WIDTH])
    (w_qg, w_kg, w_vg, w_gg, w_code, w_qa, w_ka, w_va, w_ga, w_gate) = jnp.split(
        w_in, [int(s) for s in splits], axis=1)
    w_main = jnp.concatenate(
        [w_qa * (ATT_DH ** -0.5 * LOG2_E), w_ka, w_ga, w_gate, w_vg, w_gg, w_qg, w_kg],
        axis=1).astype(bf16)
    w_vt = w_va.astype(bf16).T
    w_gk = jnp.pad(w_code, ((0, 0), (0, GK_PAD - GK_RANK))).astype(bf16)
    up_pad = jnp.pad(gk_up, ((0, GK_PAD - GK_RANK), (0, 0))).astype(bf16)

    x2 = x.reshape(batch * seq, D_MODEL)
    proj, v_t, z_gla = _front(x2, norm_pre_g.reshape(1, D_MODEL), w_main, w_vt, w_gk,
                              up_pad, gk_bias.reshape(1, GLA_K_WIDTH), seq)
    z_att = _attn(proj, v_t, _bias_table(rel_bias), batch, seq)
    w_og = (jnp.tile(gla_norm_g, GLA_HEADS)[:, None] * w_o_gla).astype(bf16)
    out = _merge(z_gla, z_att, proj, x2, w_og, w_o_att.astype(bf16),
                 w_out.astype(bf16), merge_bias.reshape(2, D_MODEL),
                 norm_post_g.reshape(1, D_MODEL))
    return out.reshape(batch, seq, D_MODEL)
```

```python
import functools

import numpy as np
import jax
import jax.numpy as jnp
from jax import lax
from jax.experimental import pallas as pl
from jax.experimental.pallas import tpu as pltpu

D_MODEL = 1024
CHUNK = 64
GLA_HEADS = 4
GLA_DK = 128
GLA_DV = 256
GLA_K_WIDTH = GLA_HEADS * GLA_DK
GLA_V_WIDTH = GLA_HEADS * GLA_DV
GK_RANK = 16
GATE_NORMALIZER = 16.0
ATT_HEADS = 16
ATT_DH = 64
ATT_WIDTH = ATT_HEADS * ATT_DH
LEFT_CHUNKS = 8
MAX_REL = 256
RMS_EPS = 1e-6
NEG_INF = -1e30

LANES = 128
BF16_SUBLANES = 16
LOG2_E = float(np.log2(np.e))
LN_2 = float(np.log(2.0))

QA_OFF = 0
KA_OFF = 1024
GA_OFF = 2048
GATE_OFF = 3072
PROJ_COLS = 5120
VG_OFF = 5120
GG_OFF = 6144
QG_OFF = 7168
KG_OFF = 7680
W_COLS = 8192
GK_PAD = LANES

GROUP = 2
STAGE_GROUPS = 2
GROUP_ROWS = GROUP * CHUNK
WIN_CHUNKS = GROUP + LEFT_CHUNKS
WIN_ROWS = WIN_CHUNKS * CHUNK
HEADS_PER_STEP = LANES // ATT_DH
PAIR_COLS = HEADS_PER_STEP * GROUP_ROWS
TABLE_PERIOD = GROUP_ROWS + WIN_ROWS

FRONT_ROWS = 512
MERGE_TM = 512
VMEM_LIMIT = 56 * 1024 * 1024

_NT = (((1,), (1,)), ((), ()))


def _sigmoid(x):
    return 1.0 / (1.0 + jnp.exp2(x * (-LOG2_E)))


def _chunk_cumsum(x):
    n_chunks = x.shape[0] // CHUNK
    hi = x.astype(jnp.bfloat16)
    lo = (x - hi.astype(jnp.float32)).astype(jnp.bfloat16)
    r = lax.broadcasted_iota(jnp.int32, (CHUNK, 2 * CHUNK), 0)
    c = lax.broadcasted_iota(jnp.int32, (CHUNK, 2 * CHUNK), 1)
    tri = (r >= jnp.bitwise_and(c, CHUNK - 1)).astype(jnp.bfloat16)
    out = []
    for n in range(n_chunks):
        rows = slice(n * CHUNK, (n + 1) * CHUNK)
        terms = jnp.concatenate([hi[rows], lo[rows]], axis=0)
        out.append(jnp.dot(tri, terms, preferred_element_type=jnp.float32))
    return jnp.stack(out, axis=0)


def _front_kernel(x_ref, g_ref, w_ref, wvt_ref, wgk_ref, up_ref, bias_ref,
                  proj_ref, vt_ref, z_ref, state_ref, *, steps_per_seq):
    n_chunks = FRONT_ROWS // CHUNK

    @pl.when(pl.program_id(0) % steps_per_seq == 0)
    def _():
        state_ref[...] = jnp.zeros_like(state_ref)

    x = x_ref[...]
    ms = jnp.mean(x * x, axis=-1, keepdims=True)
    h = (x * lax.rsqrt(ms + RMS_EPS) * g_ref[...]).astype(jnp.bfloat16)

    def project(off, width):
        return jnp.dot(h, w_ref[:, off:off + width],
                       preferred_element_type=jnp.float32).astype(jnp.bfloat16)

    def emit_proj(off):
        def run():
            proj_ref[:, off:off + D_MODEL] = project(off, D_MODEL)
        return run

    def emit_vt():
        vt_ref[...] = lax.dot_general(
            wvt_ref[...], h, _NT, preferred_element_type=jnp.float32
        ).astype(jnp.bfloat16)

    filler = [emit_proj(off) for off in range(0, PROJ_COLS, D_MODEL)] + [emit_vt]

    code = jnp.dot(h, wgk_ref[...], preferred_element_type=jnp.float32).astype(jnp.bfloat16)
    v = project(VG_OFF, GLA_V_WIDTH)
    qk = project(QG_OFF, 2 * GLA_K_WIDTH)
    q, k = qk[:, :GLA_K_WIDTH], qk[:, GLA_K_WIDTH:]
    logits = jnp.dot(code, up_ref[...],
                     preferred_element_type=jnp.float32) + bias_ref[...]
    gate_all = project(GG_OFF, GLA_V_WIDTH)
    gk = (jnp.minimum(logits, 0.0) * (1.0 / GATE_NORMALIZER)
          - jnp.log2(1.0 + jnp.exp2(jnp.abs(logits) * (-LOG2_E)))
          * (LN_2 / GATE_NORMALIZER))
    a_cum3 = _chunk_cumsum(gk)
    filler.pop(0)()
    a_end = a_cum3[:, CHUNK - 1, :]
    k_dec = (k.astype(jnp.float32).reshape(n_chunks, CHUNK, GLA_K_WIDTH)
             * jnp.exp(a_end[:, None, :] - a_cum3)).astype(jnp.bfloat16)
    decay_t = jnp.exp(a_end).T
    kcols = [slice(hd * GLA_DK, (hd + 1) * GLA_DK) for hd in range(GLA_HEADS)]
    vcols = [slice(hd * GLA_DV, (hd + 1) * GLA_DV) for hd in range(GLA_HEADS)]

    def chunk_rows(c):
        return slice(c * CHUNK, (c + 1) * CHUNK)

    def increments(c):
        return [lax.dot_general(
            k_dec[c][:, kcols[hd]], v[chunk_rows(c), vcols[hd]],
            (((0,), (0,)), ((), ())), preferred_element_type=jnp.float32)
            for hd in range(GLA_HEADS)]

    def finish(c, reads):
        for hd in range(GLA_HEADS):
            o = reads[hd]
            ms_o = jnp.mean(o * o, axis=-1, keepdims=True)
            factor = lax.rsqrt(ms_o + RMS_EPS * GLA_DK)
            gate = gate_all[chunk_rows(c), vcols[hd]].astype(jnp.float32)
            z_ref[chunk_rows(c), vcols[hd]] = (
                (o * factor) * (gate * _sigmoid(gate))).astype(jnp.bfloat16)

    upd_next = increments(0)
    reads_prev = None
    for c in range(n_chunks):
        upd = upd_next
        if c + 1 < n_chunks:
            upd_next = increments(c + 1)
        if c % 2 == 0 and filler:
            filler.pop(0)()
        reads = []
        for hd in range(GLA_HEADS):
            state = decay_t[kcols[hd], c:c + 1] * state_ref[hd] + upd[hd]
            state_ref[hd] = state
            reads.append(jnp.dot(q[chunk_rows(c), kcols[hd]],
                                 state.astype(jnp.bfloat16),
                                 preferred_element_type=jnp.float32))
        if reads_prev is not None:
            finish(c - 1, reads_prev)
        reads_prev = reads
    finish(n_chunks - 1, reads_prev)
    while filler:
        filler.pop(0)()


def _front(x2, g_pre, w_main, w_vt, w_gk, up_pad, gk_bias, seq):
    n = x2.shape[0]
    const = lambda i: (0, 0)
    resident = pl.Buffered(1)
    return pl.pallas_call(
        functools.partial(_front_kernel, steps_per_seq=seq // FRONT_ROWS),
        grid=(n // FRONT_ROWS,),
        in_specs=[
            pl.BlockSpec((FRONT_ROWS, D_MODEL), lambda i: (i, 0)),
            pl.BlockSpec((1, D_MODEL), const),
            pl.BlockSpec((D_MODEL, W_COLS), const, pipeline_mode=resident),
            pl.BlockSpec((ATT_WIDTH, D_MODEL), const, pipeline_mode=resident),
            pl.BlockSpec((D_MODEL, GK_PAD), const, pipeline_mode=resident),
            pl.BlockSpec((GK_PAD, GLA_K_WIDTH), const, pipeline_mode=resident),
            pl.BlockSpec((1, GLA_K_WIDTH), const),
        ],
        out_specs=[
            pl.BlockSpec((FRONT_ROWS, PROJ_COLS), lambda i: (i, 0)),
            pl.BlockSpec((ATT_WIDTH, FRONT_ROWS), lambda i: (0, i)),
            pl.BlockSpec((FRONT_ROWS, GLA_V_WIDTH), lambda i: (i, 0)),
        ],
        out_shape=[
            jax.ShapeDtypeStruct((n, PROJ_COLS), jnp.bfloat16),
            jax.ShapeDtypeStruct((ATT_WIDTH, n), jnp.bfloat16),
            jax.ShapeDtypeStruct((n, GLA_V_WIDTH), jnp.bfloat16),
        ],
        scratch_shapes=[pltpu.VMEM((GLA_HEADS, GLA_DK, GLA_DV), jnp.float32)],
        compiler_params=pltpu.CompilerParams(
            dimension_semantics=("arbitrary",),
            vmem_limit_bytes=VMEM_LIMIT,
        ),
        name="front",
    )(x2, g_pre, w_main, w_vt, w_gk, up_pad, gk_bias)


def _key_window(g):
    lead = LEFT_CHUNKS // GROUP
    k_start = max(g - lead, 0) * GROUP_ROWS
    return k_start, (g + 1) * GROUP_ROWS - k_start


def _attn_scores(q_ref, k_ref, tbl_ref, g):
    k_start, n_keys = _key_window(g)
    q2 = q_ref[g * GROUP_ROWS:(g + 1) * GROUP_ROWS, :]
    lane = lax.broadcasted_iota(jnp.int32, q2.shape, 1)
    first = lane < ATT_DH
    zero = jnp.zeros_like(q2)
    qs = jnp.concatenate(
        [jnp.where(first, q2, zero), jnp.where(first, zero, q2)], axis=0)
    kw = k_ref[k_start:k_start + n_keys, :]
    s = lax.dot_general(kw, qs, _NT, preferred_element_type=jnp.float32)
    s = s + tbl_ref[0, WIN_ROWS - n_keys:, :]
    return s, jnp.max(s, axis=0, keepdims=True)


def _attn_values(vt_ref, p, g):
    k_start, n_keys = _key_window(g)
    vt_aug = jnp.concatenate(
        [vt_ref[:, k_start:k_start + n_keys],
         jnp.ones((BF16_SUBLANES, n_keys), jnp.bfloat16)], axis=0)
    ot = jnp.dot(vt_aug, p, preferred_element_type=jnp.float32)
    r = 1.0 / ot[LANES:LANES + 1, :]
    return jnp.concatenate(
        [ot[:ATT_DH, :GROUP_ROWS] * r[:, :GROUP_ROWS],
         ot[ATT_DH:LANES, GROUP_ROWS:] * r[:, GROUP_ROWS:]], axis=0)


def _attn_kernel(q_ref, k_ref, vt_ref, g_ref, tbl_ref, z_ref, *, seq):
    n_stages = seq // (STAGE_GROUPS * GROUP_ROWS)

    def scores(t):
        return [_attn_scores(q_ref, k_ref, tbl_ref, STAGE_GROUPS * t + j)
                for j in range(STAGE_GROUPS)]

    cur = scores(0)
    for t in range(n_stages):
        if t + 1 < n_stages:
            nxt = scores(t + 1)
        probs = [jnp.exp2(s - m).astype(jnp.bfloat16) for s, m in cur]
        ot = jnp.concatenate(
            [_attn_values(vt_ref, p, STAGE_GROUPS * t + j) for j, p in enumerate(probs)],
            axis=1)
        rows = slice(t * STAGE_GROUPS * GROUP_ROWS, (t + 1) * STAGE_GROUPS * GROUP_ROWS)
        gate = g_ref[rows, :].astype(jnp.float32)
        z_ref[rows, :] = (ot.T * gate * _sigmoid(gate)).astype(jnp.bfloat16)
        if t + 1 < n_stages:
            cur = nxt


def _attn(proj, v_t, table, batch, seq):
    n = proj.shape[0]
    pairs = ATT_HEADS // HEADS_PER_STEP
    col_map = lambda off: (lambda p, b: (b, off // LANES + p))
    return pl.pallas_call(
        functools.partial(_attn_kernel, seq=seq),
        grid=(pairs, batch),
        in_specs=[
            pl.BlockSpec((seq, LANES), col_map(QA_OFF)),
            pl.BlockSpec((seq, LANES), col_map(KA_OFF)),
            pl.BlockSpec((LANES, seq), lambda p, b: (p, b)),
            pl.BlockSpec((seq, LANES), col_map(GA_OFF)),
            pl.BlockSpec((1, WIN_ROWS, PAIR_COLS), lambda p, b: (p, 0, 0)),
        ],
        out_specs=pl.BlockSpec((seq, LANES), lambda p, b: (b, p)),
        out_shape=jax.ShapeDtypeStruct((n, ATT_WIDTH), jnp.bfloat16),
        compiler_params=pltpu.CompilerParams(
            dimension_semantics=("arbitrary", "arbitrary"),
            vmem_limit_bytes=VMEM_LIMIT,
        ),
        name="attn",
    )(proj, proj, v_t, proj, table)


def _bias_table(rel_bias):
    heads = rel_bias.shape[0]
    pairs = heads // HEADS_PER_STEP
    t = np.arange(TABLE_PERIOD)
    dist = np.where(t < GROUP_ROWS, t, t - TABLE_PERIOD) + (WIN_ROWS - GROUP_ROWS)
    rel_idx = np.clip(dist, -MAX_REL, MAX_REL) + MAX_REL
    w = rel_bias[:, rel_idx].astype(jnp.float32) * LOG2_E
    return pl.pallas_call(
        _bias_table_kernel,
        grid=(pairs,),
        in_specs=[pl.BlockSpec((HEADS_PER_STEP, 1, TABLE_PERIOD), lambda p: (p, 0, 0))],
        out_specs=pl.BlockSpec((1, WIN_ROWS, PAIR_COLS), lambda p: (p, 0, 0)),
        out_shape=jax.ShapeDtypeStruct((pairs, WIN_ROWS, PAIR_COLS), jnp.float32),
        compiler_params=pltpu.CompilerParams(
            dimension_semantics=("arbitrary",), vmem_limit_bytes=VMEM_LIMIT),
        name="bias_table",
    )(w.reshape(heads, 1, TABLE_PERIOD))


def _bias_table_kernel(w_ref, o_ref):
    j = lax.broadcasted_iota(jnp.int32, (WIN_ROWS, GROUP_ROWS), 0)
    i = lax.broadcasted_iota(jnp.int32, (WIN_ROWS, GROUP_ROWS), 1)
    dc = j // CHUNK - i // CHUNK
    valid = (dc >= 0) & (dc <= LEFT_CHUNKS)
    for hh in range(HEADS_PER_STEP):
        rows = jnp.broadcast_to(w_ref[hh], (WIN_ROWS, TABLE_PERIOD))
        toep = pltpu.roll(rows, 0, axis=1, stride=1, stride_axis=0)[:, :GROUP_ROWS]
        o_ref[0, :, hh * GROUP_ROWS:(hh + 1) * GROUP_ROWS] = jnp.where(valid, toep, NEG_INF)


def _merge_kernel(zg_ref, za_ref, lg_ref, la_ref, x_ref, wog_ref, woa_ref,
                  wout_ref, mb_ref, gp_ref, o_ref):
    yg = jnp.dot(zg_ref[...], wog_ref[...], preferred_element_type=jnp.float32)
    ya = jnp.dot(za_ref[...], woa_ref[...], preferred_element_type=jnp.float32)
    gate_g = _sigmoid(lg_ref[...].astype(jnp.float32) + mb_ref[0:1, :])
    gate_a = _sigmoid(la_ref[...].astype(jnp.float32) + mb_ref[1:2, :])
    merged = (gate_g * yg + gate_a * ya).astype(jnp.bfloat16)
    y = jnp.dot(merged, wout_ref[...], preferred_element_type=jnp.float32)
    ms = jnp.mean(y * y, axis=-1, keepdims=True)
    o_ref[...] = x_ref[...] + y * lax.rsqrt(ms + RMS_EPS) * gp_ref[...]


def _merge(z_gla, z_att, proj, x2, w_o_gla, w_o_att, w_out, merge_bias2, g_post):
    n = x2.shape[0]
    row = lambda i: (i, 0)
    const = lambda i: (0, 0)
    return pl.pallas_call(
        _merge_kernel,
        grid=(n // MERGE_TM,),
        in_specs=[
            pl.BlockSpec((MERGE_TM, D_MODEL), row),
            pl.BlockSpec((MERGE_TM, D_MODEL), row),
            pl.BlockSpec((MERGE_TM, D_MODEL), lambda i: (i, GATE_OFF // D_MODEL)),
            pl.BlockSpec((MERGE_TM, D_MODEL), lambda i: (i, GATE_OFF // D_MODEL + 1)),
            pl.BlockSpec((MERGE_TM, D_MODEL), row),
            pl.BlockSpec((D_MODEL, D_MODEL), const),
            pl.BlockSpec((D_MODEL, D_MODEL), const),
            pl.BlockSpec((D_MODEL, D_MODEL), const),
            pl.BlockSpec((2, D_MODEL), const),
            pl.BlockSpec((1, D_MODEL), const),
        ],
        out_specs=pl.BlockSpec((MERGE_TM, D_MODEL), row),
        out_shape=jax.ShapeDtypeStruct((n, D_MODEL), jnp.float32),
        compiler_params=pltpu.CompilerParams(
            dimension_semantics=("arbitrary",),
            vmem_limit_bytes=VMEM_LIMIT,
        ),
        name="merge",
    )(z_gla, z_att, proj, proj, x2, w_o_gla, w_o_att, w_out, merge_bias2, g_post)


def kernel(x, norm_pre_g, w_in, gk_up, gk_bias, gla_norm_g, rel_bias, w_o_gla,
           w_o_att, merge_bias, w_out, norm_post_g):
    batch, seq, d = x.shape
    assert d == D_MODEL and seq % FRONT_ROWS == 0 and seq % GROUP_ROWS == 0
    assert (batch * seq) % MERGE_TM == 0
    assert rel_bias.shape == (ATT_HEADS, 2 * MAX_REL + 1)
    bf16 = jnp.bfloat16

    splits = np.cumsum([GLA_K_WIDTH, GLA_K_WIDTH, GLA_V_WIDTH, GLA_V_WIDTH, GK_RANK,
                        ATT_WIDTH, ATT_WIDTH, ATT_WIDTH, ATT_WIDTH])
    (w_qg, w_kg, w_vg, w_gg, w_code, w_qa, w_ka, w_va, w_ga, w_gate) = jnp.split(
        w_in, [int(s) for s in splits], axis=1)
    w_main = jnp.concatenate(
        [w_qa * (ATT_DH ** -0.5 * LOG2_E), w_ka, w_ga, w_gate, w_vg, w_gg, w_qg, w_kg],
        axis=1).astype(bf16)
    w_vt = w_va.astype(bf16).T
    w_gk = jnp.pad(w_code, ((0, 0), (0, GK_PAD - GK_RANK))).astype(bf16)
    up_pad = jnp.pad(gk_up, ((0, GK_PAD - GK_RANK), (0, 0))).astype(bf16)

    x2 = x.reshape(batch * seq, D_MODEL)
    proj, v_t, z_gla = _front(x2, norm_pre_g.reshape(1, D_MODEL), w_main, w_vt, w_gk,
                              up_pad, gk_bias.reshape(1, GLA_K_WIDTH), seq)
    z_att = _attn(proj, v_t, _bias_table(rel_bias), batch, seq)
    w_og = (jnp.tile(gla_norm_g, GLA_HEADS)[:, None] * w_o_gla).astype(bf16)
    out = _merge(z_gla, z_att, proj, x2, w_og, w_o_att.astype(bf16),
                 w_out.astype(bf16), merge_bias.reshape(2, D_MODEL),
                 norm_post_g.reshape(1, D_MODEL))
    return out.reshape(batch, seq, D_MODEL)
```

```python
import functools

import numpy as np
import jax
import jax.numpy as jnp
from jax import lax
from jax.experimental import pallas as pl
from jax.experimental.pallas import tpu as pltpu

D_MODEL = 1024
CHUNK = 64
GLA_HEADS = 4
GLA_DK = 128
GLA_DV = 256
GLA_K_WIDTH = GLA_HEADS * GLA_DK
GLA_V_WIDTH = GLA_HEADS * GLA_DV
GK_RANK = 16
GATE_NORMALIZER = 16.0
ATT_HEADS = 16
ATT_DH = 64
ATT_WIDTH = ATT_HEADS * ATT_DH
LEFT_CHUNKS = 8
MAX_REL = 256
RMS_EPS = 1e-6
NEG_INF = -1e30

LANES = 128
BF16_SUBLANES = 16
LOG2_E = float(np.log2(np.e))
LN_2 = float(np.log(2.0))

QA_OFF = 0
KA_OFF = 1024
GA_OFF = 2048
GATE_OFF = 3072
PROJ_COLS = 5120
VG_OFF = 5120
GG_OFF = 6144
QG_OFF = 7168
KG_OFF = 7680
W_COLS = 8192

GROUP = 2
STAGE_GROUPS = 2
GROUP_ROWS = GROUP * CHUNK
WIN_CHUNKS = GROUP + LEFT_CHUNKS
WIN_ROWS = WIN_CHUNKS * CHUNK
HEADS_PER_STEP = LANES // ATT_DH
PAIR_COLS = HEADS_PER_STEP * GROUP_ROWS
TABLE_PERIOD = GROUP_ROWS + WIN_ROWS

FRONT_ROWS = 512
MERGE_TM = 1024
MERGE_SUB = 256
VMEM_LIMIT = 56 * 1024 * 1024

_NT = (((1,), (1,)), ((), ()))


def _sigmoid(x):
    return 1.0 / (1.0 + jnp.exp2(x * (-LOG2_E)))


def _chunk_cumsum(x):
    n_chunks = x.shape[0] // CHUNK
    hi = x.astype(jnp.bfloat16)
    lo = (x - hi.astype(jnp.float32)).astype(jnp.bfloat16)
    r = lax.broadcasted_iota(jnp.int32, (CHUNK, 2 * CHUNK), 0)
    c = lax.broadcasted_iota(jnp.int32, (CHUNK, 2 * CHUNK), 1)
    tri = (r >= jnp.bitwise_and(c, CHUNK - 1)).astype(jnp.bfloat16)
    out = []
    for n in range(n_chunks):
        rows = slice(n * CHUNK, (n + 1) * CHUNK)
        terms = jnp.concatenate([hi[rows], lo[rows]], axis=0)
        out.append(jnp.dot(tri, terms, preferred_element_type=jnp.float32))
    return jnp.stack(out, axis=0)


def _front_kernel(x_ref, w_ref, wvt_ref, up_ref, bias_ref,
                  proj_ref, vt_ref, z_ref, state_ref, *, steps_per_seq):
    n_chunks = FRONT_ROWS // CHUNK

    @pl.when(pl.program_id(0) % steps_per_seq == 0)
    def _():
        state_ref[...] = jnp.zeros_like(state_ref)

    x = x_ref[...]
    h = x.astype(jnp.bfloat16)
    r = lax.rsqrt(jnp.mean(x * x, axis=-1, keepdims=True) + RMS_EPS)
    r_row = jnp.broadcast_to(r, (FRONT_ROWS, LANES)).T[:1, :]

    def project(off, width):
        return (jnp.dot(h, w_ref[:, off:off + width],
                        preferred_element_type=jnp.float32) * r).astype(jnp.bfloat16)

    def emit_proj(off):
        def run():
            proj_ref[:, off:off + D_MODEL] = project(off, D_MODEL)
        return run

    filler = [emit_proj(off) for off in range(0, PROJ_COLS, D_MODEL)]

    vt_code = lax.dot_general(wvt_ref[...], h, _NT,
                              preferred_element_type=jnp.float32) * r_row
    vt_ref[...] = vt_code[:ATT_WIDTH].astype(jnp.bfloat16)
    code_t = vt_code[ATT_WIDTH:].astype(jnp.bfloat16)
    v = project(VG_OFF, GLA_V_WIDTH)
    qk = project(QG_OFF, 2 * GLA_K_WIDTH)
    q, k = qk[:, :GLA_K_WIDTH], qk[:, GLA_K_WIDTH:]
    logits = lax.dot_general(code_t, up_ref[...], (((0,), (0,)), ((), ())),
                             preferred_element_type=jnp.float32) + bias_ref[...]
    gate_all = project(GG_OFF, GLA_V_WIDTH)
    gk = (jnp.minimum(logits, 0.0) * (1.0 / GATE_NORMALIZER)
          - jnp.log2(1.0 + jnp.exp2(jnp.abs(logits) * (-LOG2_E)))
          * (LN_2 / GATE_NORMALIZER))
    a_cum3 = _chunk_cumsum(gk)
    filler.pop(0)()
    a_end = a_cum3[:, CHUNK - 1, :]
    k_dec = (k.astype(jnp.float32).reshape(n_chunks, CHUNK, GLA_K_WIDTH)
             * jnp.exp(a_end[:, None, :] - a_cum3)).astype(jnp.bfloat16)
    decay_t = jnp.exp(a_end).T
    kcols = [slice(hd * GLA_DK, (hd + 1) * GLA_DK) for hd in range(GLA_HEADS)]
    vcols = [slice(hd * GLA_DV, (hd + 1) * GLA_DV) for hd in range(GLA_HEADS)]

    def chunk_rows(c):
        return slice(c * CHUNK, (c + 1) * CHUNK)

    def increments(c):
        return [lax.dot_general(
            k_dec[c][:, kcols[hd]], v[chunk_rows(c), vcols[hd]],
            (((0,), (0,)), ((), ())), preferred_element_type=jnp.float32)
            for hd in range(GLA_HEADS)]

    def finish(c, reads):
        for hd in range(GLA_HEADS):
            o = reads[hd]
            ms_o = jnp.mean(o * o, axis=-1, keepdims=True)
            factor = lax.rsqrt(ms_o + RMS_EPS * GLA_DK)
            gate = gate_all[chunk_rows(c), vcols[hd]].astype(jnp.float32)
            z_ref[chunk_rows(c), vcols[hd]] = (
                (o * factor) * (gate * _sigmoid(gate))).astype(jnp.bfloat16)

    upd_next = increments(0)
    reads_prev = None
    for c in range(n_chunks):
        upd = upd_next
        if c + 1 < n_chunks:
            upd_next = increments(c + 1)
        if c % 2 == 0 and c > 0:
            filler.pop(0)()
        reads = []
        for hd in range(GLA_HEADS):
            state = decay_t[kcols[hd], c:c + 1] * state_ref[hd] + upd[hd]
            state_ref[hd] = state
            reads.append(jnp.dot(q[chunk_rows(c), kcols[hd]],
                                 state.astype(jnp.bfloat16),
                                 preferred_element_type=jnp.float32))
        if reads_prev is not None:
            finish(c - 1, reads_prev)
        reads_prev = reads
    while filler:
        filler.pop(0)()
    finish(n_chunks - 1, reads_prev)


def _front(x2, w_main, w_vt_code, gk_up, gk_bias, seq):
    n = x2.shape[0]
    const = lambda i: (0, 0)
    resident = pl.Buffered(1)
    return pl.pallas_call(
        functools.partial(_front_kernel, steps_per_seq=seq // FRONT_ROWS),
        grid=(n // FRONT_ROWS,),
        in_specs=[
            pl.BlockSpec((FRONT_ROWS, D_MODEL), lambda i: (i, 0)),
            pl.BlockSpec((D_MODEL, W_COLS), const, pipeline_mode=resident),
            pl.BlockSpec((ATT_WIDTH + GK_RANK, D_MODEL), const, pipeline_mode=resident),
            pl.BlockSpec((GK_RANK, GLA_K_WIDTH), const, pipeline_mode=resident),
            pl.BlockSpec((1, GLA_K_WIDTH), const),
        ],
        out_specs=[
            pl.BlockSpec((FRONT_ROWS, PROJ_COLS), lambda i: (i, 0)),
            pl.BlockSpec((ATT_WIDTH, FRONT_ROWS), lambda i: (0, i)),
            pl.BlockSpec((FRONT_ROWS, GLA_V_WIDTH), lambda i: (i, 0)),
        ],
        out_shape=[
            jax.ShapeDtypeStruct((n, PROJ_COLS), jnp.bfloat16),
            jax.ShapeDtypeStruct((ATT_WIDTH, n), jnp.bfloat16),
            jax.ShapeDtypeStruct((n, GLA_V_WIDTH), jnp.bfloat16),
        ],
        scratch_shapes=[pltpu.VMEM((GLA_HEADS, GLA_DK, GLA_DV), jnp.float32)],
        compiler_params=pltpu.CompilerParams(
            dimension_semantics=("arbitrary",),
            vmem_limit_bytes=VMEM_LIMIT,
        ),
        name="front",
    )(x2, w_main, w_vt_code, gk_up, gk_bias)


def _key_window(g):
    lead = LEFT_CHUNKS // GROUP
    k_start = max(g - lead, 0) * GROUP_ROWS
    return k_start, (g + 1) * GROUP_ROWS - k_start


def _attn_scores(q_ref, k_ref, tbl_ref, g):
    k_start, n_keys = _key_window(g)
    q2 = q_ref[g * GROUP_ROWS:(g + 1) * GROUP_ROWS, :]
    lane = lax.broadcasted_iota(jnp.int32, q2.shape, 1)
    first = lane < ATT_DH
    zero = jnp.zeros_like(q2)
    qs = jnp.concatenate(
        [jnp.where(first, q2, zero), jnp.where(first, zero, q2)], axis=0)
    kw = k_ref[k_start:k_start + n_keys, :]
    s = lax.dot_general(kw, qs, _NT, preferred_element_type=jnp.float32)
    s = s + tbl_ref[0, WIN_ROWS - n_keys:, :]
    return s, jnp.max(s, axis=0, keepdims=True)


def _attn_values(vt_ref, p, g):
    k_start, n_keys = _key_window(g)
    vt_aug = jnp.concatenate(
        [vt_ref[:, k_start:k_start + n_keys],
         jnp.ones((BF16_SUBLANES, n_keys), jnp.bfloat16)], axis=0)
    ot = jnp.dot(vt_aug, p, preferred_element_type=jnp.float32)
    r = 1.0 / ot[LANES:LANES + 1, :]
    return jnp.concatenate(
        [ot[:ATT_DH, :GROUP_ROWS] * r[:, :GROUP_ROWS],
         ot[ATT_DH:LANES, GROUP_ROWS:] * r[:, GROUP_ROWS:]], axis=0)


def _attn_kernel(q_ref, k_ref, vt_ref, g_ref, tbl_ref, z_ref, *, seq):
    n_stages = seq // (STAGE_GROUPS * GROUP_ROWS)

    def scores(t):
        return [_attn_scores(q_ref, k_ref, tbl_ref, STAGE_GROUPS * t + j)
                for j in range(STAGE_GROUPS)]

    cur = scores(0)
    for t in range(n_stages):
        if t + 1 < n_stages:
            nxt = scores(t + 1)
        probs = [jnp.exp2(s - m).astype(jnp.bfloat16) for s, m in cur]
        ot = jnp.concatenate(
            [_attn_values(vt_ref, p, STAGE_GROUPS * t + j) for j, p in enumerate(probs)],
            axis=1)
        rows = slice(t * STAGE_GROUPS * GROUP_ROWS, (t + 1) * STAGE_GROUPS * GROUP_ROWS)
        gate = g_ref[rows, :].astype(jnp.float32)
        z_ref[rows, :] = (ot.T * gate * _sigmoid(gate)).astype(jnp.bfloat16)
        if t + 1 < n_stages:
            cur = nxt


def _attn(proj, v_t, table, batch, seq):
    n = proj.shape[0]
    pairs = ATT_HEADS // HEADS_PER_STEP
    col_map = lambda off: (lambda p, b: (b, off // LANES + p))
    return pl.pallas_call(
        functools.partial(_attn_kernel, seq=seq),
        grid=(pairs, batch),
        in_specs=[
            pl.BlockSpec((seq, LANES), col_map(QA_OFF)),
            pl.BlockSpec((seq, LANES), col_map(KA_OFF)),
            pl.BlockSpec((LANES, seq), lambda p, b: (p, b)),
            pl.BlockSpec((seq, LANES), col_map(GA_OFF)),
            pl.BlockSpec((1, WIN_ROWS, PAIR_COLS), lambda p, b: (p, 0, 0)),
        ],
        out_specs=pl.BlockSpec((seq, LANES), lambda p, b: (b, p)),
        out_shape=jax.ShapeDtypeStruct((n, ATT_WIDTH), jnp.bfloat16),
        compiler_params=pltpu.CompilerParams(
            dimension_semantics=("arbitrary", "arbitrary"),
            vmem_limit_bytes=VMEM_LIMIT,
        ),
        name="attn",
    )(proj, proj, v_t, proj, table)


def _bias_table(rel_bias):
    heads = rel_bias.shape[0]
    pairs = heads // HEADS_PER_STEP
    t = np.arange(TABLE_PERIOD)
    dist = np.where(t < GROUP_ROWS, t, t - TABLE_PERIOD) + (WIN_ROWS - GROUP_ROWS)
    rel_idx = np.clip(dist, -MAX_REL, MAX_REL) + MAX_REL
    w = rel_bias[:, rel_idx].astype(jnp.float32) * LOG2_E
    return pl.pallas_call(
        _bias_table_kernel,
        grid=(pairs,),
        in_specs=[pl.BlockSpec((HEADS_PER_STEP, 1, TABLE_PERIOD), lambda p: (p, 0, 0))],
        out_specs=pl.BlockSpec((1, WIN_ROWS, PAIR_COLS), lambda p: (p, 0, 0)),
        out_shape=jax.ShapeDtypeStruct((pairs, WIN_ROWS, PAIR_COLS), jnp.float32),
        compiler_params=pltpu.CompilerParams(
            dimension_semantics=("arbitrary",), vmem_limit_bytes=VMEM_LIMIT),
        name="bias_table",
    )(w.reshape(heads, 1, TABLE_PERIOD))


def _bias_table_kernel(w_ref, o_ref):
    j = lax.broadcasted_iota(jnp.int32, (WIN_ROWS, GROUP_ROWS), 0)
    i = lax.broadcasted_iota(jnp.int32, (WIN_ROWS, GROUP_ROWS), 1)
    dc = j // CHUNK - i // CHUNK
    valid = (dc >= 0) & (dc <= LEFT_CHUNKS)
    for hh in range(HEADS_PER_STEP):
        rows = jnp.broadcast_to(w_ref[hh], (WIN_ROWS, TABLE_PERIOD))
        toep = pltpu.roll(rows, 0, axis=1, stride=1, stride_axis=0)[:, :GROUP_ROWS]
        o_ref[0, :, hh * GROUP_ROWS:(hh + 1) * GROUP_ROWS] = jnp.where(valid, toep, NEG_INF)


def _merge_kernel(zg_ref, za_ref, lg_ref, la_ref, x_ref, wog_ref, woa_ref,
                  wout_ref, mb_ref, gp_ref, o_ref):
    n_sub = MERGE_TM // MERGE_SUB

    def rows(j):
        return slice(j * MERGE_SUB, (j + 1) * MERGE_SUB)

    def branches(j):
        yg = jnp.dot(zg_ref[rows(j), :], wog_ref[...], preferred_element_type=jnp.float32)
        ya = jnp.dot(za_ref[rows(j), :], woa_ref[...], preferred_element_type=jnp.float32)
        gate_g = _sigmoid(lg_ref[rows(j), :].astype(jnp.float32) + mb_ref[0:1, :])
        gate_a = _sigmoid(la_ref[rows(j), :].astype(jnp.float32) + mb_ref[1:2, :])
        return (gate_g * yg + gate_a * ya).astype(jnp.bfloat16)

    def finish(j, y):
        ms = jnp.mean(y * y, axis=-1, keepdims=True)
        o_ref[rows(j), :] = x_ref[rows(j), :] + y * lax.rsqrt(ms + RMS_EPS) * gp_ref[...]

    merged = branches(0)
    y_prev = None
    for j in range(n_sub):
        if j + 1 < n_sub:
            merged_next = branches(j + 1)
        y = jnp.dot(merged, wout_ref[...], preferred_element_type=jnp.float32)
        if y_prev is not None:
            finish(j - 1, y_prev)
        y_prev = y
        if j + 1 < n_sub:
            merged = merged_next
    finish(n_sub - 1, y_prev)


def _merge(z_gla, z_att, proj, x2, w_o_gla, w_o_att, w_out, merge_bias2, g_post):
    n = x2.shape[0]
    row = lambda i: (i, 0)
    const = lambda i: (0, 0)
    return pl.pallas_call(
        _merge_kernel,
        grid=(n // MERGE_TM,),
        in_specs=[
            pl.BlockSpec((MERGE_TM, D_MODEL), row),
            pl.BlockSpec((MERGE_TM, D_MODEL), row),
            pl.BlockSpec((MERGE_TM, D_MODEL), lambda i: (i, GATE_OFF // D_MODEL)),
            pl.BlockSpec((MERGE_TM, D_MODEL), lambda i: (i, GATE_OFF // D_MODEL + 1)),
            pl.BlockSpec((MERGE_TM, D_MODEL), row),
            pl.BlockSpec((D_MODEL, D_MODEL), const, pipeline_mode=pl.Buffered(1)),
            pl.BlockSpec((D_MODEL, D_MODEL), const, pipeline_mode=pl.Buffered(1)),
            pl.BlockSpec((D_MODEL, D_MODEL), const, pipeline_mode=pl.Buffered(1)),
            pl.BlockSpec((2, D_MODEL), const),
            pl.BlockSpec((1, D_MODEL), const),
        ],
        out_specs=pl.BlockSpec((MERGE_TM, D_MODEL), row),
        out_shape=jax.ShapeDtypeStruct((n, D_MODEL), jnp.float32),
        compiler_params=pltpu.CompilerParams(
            dimension_semantics=("arbitrary",),
            vmem_limit_bytes=VMEM_LIMIT,
        ),
        name="merge",
    )(z_gla, z_att, proj, proj, x2, w_o_gla, w_o_att, w_out, merge_bias2, g_post)


def kernel(x, norm_pre_g, w_in, gk_up, gk_bias, gla_norm_g, rel_bias, w_o_gla,
           w_o_att, merge_bias, w_out, norm_post_g):
    batch, seq, d = x.shape
    assert d == D_MODEL and seq % FRONT_ROWS == 0 and seq % GROUP_ROWS == 0
    assert (batch * seq) % MERGE_TM == 0
    assert rel_bias.shape == (ATT_HEADS, 2 * MAX_REL + 1)
    bf16 = jnp.bfloat16

    splits = np.cumsum([GLA_K_WIDTH, GLA_K_WIDTH, GLA_V_WIDTH, GLA_V_WIDTH, GK_RANK,
                        ATT_WIDTH, ATT_WIDTH, ATT_WIDTH, ATT_WIDTH])
    (w_qg, w_kg, w_vg, w_gg, w_code, w_qa, w_ka, w_va, w_ga, w_gate) = jnp.split(
        w_in, [int(s) for s in splits], axis=1)
    g_col = norm_pre_g[:, None]
    w_main = (g_col * jnp.concatenate(
        [w_qa * (ATT_DH ** -0.5 * LOG2_E), w_ka, w_ga, w_gate, w_vg, w_gg, w_qg, w_kg],
        axis=1)).astype(bf16)
    w_vt_code = (g_col * jnp.concatenate([w_va, w_code], axis=1)).astype(bf16).T

    x2 = x.reshape(batch * seq, D_MODEL)
    proj, v_t, z_gla = _front(x2, w_main, w_vt_code, gk_up.astype(bf16),
                              gk_bias.reshape(1, GLA_K_WIDTH), seq)
    z_att = _attn(proj, v_t, _bias_table(rel_bias), batch, seq)
    w_og = (jnp.tile(gla_norm_g, GLA_HEADS)[:, None] * w_o_gla).astype(bf16)
    out = _merge(z_gla, z_att, proj, x2, w_og, w_o_att.astype(bf16),
                 w_out.astype(bf16), merge_bias.reshape(2, D_MODEL),
                 norm_post_g.reshape(1, D_MODEL))
    return out.reshape(batch, seq, D_MODEL)
```

```python
import functools

import numpy as np
import jax
import jax.numpy as jnp
from jax import lax
from jax.experimental import pallas as pl
from jax.experimental.pallas import tpu as pltpu

D_MODEL = 1024
CHUNK = 64
GLA_HEADS = 4
GLA_DK = 128
GLA_DV = 256
GLA_K_WIDTH = GLA_HEADS * GLA_DK
GLA_V_WIDTH = GLA_HEADS * GLA_DV
GK_RANK = 16
GATE_NORMALIZER = 16.0
ATT_HEADS = 16
ATT_DH = 64
ATT_WIDTH = ATT_HEADS * ATT_DH
LEFT_CHUNKS = 8
MAX_REL = 256
RMS_EPS = 1e-6
NEG_INF = -1e30

LANES = 128
BF16_SUBLANES = 16
LOG2_E = float(np.log2(np.e))
LN_2 = float(np.log(2.0))

QA_OFF = 0
KA_OFF = 1024
GA_OFF = 2048
GATE_OFF = 3072
PROJ_COLS = 5120
VG_OFF = 5120
GG_OFF = 6144
QG_OFF = 7168
KG_OFF = 7680
W_COLS = 8192
GK_PAD = LANES

_SRC_QG, _SRC_KG, _SRC_VG, _SRC_GG, _SRC_CODE, _SRC_QA, _SRC_KA, _SRC_VA, _SRC_GA, _SRC_GATE = (
    int(v) for v in np.cumsum([0, GLA_K_WIDTH, GLA_K_WIDTH, GLA_V_WIDTH, GLA_V_WIDTH, GK_RANK,
                               ATT_WIDTH, ATT_WIDTH, ATT_WIDTH, ATT_WIDTH]))
IN_COLS = _SRC_GATE + 2 * D_MODEL
REGROUP = (
    (QA_OFF, _SRC_QA, ATT_WIDTH), (KA_OFF, _SRC_KA, ATT_WIDTH), (GA_OFF, _SRC_GA, ATT_WIDTH),
    (GATE_OFF, _SRC_GATE, 2 * D_MODEL), (VG_OFF, _SRC_VG, GLA_V_WIDTH),
    (GG_OFF, _SRC_GG, GLA_V_WIDTH), (QG_OFF, _SRC_QG, GLA_K_WIDTH), (KG_OFF, _SRC_KG, GLA_K_WIDTH),
)
WPREP_ROWS = 128

GROUP = 2
STAGE_GROUPS = 2
GROUP_ROWS = GROUP * CHUNK
WIN_CHUNKS = GROUP + LEFT_CHUNKS
WIN_ROWS = WIN_CHUNKS * CHUNK
HEADS_PER_STEP = LANES // ATT_DH
PAIR_COLS = HEADS_PER_STEP * GROUP_ROWS
TABLE_PERIOD = GROUP_ROWS + WIN_ROWS

FRONT_ROWS = 512
MERGE_TM = 512
VMEM_LIMIT = 56 * 1024 * 1024

_NT = (((1,), (1,)), ((), ()))


def _sigmoid(x):
    return 1.0 / (1.0 + jnp.exp2(x * (-LOG2_E)))


def _wprep_kernel(w_ref, main_ref, vt_ref, gk_ref):
    for dst, src, width in REGROUP:
        blk = w_ref[:, src:src + width]
        if dst == QA_OFF:
            blk = blk * (ATT_DH ** -0.5 * LOG2_E)
        main_ref[:, dst:dst + width] = blk.astype(jnp.bfloat16)
    vt_ref[...] = w_ref[:, _SRC_VA:_SRC_VA + ATT_WIDTH].T.astype(jnp.bfloat16)
    code = w_ref[:, _SRC_CODE:_SRC_CODE + GK_PAD]
    lane = lax.broadcasted_iota(jnp.int32, code.shape, 1)
    gk_ref[...] = jnp.where(lane < GK_RANK, code, 0.0).astype(jnp.bfloat16)


def _wprep(w_in):
    k = w_in.shape[0]
    return pl.pallas_call(
        _wprep_kernel,
        grid=(k // WPREP_ROWS,),
        in_specs=[pl.BlockSpec((WPREP_ROWS, IN_COLS), lambda i: (i, 0))],
        out_specs=[
            pl.BlockSpec((WPREP_ROWS, W_COLS), lambda i: (i, 0)),
            pl.BlockSpec((ATT_WIDTH, WPREP_ROWS), lambda i: (0, i)),
            pl.BlockSpec((WPREP_ROWS, GK_PAD), lambda i: (i, 0)),
        ],
        out_shape=[
            jax.ShapeDtypeStruct((k, W_COLS), jnp.bfloat16),
            jax.ShapeDtypeStruct((ATT_WIDTH, k), jnp.bfloat16),
            jax.ShapeDtypeStruct((k, GK_PAD), jnp.bfloat16),
        ],
        compiler_params=pltpu.CompilerParams(
            dimension_semantics=("arbitrary",), vmem_limit_bytes=VMEM_LIMIT),
        name="wprep",
    )(w_in)


def _chunk_cumsum(x):
    n_chunks = x.shape[0] // CHUNK
    hi = x.astype(jnp.bfloat16)
    lo = (x - hi.astype(jnp.float32)).astype(jnp.bfloat16)
    r = lax.broadcasted_iota(jnp.int32, (CHUNK, 2 * CHUNK), 0)
    c = lax.broadcasted_iota(jnp.int32, (CHUNK, 2 * CHUNK), 1)
    tri = (r >= jnp.bitwise_and(c, CHUNK - 1)).astype(jnp.bfloat16)
    out = []
    for n in range(n_chunks):
        rows = slice(n * CHUNK, (n + 1) * CHUNK)
        terms = jnp.concatenate([hi[rows], lo[rows]], axis=0)
        out.append(jnp.dot(tri, terms, preferred_element_type=jnp.float32))
    return jnp.stack(out, axis=0)


def _front_kernel(x_ref, g_ref, w_ref, wvt_ref, wgk_ref, up_ref, bias_ref,
                  proj_ref, vt_ref, z_ref, state_ref, *, steps_per_seq):
    n_chunks = FRONT_ROWS // CHUNK

    @pl.when(pl.program_id(0) % steps_per_seq == 0)
    def _():
        state_ref[...] = jnp.zeros_like(state_ref)

    x = x_ref[...]
    ms = jnp.mean(x * x, axis=-1, keepdims=True)
    h = (x * lax.rsqrt(ms + RMS_EPS) * g_ref[...]).astype(jnp.bfloat16)

    def project(off, width):
        return jnp.dot(h, w_ref[:, off:off + width],
                       preferred_element_type=jnp.float32).astype(jnp.bfloat16)

    def emit_proj(off):
        def run():
            proj_ref[:, off:off + D_MODEL] = project(off, D_MODEL)
        return run

    def emit_vt():
        vt_ref[...] = lax.dot_general(
            wvt_ref[...], h, _NT, preferred_element_type=jnp.float32
        ).astype(jnp.bfloat16)

    filler = [emit_proj(off) for off in range(0, PROJ_COLS, D_MODEL)] + [emit_vt]

    code = jnp.dot(h, wgk_ref[...], preferred_element_type=jnp.float32).astype(jnp.bfloat16)
    v = project(VG_OFF, GLA_V_WIDTH)
    qk = project(QG_OFF, 2 * GLA_K_WIDTH)
    q, k = qk[:, :GLA_K_WIDTH], qk[:, GLA_K_WIDTH:]
    logits = jnp.dot(code, up_ref[...],
                     preferred_element_type=jnp.float32) + bias_ref[...]
    gate_all = project(GG_OFF, GLA_V_WIDTH)
    gk = (jnp.minimum(logits, 0.0) * (1.0 / GATE_NORMALIZER)
          - jnp.log2(1.0 + jnp.exp2(jnp.abs(logits) * (-LOG2_E)))
          * (LN_2 / GATE_NORMALIZER))
    a_cum3 = _chunk_cumsum(gk)
    filler.pop(0)()
    a_end = a_cum3[:, CHUNK - 1, :]
    k_dec = (k.astype(jnp.float32).reshape(n_chunks, CHUNK, GLA_K_WIDTH)
             * jnp.exp(a_end[:, None, :] - a_cum3)).astype(jnp.bfloat16)
    decay_t = jnp.exp(a_end).T
    kcols = [slice(hd * GLA_DK, (hd + 1) * GLA_DK) for hd in range(GLA_HEADS)]
    vcols = [slice(hd * GLA_DV, (hd + 1) * GLA_DV) for hd in range(GLA_HEADS)]

    def chunk_rows(c):
        return slice(c * CHUNK, (c + 1) * CHUNK)

    def increments(c):
        return [lax.dot_general(
            k_dec[c][:, kcols[hd]], v[chunk_rows(c), vcols[hd]],
            (((0,), (0,)), ((), ())), preferred_element_type=jnp.float32)
            for hd in range(GLA_HEADS)]

    def finish(c, reads):
        for hd in range(GLA_HEADS):
            o = reads[hd]
            ms_o = jnp.mean(o * o, axis=-1, keepdims=True)
            factor = lax.rsqrt(ms_o + RMS_EPS * GLA_DK)
            gate = gate_all[chunk_rows(c), vcols[hd]].astype(jnp.float32)
            z_ref[chunk_rows(c), vcols[hd]] = (
                (o * factor) * (gate * _sigmoid(gate))).astype(jnp.bfloat16)

    upd_next = increments(0)
    reads_prev = None
    for c in range(n_chunks):
        upd = upd_next
        if c + 1 < n_chunks:
            upd_next = increments(c + 1)
        if c % 2 == 0 and filler:
            filler.pop(0)()
        reads = []
        for hd in range(GLA_HEADS):
            state = decay_t[kcols[hd], c:c + 1] * state_ref[hd] + upd[hd]
            state_ref[hd] = state
            reads.append(jnp.dot(q[chunk_rows(c), kcols[hd]],
                                 state.astype(jnp.bfloat16),
                                 preferred_element_type=jnp.float32))
        if reads_prev is not None:
            finish(c - 1, reads_prev)
        reads_prev = reads
    finish(n_chunks - 1, reads_prev)
    while filler:
        filler.pop(0)()


def _front(x2, g_pre, w_main, w_vt, w_gk, up_pad, gk_bias, seq):
    n = x2.shape[0]
    const = lambda i: (0, 0)
    resident = pl.Buffered(1)
    return pl.pallas_call(
        functools.partial(_front_kernel, steps_per_seq=seq // FRONT_ROWS),
        grid=(n // FRONT_ROWS,),
        in_specs=[
            pl.BlockSpec((FRONT_ROWS, D_MODEL), lambda i: (i, 0)),
            pl.BlockSpec((1, D_MODEL), const),
            pl.BlockSpec((D_MODEL, W_COLS), const, pipeline_mode=resident),
            pl.BlockSpec((ATT_WIDTH, D_MODEL), const, pipeline_mode=resident),
            pl.BlockSpec((D_MODEL, GK_PAD), const, pipeline_mode=resident),
            pl.BlockSpec((GK_PAD, GLA_K_WIDTH), const, pipeline_mode=resident),
            pl.BlockSpec((1, GLA_K_WIDTH), const),
        ],
        out_specs=[
            pl.BlockSpec((FRONT_ROWS, PROJ_COLS), lambda i: (i, 0)),
            pl.BlockSpec((ATT_WIDTH, FRONT_ROWS), lambda i: (0, i)),
            pl.BlockSpec((FRONT_ROWS, GLA_V_WIDTH), lambda i: (i, 0)),
        ],
        out_shape=[
            jax.ShapeDtypeStruct((n, PROJ_COLS), jnp.bfloat16),
            jax.ShapeDtypeStruct((ATT_WIDTH, n), jnp.bfloat16),
            jax.ShapeDtypeStruct((n, GLA_V_WIDTH), jnp.bfloat16),
        ],
        scratch_shapes=[pltpu.VMEM((GLA_HEADS, GLA_DK, GLA_DV), jnp.float32)],
        compiler_params=pltpu.CompilerParams(
            dimension_semantics=("arbitrary",),
            vmem_limit_bytes=VMEM_LIMIT,
        ),
        name="front",
    )(x2, g_pre, w_main, w_vt, w_gk, up_pad, gk_bias)


def _key_window(g):
    lead = LEFT_CHUNKS // GROUP
    k_start = max(g - lead, 0) * GROUP_ROWS
    return k_start, (g + 1) * GROUP_ROWS - k_start


def _attn_scores(q_ref, k_ref, tbl_ref, g):
    k_start, n_keys = _key_window(g)
    q2 = q_ref[g * GROUP_ROWS:(g + 1) * GROUP_ROWS, :]
    lane = lax.broadcasted_iota(jnp.int32, q2.shape, 1)
    first = lane < ATT_DH
    zero = jnp.zeros_like(q2)
    qs = jnp.concatenate(
        [jnp.where(first, q2, zero), jnp.where(first, zero, q2)], axis=0)
    kw = k_ref[k_start:k_start + n_keys, :]
    s = lax.dot_general(kw, qs, _NT, preferred_element_type=jnp.float32)
    s = s + tbl_ref[0, WIN_ROWS - n_keys:, :]
    return s, jnp.max(s, axis=0, keepdims=True)


def _attn_values(vt_ref, p, g):
    k_start, n_keys = _key_window(g)
    vt_aug = jnp.concatenate(
        [vt_ref[:, k_start:k_start + n_keys],
         jnp.ones((BF16_SUBLANES, n_keys), jnp.bfloat16)], axis=0)
    ot = jnp.dot(vt_aug, p, preferred_element_type=jnp.float32)
    r = 1.0 / ot[LANES:LANES + 1, :]
    return jnp.concatenate(
        [ot[:ATT_DH, :GROUP_ROWS] * r[:, :GROUP_ROWS],
         ot[ATT_DH:LANES, GROUP_ROWS:] * r[:, GROUP_ROWS:]], axis=0)


def _attn_kernel(q_ref, k_ref, vt_ref, g_ref, tbl_ref, z_ref, *, seq):
    n_stages = seq // (STAGE_GROUPS * GROUP_ROWS)

    def scores(t):
        return [_attn_scores(q_ref, k_ref, tbl_ref, STAGE_GROUPS * t + j)
                for j in range(STAGE_GROUPS)]

    cur = scores(0)
    for t in range(n_stages):
        if t + 1 < n_stages:
            nxt = scores(t + 1)
        probs = [jnp.exp2(s - m).astype(jnp.bfloat16) for s, m in cur]
        ot = jnp.concatenate(
            [_attn_values(vt_ref, p, STAGE_GROUPS * t + j) for j, p in enumerate(probs)],
            axis=1)
        rows = slice(t * STAGE_GROUPS * GROUP_ROWS, (t + 1) * STAGE_GROUPS * GROUP_ROWS)
        gate = g_ref[rows, :].astype(jnp.float32)
        z_ref[rows, :] = (ot.T * gate * _sigmoid(gate)).astype(jnp.bfloat16)
        if t + 1 < n_stages:
            cur = nxt


def _attn(proj, v_t, table, batch, seq):
    n = proj.shape[0]
    pairs = ATT_HEADS // HEADS_PER_STEP
    col_map = lambda off: (lambda p, b: (b, off // LANES + p))
    return pl.pallas_call(
        functools.partial(_attn_kernel, seq=seq),
        grid=(pairs, batch),
        in_specs=[
            pl.BlockSpec((seq, LANES), col_map(QA_OFF)),
            pl.BlockSpec((seq, LANES), col_map(KA_OFF)),
            pl.BlockSpec((LANES, seq), lambda p, b: (p, b)),
            pl.BlockSpec((seq, LANES), col_map(GA_OFF)),
            pl.BlockSpec((1, WIN_ROWS, PAIR_COLS), lambda p, b: (p, 0, 0)),
        ],
        out_specs=pl.BlockSpec((seq, LANES), lambda p, b: (b, p)),
        out_shape=jax.ShapeDtypeStruct((n, ATT_WIDTH), jnp.bfloat16),
        compiler_params=pltpu.CompilerParams(
            dimension_semantics=("arbitrary", "arbitrary"),
            vmem_limit_bytes=VMEM_LIMIT,
        ),
        name="attn",
    )(proj, proj, v_t, proj, table)


def _bias_table(rel_bias):
    heads = rel_bias.shape[0]
    pairs = heads // HEADS_PER_STEP
    t = np.arange(TABLE_PERIOD)
    dist = np.where(t < GROUP_ROWS, t, t - TABLE_PERIOD) + (WIN_ROWS - GROUP_ROWS)
    rel_idx = np.clip(dist, -MAX_REL, MAX_REL) + MAX_REL
    w = rel_bias[:, rel_idx].astype(jnp.float32) * LOG2_E
    return pl.pallas_call(
        _bias_table_kernel,
        grid=(pairs,),
        in_specs=[pl.BlockSpec((HEADS_PER_STEP, 1, TABLE_PERIOD), lambda p: (p, 0, 0))],
        out_specs=pl.BlockSpec((1, WIN_ROWS, PAIR_COLS), lambda p: (p, 0, 0)),
        out_shape=jax.ShapeDtypeStruct((pairs, WIN_ROWS, PAIR_COLS), jnp.float32),
        compiler_params=pltpu.CompilerParams(
            dimension_semantics=("arbitrary",), vmem_limit_bytes=VMEM_LIMIT),
        name="bias_table",
    )(w.reshape(heads, 1, TABLE_PERIOD))


def _bias_table_kernel(w_ref, o_ref):
    j = lax.broadcasted_iota(jnp.int32, (WIN_ROWS, GROUP_ROWS), 0)
    i = lax.broadcasted_iota(jnp.int32, (WIN_ROWS, GROUP_ROWS), 1)
    dc = j // CHUNK - i // CHUNK
    valid = (dc >= 0) & (dc <= LEFT_CHUNKS)
    for hh in range(HEADS_PER_STEP):
        rows = jnp.broadcast_to(w_ref[hh], (WIN_ROWS, TABLE_PERIOD))
        toep = pltpu.roll(rows, 0, axis=1, stride=1, stride_axis=0)[:, :GROUP_ROWS]
        o_ref[0, :, hh * GROUP_ROWS:(hh + 1) * GROUP_ROWS] = jnp.where(valid, toep, NEG_INF)


def _merge_kernel(zg_ref, za_ref, lg_ref, la_ref, x_ref, wog_ref, woa_ref,
                  wout_ref, mb_ref, gp_ref, o_ref):
    yg = jnp.dot(zg_ref[...], wog_ref[...], preferred_element_type=jnp.float32)
    ya = jnp.dot(za_ref[...], woa_ref[...], preferred_element_type=jnp.float32)
    gate_g = _sigmoid(lg_ref[...].astype(jnp.float32) + mb_ref[0:1, :])
    gate_a = _sigmoid(la_ref[...].astype(jnp.float32) + mb_ref[1:2, :])
    merged = (gate_g * yg + gate_a * ya).astype(jnp.bfloat16)
    y = jnp.dot(merged, wout_ref[...], preferred_element_type=jnp.float32)
    ms = jnp.mean(y * y, axis=-1, keepdims=True)
    o_ref[...] = x_ref[...] + y * lax.rsqrt(ms + RMS_EPS) * gp_ref[...]


def _merge(z_gla, z_att, proj, x2, w_o_gla, w_o_att, w_out, merge_bias2, g_post):
    n = x2.shape[0]
    row = lambda i: (i, 0)
    const = lambda i: (0, 0)
    return pl.pallas_call(
        _merge_kernel,
        grid=(n // MERGE_TM,),
        in_specs=[
            pl.BlockSpec((MERGE_TM, D_MODEL), row),
            pl.BlockSpec((MERGE_TM, D_MODEL), row),
            pl.BlockSpec((MERGE_TM, D_MODEL), lambda i: (i, GATE_OFF // D_MODEL)),
            pl.BlockSpec((MERGE_TM, D_MODEL), lambda i: (i, GATE_OFF // D_MODEL + 1)),
            pl.BlockSpec((MERGE_TM, D_MODEL), row),
            pl.BlockSpec((D_MODEL, D_MODEL), const),
            pl.BlockSpec((D_MODEL, D_MODEL), const),
            pl.BlockSpec((D_MODEL, D_MODEL), const),
            pl.BlockSpec((2, D_MODEL), const),
            pl.BlockSpec((1, D_MODEL), const),
        ],
        out_specs=pl.BlockSpec((MERGE_TM, D_MODEL), row),
        out_shape=jax.ShapeDtypeStruct((n, D_MODEL), jnp.float32),
        compiler_params=pltpu.CompilerParams(
            dimension_semantics=("arbitrary",),
            vmem_limit_bytes=VMEM_LIMIT,
        ),
        name="merge",
    )(z_gla, z_att, proj, proj, x2, w_o_gla, w_o_att, w_out, merge_bias2, g_post)


def kernel(x, norm_pre_g, w_in, gk_up, gk_bias, gla_norm_g, rel_bias, w_o_gla,
           w_o_att, merge_bias, w_out, norm_post_g):
    batch, seq, d = x.shape
    assert d == D_MODEL and seq % FRONT_ROWS == 0 and seq % GROUP_ROWS == 0
    assert (batch * seq) % MERGE_TM == 0
    assert w_in.shape == (D_MODEL, IN_COLS)
    assert rel_bias.shape == (ATT_HEADS, 2 * MAX_REL + 1)
    bf16 = jnp.bfloat16

    w_main, w_vt, w_gk = _wprep(w_in)
    up_pad = jnp.pad(gk_up, ((0, GK_PAD - GK_RANK), (0, 0))).astype(bf16)

    x2 = x.reshape(batch * seq, D_MODEL)
    proj, v_t, z_gla = _front(x2, norm_pre_g.reshape(1, D_MODEL), w_main, w_vt, w_gk,
                              up_pad, gk_bias.reshape(1, GLA_K_WIDTH), seq)
    z_att = _attn(proj, v_t, _bias_table(rel_bias), batch, seq)
    w_og = (jnp.tile(gla_norm_g, GLA_HEADS)[:, None] * w_o_gla).astype(bf16)
    out = _merge(z_gla, z_att, proj, x2, w_og, w_o_att.astype(bf16),
                 w_out.astype(bf16), merge_bias.reshape(2, D_MODEL),
                 norm_post_g.reshape(1, D_MODEL))
    return out.reshape(batch, seq, D_MODEL)
```

```python
import functools

import numpy as np
import jax
import jax.numpy as jnp
from jax import lax
from jax.experimental import pallas as pl
from jax.experimental.pallas import tpu as pltpu

D_MODEL = 1024
CHUNK = 64
GLA_HEADS = 4
GLA_DK = 128
GLA_DV = 256
GLA_K_WIDTH = GLA_HEADS * GLA_DK
GLA_V_WIDTH = GLA_HEADS * GLA_DV
GK_RANK = 16
GATE_NORMALIZER = 16.0
ATT_HEADS = 16
ATT_DH = 64
ATT_WIDTH = ATT_HEADS * ATT_DH
LEFT_CHUNKS = 8
MAX_REL = 256
RMS_EPS = 1e-6
NEG_INF = -1e30

LANES = 128
BF16_SUBLANES = 16
LOG2_E = float(np.log2(np.e))
LN_2 = float(np.log(2.0))

QA_OFF = 0
KA_OFF = 1024
GA_OFF = 2048
GATE_OFF = 3072
PROJ_COLS = 5120
VG_OFF = 5120
GG_OFF = 6144
QG_OFF = 7168
KG_OFF = 7680
VA_OFF = 8192
CODE_OFF = 9216
W_ROWS = CODE_OFF + GK_RANK

_SRC_QG, _SRC_KG, _SRC_VG, _SRC_GG, _SRC_CODE, _SRC_QA, _SRC_KA, _SRC_VA, _SRC_GA, _SRC_GATE = (
    int(v) for v in np.cumsum([0, GLA_K_WIDTH, GLA_K_WIDTH, GLA_V_WIDTH, GLA_V_WIDTH, GK_RANK,
                               ATT_WIDTH, ATT_WIDTH, ATT_WIDTH, ATT_WIDTH]))
REGROUP = (
    (QA_OFF, _SRC_QA, ATT_WIDTH), (KA_OFF, _SRC_KA, ATT_WIDTH), (GA_OFF, _SRC_GA, ATT_WIDTH),
    (GATE_OFF, _SRC_GATE, 2 * D_MODEL), (VG_OFF, _SRC_VG, GLA_V_WIDTH),
    (GG_OFF, _SRC_GG, GLA_V_WIDTH), (QG_OFF, _SRC_QG, GLA_K_WIDTH), (KG_OFF, _SRC_KG, GLA_K_WIDTH),
    (VA_OFF, _SRC_VA, ATT_WIDTH), (CODE_OFF, _SRC_CODE, GK_RANK),
)

GROUP = 2
STAGE_GROUPS = 2
GROUP_ROWS = GROUP * CHUNK
WIN_CHUNKS = GROUP + LEFT_CHUNKS
WIN_ROWS = WIN_CHUNKS * CHUNK
HEADS_PER_STEP = LANES // ATT_DH
PAIR_COLS = HEADS_PER_STEP * GROUP_ROWS
TABLE_PERIOD = GROUP_ROWS + WIN_ROWS

FRONT_ROWS = 512
MERGE_TM = 512
VMEM_LIMIT = 56 * 1024 * 1024

_NT = (((1,), (1,)), ((), ()))


def _sigmoid(x):
    return 1.0 / (1.0 + jnp.exp2(x * (-LOG2_E)))


def _chunk_cumsum(x):
    n_chunks = x.shape[0] // CHUNK
    hi = x.astype(jnp.bfloat16)
    lo = (x - hi.astype(jnp.float32)).astype(jnp.bfloat16)
    r = lax.broadcasted_iota(jnp.int32, (CHUNK, 2 * CHUNK), 0)
    c = lax.broadcasted_iota(jnp.int32, (CHUNK, 2 * CHUNK), 1)
    tri = (r >= jnp.bitwise_and(c, CHUNK - 1)).astype(jnp.bfloat16)
    out = []
    for n in range(n_chunks):
        rows = slice(n * CHUNK, (n + 1) * CHUNK)
        terms = jnp.concatenate([hi[rows], lo[rows]], axis=0)
        out.append(jnp.dot(tri, terms, preferred_element_type=jnp.float32))
    return jnp.stack(out, axis=0)


def _front_kernel(x_ref, g_ref, w_ref, up_ref, bias_ref,
                  proj_ref, vt_ref, z_ref, state_ref, *, steps_per_seq):
    n_chunks = FRONT_ROWS // CHUNK

    @pl.when(pl.program_id(0) % steps_per_seq == 0)
    def _():
        state_ref[...] = jnp.zeros_like(state_ref)

    x = x_ref[...]
    ms = jnp.mean(x * x, axis=-1, keepdims=True)
    h = (x * lax.rsqrt(ms + RMS_EPS) * g_ref[...]).astype(jnp.bfloat16)

    def project(off, width):
        return lax.dot_general(h, w_ref[off:off + width, :], _NT,
                               preferred_element_type=jnp.float32).astype(jnp.bfloat16)

    def emit_proj(off):
        def run():
            proj_ref[:, off:off + D_MODEL] = project(off, D_MODEL)
        return run

    filler = [emit_proj(off) for off in range(0, PROJ_COLS, D_MODEL)]

    vt_code = lax.dot_general(w_ref[VA_OFF:W_ROWS, :], h, _NT,
                              preferred_element_type=jnp.float32)
    vt_ref[...] = vt_code[:ATT_WIDTH].astype(jnp.bfloat16)
    code_t = vt_code[ATT_WIDTH:].astype(jnp.bfloat16)
    v = project(VG_OFF, GLA_V_WIDTH)
    qk = project(QG_OFF, 2 * GLA_K_WIDTH)
    q, k = qk[:, :GLA_K_WIDTH], qk[:, GLA_K_WIDTH:]
    logits = lax.dot_general(code_t, up_ref[...], (((0,), (0,)), ((), ())),
                             preferred_element_type=jnp.float32) + bias_ref[...]
    gate_all = project(GG_OFF, GLA_V_WIDTH)
    gk = (jnp.minimum(logits, 0.0) * (1.0 / GATE_NORMALIZER)
          - jnp.log2(1.0 + jnp.exp2(jnp.abs(logits) * (-LOG2_E)))
          * (LN_2 / GATE_NORMALIZER))
    a_cum3 = _chunk_cumsum(gk)
    filler.pop(0)()
    a_end = a_cum3[:, CHUNK - 1, :]
    k_dec = (k.astype(jnp.float32).reshape(n_chunks, CHUNK, GLA_K_WIDTH)
             * jnp.exp(a_end[:, None, :] - a_cum3)).astype(jnp.bfloat16)
    decay_t = jnp.exp(a_end).T
    kcols = [slice(hd * GLA_DK, (hd + 1) * GLA_DK) for hd in range(GLA_HEADS)]
    vcols = [slice(hd * GLA_DV, (hd + 1) * GLA_DV) for hd in range(GLA_HEADS)]

    def chunk_rows(c):
        return slice(c * CHUNK, (c + 1) * CHUNK)

    def increments(c):
        return [lax.dot_general(
            k_dec[c][:, kcols[hd]], v[chunk_rows(c), vcols[hd]],
            (((0,), (0,)), ((), ())), preferred_element_type=jnp.float32)
            for hd in range(GLA_HEADS)]

    def finish(c, reads):
        for hd in range(GLA_HEADS):
            o = reads[hd]
            ms_o = jnp.mean(o * o, axis=-1, keepdims=True)
            factor = lax.rsqrt(ms_o + RMS_EPS * GLA_DK)
            gate = gate_all[chunk_rows(c), vcols[hd]].astype(jnp.float32)
            z_ref[chunk_rows(c), vcols[hd]] = (
                (o * factor) * (gate * _sigmoid(gate))).astype(jnp.bfloat16)

    upd_next = increments(0)
    reads_prev = None
    for c in range(n_chunks):
        upd = upd_next
        if c + 1 < n_chunks:
            upd_next = increments(c + 1)
        if c % 2 == 0 and filler:
            filler.pop(0)()
        reads = []
        for hd in range(GLA_HEADS):
            state = decay_t[kcols[hd], c:c + 1] * state_ref[hd] + upd[hd]
            state_ref[hd] = state
            reads.append(jnp.dot(q[chunk_rows(c), kcols[hd]],
                                 state.astype(jnp.bfloat16),
                                 preferred_element_type=jnp.float32))
        if reads_prev is not None:
            finish(c - 1, reads_prev)
        reads_prev = reads
    finish(n_chunks - 1, reads_prev)
    while filler:
        filler.pop(0)()


def _front(x2, g_pre, w_all, gk_up, gk_bias, seq):
    n = x2.shape[0]
    const = lambda i: (0, 0)
    resident = pl.Buffered(1)
    return pl.pallas_call(
        functools.partial(_front_kernel, steps_per_seq=seq // FRONT_ROWS),
        grid=(n // FRONT_ROWS,),
        in_specs=[
            pl.BlockSpec((FRONT_ROWS, D_MODEL), lambda i: (i, 0)),
            pl.BlockSpec((1, D_MODEL), const),
            pl.BlockSpec((W_ROWS, D_MODEL), const, pipeline_mode=resident),
            pl.BlockSpec((GK_RANK, GLA_K_WIDTH), const, pipeline_mode=resident),
            pl.BlockSpec((1, GLA_K_WIDTH), const),
        ],
        out_specs=[
            pl.BlockSpec((FRONT_ROWS, PROJ_COLS), lambda i: (i, 0)),
            pl.BlockSpec((ATT_WIDTH, FRONT_ROWS), lambda i: (0, i)),
            pl.BlockSpec((FRONT_ROWS, GLA_V_WIDTH), lambda i: (i, 0)),
        ],
        out_shape=[
            jax.ShapeDtypeStruct((n, PROJ_COLS), jnp.bfloat16),
            jax.ShapeDtypeStruct((ATT_WIDTH, n), jnp.bfloat16),
            jax.ShapeDtypeStruct((n, GLA_V_WIDTH), jnp.bfloat16),
        ],
        scratch_shapes=[pltpu.VMEM((GLA_HEADS, GLA_DK, GLA_DV), jnp.float32)],
        compiler_params=pltpu.CompilerParams(
            dimension_semantics=("arbitrary",),
            vmem_limit_bytes=VMEM_LIMIT,
        ),
        name="front",
    )(x2, g_pre, w_all, gk_up, gk_bias)


def _key_window(g):
    lead = LEFT_CHUNKS // GROUP
    k_start = max(g - lead, 0) * GROUP_ROWS
    return k_start, (g + 1) * GROUP_ROWS - k_start


def _attn_scores(q_ref, k_ref, tbl_ref, g):
    k_start, n_keys = _key_window(g)
    q2 = q_ref[g * GROUP_ROWS:(g + 1) * GROUP_ROWS, :]
    lane = lax.broadcasted_iota(jnp.int32, q2.shape, 1)
    first = lane < ATT_DH
    zero = jnp.zeros_like(q2)
    qs = jnp.concatenate(
        [jnp.where(first, q2, zero), jnp.where(first, zero, q2)], axis=0)
    kw = k_ref[k_start:k_start + n_keys, :]
    s = lax.dot_general(kw, qs, _NT, preferred_element_type=jnp.float32)
    s = s + tbl_ref[0, WIN_ROWS - n_keys:, :]
    return s, jnp.max(s, axis=0, keepdims=True)


def _attn_values(vt_ref, p, g):
    k_start, n_keys = _key_window(g)
    vt_aug = jnp.concatenate(
        [vt_ref[:, k_start:k_start + n_keys],
         jnp.ones((BF16_SUBLANES, n_keys), jnp.bfloat16)], axis=0)
    ot = jnp.dot(vt_aug, p, preferred_element_type=jnp.float32)
    r = 1.0 / ot[LANES:LANES + 1, :]
    return jnp.concatenate(
        [ot[:ATT_DH, :GROUP_ROWS] * r[:, :GROUP_ROWS],
         ot[ATT_DH:LANES, GROUP_ROWS:] * r[:, GROUP_ROWS:]], axis=0)


def _attn_kernel(q_ref, k_ref, vt_ref, g_ref, tbl_ref, z_ref, *, seq):
    n_stages = seq // (STAGE_GROUPS * GROUP_ROWS)

    def scores(t):
        return [_attn_scores(q_ref, k_ref, tbl_ref, STAGE_GROUPS * t + j)
                for j in range(STAGE_GROUPS)]

    cur = scores(0)
    for t in range(n_stages):
        if t + 1 < n_stages:
            nxt = scores(t + 1)
        probs = [jnp.exp2(s - m).astype(jnp.bfloat16) for s, m in cur]
        ot = jnp.concatenate(
            [_attn_values(vt_ref, p, STAGE_GROUPS * t + j) for j, p in enumerate(probs)],
            axis=1)
        rows = slice(t * STAGE_GROUPS * GROUP_ROWS, (t + 1) * STAGE_GROUPS * GROUP_ROWS)
        gate = g_ref[rows, :].astype(jnp.float32)
        z_ref[rows, :] = (ot.T * gate * _sigmoid(gate)).astype(jnp.bfloat16)
        if t + 1 < n_stages:
            cur = nxt


def _attn(proj, v_t, table, batch, seq):
    n = proj.shape[0]
    pairs = ATT_HEADS // HEADS_PER_STEP
    col_map = lambda off: (lambda p, b: (b, off // LANES + p))
    return pl.pallas_call(
        functools.partial(_attn_kernel, seq=seq),
        grid=(pairs, batch),
        in_specs=[
            pl.BlockSpec((seq, LANES), col_map(QA_OFF)),
            pl.BlockSpec((seq, LANES), col_map(KA_OFF)),
            pl.BlockSpec((LANES, seq), lambda p, b: (p, b)),
            pl.BlockSpec((seq, LANES), col_map(GA_OFF)),
            pl.BlockSpec((1, WIN_ROWS, PAIR_COLS), lambda p, b: (p, 0, 0)),
        ],
        out_specs=pl.BlockSpec((seq, LANES), lambda p, b: (b, p)),
        out_shape=jax.ShapeDtypeStruct((n, ATT_WIDTH), jnp.bfloat16),
        compiler_params=pltpu.CompilerParams(
            dimension_semantics=("arbitrary", "arbitrary"),
            vmem_limit_bytes=VMEM_LIMIT,
        ),
        name="attn",
    )(proj, proj, v_t, proj, table)


def _bias_table(rel_bias):
    heads = rel_bias.shape[0]
    pairs = heads // HEADS_PER_STEP
    t = np.arange(TABLE_PERIOD)
    dist = np.where(t < GROUP_ROWS, t, t - TABLE_PERIOD) + (WIN_ROWS - GROUP_ROWS)
    rel_idx = np.clip(dist, -MAX_REL, MAX_REL) + MAX_REL
    w = rel_bias[:, rel_idx].astype(jnp.float32) * LOG2_E
    return pl.pallas_call(
        _bias_table_kernel,
        grid=(pairs,),
        in_specs=[pl.BlockSpec((HEADS_PER_STEP, 1, TABLE_PERIOD), lambda p: (p, 0, 0))],
        out_specs=pl.BlockSpec((1, WIN_ROWS, PAIR_COLS), lambda p: (p, 0, 0)),
        out_shape=jax.ShapeDtypeStruct((pairs, WIN_ROWS, PAIR_COLS), jnp.float32),
        compiler_params=pltpu.CompilerParams(
            dimension_semantics=("arbitrary",), vmem_limit_bytes=VMEM_LIMIT),
        name="bias_table",
    )(w.reshape(heads, 1, TABLE_PERIOD))


def _bias_table_kernel(w_ref, o_ref):
    j = lax.broadcasted_iota(jnp.int32, (WIN_ROWS, GROUP_ROWS), 0)
    i = lax.broadcasted_iota(jnp.int32, (WIN_ROWS, GROUP_ROWS), 1)
    dc = j // CHUNK - i // CHUNK
    valid = (dc >= 0) & (dc <= LEFT_CHUNKS)
    for hh in range(HEADS_PER_STEP):
        rows = jnp.broadcast_to(w_ref[hh], (WIN_ROWS, TABLE_PERIOD))
        toep = pltpu.roll(rows, 0, axis=1, stride=1, stride_axis=0)[:, :GROUP_ROWS]
        o_ref[0, :, hh * GROUP_ROWS:(hh + 1) * GROUP_ROWS] = jnp.where(valid, toep, NEG_INF)


def _merge_kernel(zg_ref, za_ref, lg_ref, la_ref, x_ref, wog_ref, woa_ref,
                  wout_ref, mb_ref, gp_ref, o_ref):
    yg = jnp.dot(zg_ref[...], wog_ref[...], preferred_element_type=jnp.float32)
    ya = jnp.dot(za_ref[...], woa_ref[...], preferred_element_type=jnp.float32)
    gate_g = _sigmoid(lg_ref[...].astype(jnp.float32) + mb_ref[0:1, :])
    gate_a = _sigmoid(la_ref[...].astype(jnp.float32) + mb_ref[1:2, :])
    merged = (gate_g * yg + gate_a * ya).astype(jnp.bfloat16)
    y = jnp.dot(merged, wout_ref[...], preferred_element_type=jnp.float32)
    ms = jnp.mean(y * y, axis=-1, keepdims=True)
    o_ref[...] = x_ref[...] + y * lax.rsqrt(ms + RMS_EPS) * gp_ref[...]


def _merge(z_gla, z_att, proj, x2, w_o_gla, w_o_att, w_out, merge_bias2, g_post):
    n = x2.shape[0]
    row = lambda i: (i, 0)
    const = lambda i: (0, 0)
    return pl.pallas_call(
        _merge_kernel,
        grid=(n // MERGE_TM,),
        in_specs=[
            pl.BlockSpec((MERGE_TM, D_MODEL), row),
            pl.BlockSpec((MERGE_TM, D_MODEL), row),
            pl.BlockSpec((MERGE_TM, D_MODEL), lambda i: (i, GATE_OFF // D_MODEL)),
            pl.BlockSpec((MERGE_TM, D_MODEL), lambda i: (i, GATE_OFF // D_MODEL + 1)),
            pl.BlockSpec((MERGE_TM, D_MODEL), row),
            pl.BlockSpec((D_MODEL, D_MODEL), const),
            pl.BlockSpec((D_MODEL, D_MODEL), const),
            pl.BlockSpec((D_MODEL, D_MODEL), const),
            pl.BlockSpec((2, D_MODEL), const),
            pl.BlockSpec((1, D_MODEL), const),
        ],
        out_specs=pl.BlockSpec((MERGE_TM, D_MODEL), row),
        out_shape=jax.ShapeDtypeStruct((n, D_MODEL), jnp.float32),
        compiler_params=pltpu.CompilerParams(
            dimension_semantics=("arbitrary",),
            vmem_limit_bytes=VMEM_LIMIT,
        ),
        name="merge",
    )(z_gla, z_att, proj, proj, x2, w_o_gla, w_o_att, w_out, merge_bias2, g_post)


def kernel(x, norm_pre_g, w_in, gk_up, gk_bias, gla_norm_g, rel_bias, w_o_gla,
           w_o_att, merge_bias, w_out, norm_post_g):
    batch, seq, d = x.shape
    assert d == D_MODEL and seq % FRONT_ROWS == 0 and seq % GROUP_ROWS == 0
    assert (batch * seq) % MERGE_TM == 0
    assert w_in.shape == (D_MODEL, W_ROWS)
    assert rel_bias.shape == (ATT_HEADS, 2 * MAX_REL + 1)
    bf16 = jnp.bfloat16

    w_t = w_in.T
    scale = {QA_OFF: ATT_DH ** -0.5 * LOG2_E}
    w_all = jnp.concatenate(
        [w_t[src:src + width] * scale.get(dst, 1.0) for dst, src, width in REGROUP],
        axis=0).astype(bf16)

    x2 = x.reshape(batch * seq, D_MODEL)
    proj, v_t, z_gla = _front(x2, norm_pre_g.reshape(1, D_MODEL), w_all, gk_up.astype(bf16),
                              gk_bias.reshape(1, GLA_K_WIDTH), seq)
    z_att = _attn(proj, v_t, _bias_table(rel_bias), batch, seq)
    w_og = (jnp.tile(gla_norm_g, GLA_HEADS)[:, None] * w_o_gla).astype(bf16)
    out = _merge(z_gla, z_att, proj, x2, w_og, w_o_att.astype(bf16),
                 w_out.astype(bf16), merge_bias.reshape(2, D_MODEL),
                 norm_post_g.reshape(1, D_MODEL))
    return out.reshape(batch, seq, D_MODEL)
```

```python
import functools

import numpy as np
import jax
import jax.numpy as jnp
from jax import lax
from jax.experimental import pallas as pl
from jax.experimental.pallas import tpu as pltpu

D_MODEL = 1024
CHUNK = 64
GLA_HEADS = 4
GLA_DK = 128
GLA_DV = 256
GLA_K_WIDTH = GLA_HEADS * GLA_DK
GLA_V_WIDTH = GLA_HEADS * GLA_DV
GK_RANK = 16
GATE_NORMALIZER = 16.0
ATT_HEADS = 16
ATT_DH = 64
ATT_WIDTH = ATT_HEADS * ATT_DH
LEFT_CHUNKS = 8
MAX_REL = 256
RMS_EPS = 1e-6
NEG_INF = -1e30

LANES = 128
BF16_SUBLANES = 16
LOG2_E = float(np.log2(np.e))
LN_2 = float(np.log(2.0))

QA_OFF = 0
KA_OFF = 1024
GA_OFF = 2048
GATE_OFF = 3072
PROJ_COLS = 5120
VG_OFF = 5120
GG_OFF = 6144
QG_OFF = 7168
KG_OFF = 7680
W_COLS = 8192
GK_PAD = LANES

_SRC_QG, _SRC_KG, _SRC_VG, _SRC_GG, _SRC_CODE, _SRC_QA, _SRC_KA, _SRC_VA, _SRC_GA, _SRC_GATE = (
    int(v) for v in np.cumsum([0, GLA_K_WIDTH, GLA_K_WIDTH, GLA_V_WIDTH, GLA_V_WIDTH, GK_RANK,
                               ATT_WIDTH, ATT_WIDTH, ATT_WIDTH, ATT_WIDTH]))
IN_COLS = _SRC_GATE + 2 * D_MODEL
REGROUP = (
    (QA_OFF, _SRC_QA, ATT_WIDTH), (KA_OFF, _SRC_KA, ATT_WIDTH), (GA_OFF, _SRC_GA, ATT_WIDTH),
    (GATE_OFF, _SRC_GATE, 2 * D_MODEL), (VG_OFF, _SRC_VG, GLA_V_WIDTH),
    (GG_OFF, _SRC_GG, GLA_V_WIDTH), (QG_OFF, _SRC_QG, GLA_K_WIDTH), (KG_OFF, _SRC_KG, GLA_K_WIDTH),
)
WPREP_ROWS = 128

GROUP = 2
STAGE_GROUPS = 2
GROUP_ROWS = GROUP * CHUNK
WIN_CHUNKS = GROUP + LEFT_CHUNKS
WIN_ROWS = WIN_CHUNKS * CHUNK
HEADS_PER_STEP = LANES // ATT_DH
PAIR_COLS = HEADS_PER_STEP * GROUP_ROWS
TABLE_PERIOD = GROUP_ROWS + WIN_ROWS

FRONT_ROWS = 512
MERGE_TM = 512
VMEM_LIMIT = 56 * 1024 * 1024

_NT = (((1,), (1,)), ((), ()))


def _sigmoid(x):
    return 1.0 / (1.0 + jnp.exp2(x * (-LOG2_E)))


def _wprep_kernel(wt_ref, main_ref, vt_ref, gk_ref):
    for dst, src, width in REGROUP:
        blk = wt_ref[src:src + width, :].T
        if dst == QA_OFF:
            blk = blk * (ATT_DH ** -0.5 * LOG2_E)
        main_ref[:, dst:dst + width] = blk.astype(jnp.bfloat16)
    vt_ref[...] = wt_ref[_SRC_VA:_SRC_VA + ATT_WIDTH, :].astype(jnp.bfloat16)
    code = wt_ref[_SRC_CODE:_SRC_CODE + GK_PAD, :].T
    lane = lax.broadcasted_iota(jnp.int32, code.shape, 1)
    gk_ref[...] = jnp.where(lane < GK_RANK, code, 0.0).astype(jnp.bfloat16)


def _wprep(w_in):
    k = w_in.shape[0]
    return pl.pallas_call(
        _wprep_kernel,
        grid=(k // WPREP_ROWS,),
        in_specs=[pl.BlockSpec((IN_COLS, WPREP_ROWS), lambda i: (0, i))],
        out_specs=[
            pl.BlockSpec((WPREP_ROWS, W_COLS), lambda i: (i, 0)),
            pl.BlockSpec((ATT_WIDTH, WPREP_ROWS), lambda i: (0, i)),
            pl.BlockSpec((WPREP_ROWS, GK_PAD), lambda i: (i, 0)),
        ],
        out_shape=[
            jax.ShapeDtypeStruct((k, W_COLS), jnp.bfloat16),
            jax.ShapeDtypeStruct((ATT_WIDTH, k), jnp.bfloat16),
            jax.ShapeDtypeStruct((k, GK_PAD), jnp.bfloat16),
        ],
        compiler_params=pltpu.CompilerParams(
            dimension_semantics=("arbitrary",), vmem_limit_bytes=VMEM_LIMIT),
        name="wprep",
    )(w_in.T)


def _chunk_cumsum(x):
    n_chunks = x.shape[0] // CHUNK
    hi = x.astype(jnp.bfloat16)
    lo = (x - hi.astype(jnp.float32)).astype(jnp.bfloat16)
    r = lax.broadcasted_iota(jnp.int32, (CHUNK, 2 * CHUNK), 0)
    c = lax.broadcasted_iota(jnp.int32, (CHUNK, 2 * CHUNK), 1)
    tri = (r >= jnp.bitwise_and(c, CHUNK - 1)).astype(jnp.bfloat16)
    out = []
    for n in range(n_chunks):
        rows = slice(n * CHUNK, (n + 1) * CHUNK)
        terms = jnp.concatenate([hi[rows], lo[rows]], axis=0)
        out.append(jnp.dot(tri, terms, preferred_element_type=jnp.float32))
    return jnp.stack(out, axis=0)


def _front_kernel(x_ref, g_ref, w_ref, wvt_ref, wgk_ref, up_ref, bias_ref,
                  proj_ref, vt_ref, z_ref, state_ref, *, steps_per_seq):
    n_chunks = FRONT_ROWS // CHUNK

    @pl.when(pl.program_id(0) % steps_per_seq == 0)
    def _():
        state_ref[...] = jnp.zeros_like(state_ref)

    x = x_ref[...]
    ms = jnp.mean(x * x, axis=-1, keepdims=True)
    h = (x * lax.rsqrt(ms + RMS_EPS) * g_ref[...]).astype(jnp.bfloat16)

    def project(off, width):
        return jnp.dot(h, w_ref[:, off:off + width],
                       preferred_element_type=jnp.float32).astype(jnp.bfloat16)

    def emit_proj(off):
        def run():
            proj_ref[:, off:off + D_MODEL] = project(off, D_MODEL)
        return run

    def emit_vt():
        vt_ref[...] = lax.dot_general(
            wvt_ref[...], h, _NT, preferred_element_type=jnp.float32
        ).astype(jnp.bfloat16)

    filler = [emit_proj(off) for off in range(0, PROJ_COLS, D_MODEL)] + [emit_vt]

    code = jnp.dot(h, wgk_ref[...], preferred_element_type=jnp.float32).astype(jnp.bfloat16)
    v = project(VG_OFF, GLA_V_WIDTH)
    qk = project(QG_OFF, 2 * GLA_K_WIDTH)
    q, k = qk[:, :GLA_K_WIDTH], qk[:, GLA_K_WIDTH:]
    logits = jnp.dot(code, up_ref[...],
                     preferred_element_type=jnp.float32) + bias_ref[...]
    gate_all = project(GG_OFF, GLA_V_WIDTH)
    gk = (jnp.minimum(logits, 0.0) * (1.0 / GATE_NORMALIZER)
          - jnp.log2(1.0 + jnp.exp2(jnp.abs(logits) * (-LOG2_E)))
          * (LN_2 / GATE_NORMALIZER))
    a_cum3 = _chunk_cumsum(gk)
    filler.pop(0)()
    a_end = a_cum3[:, CHUNK - 1, :]
    k_dec = (k.astype(jnp.float32).reshape(n_chunks, CHUNK, GLA_K_WIDTH)
             * jnp.exp(a_end[:, None, :] - a_cum3)).astype(jnp.bfloat16)
    decay_t = jnp.exp(a_end).T
    kcols = [slice(hd * GLA_DK, (hd + 1) * GLA_DK) for hd in range(GLA_HEADS)]
    vcols = [slice(hd * GLA_DV, (hd + 1) * GLA_DV) for hd in range(GLA_HEADS)]

    def chunk_rows(c):
        return slice(c * CHUNK, (c + 1) * CHUNK)

    def increments(c):
        return [lax.dot_general(
            k_dec[c][:, kcols[hd]], v[chunk_rows(c), vcols[hd]],
            (((0,), (0,)), ((), ())), preferred_element_type=jnp.float32)
            for hd in range(GLA_HEADS)]

    def finish(c, reads):
        for hd in range(GLA_HEADS):
            o = reads[hd]
            ms_o = jnp.mean(o * o, axis=-1, keepdims=True)
            factor = lax.rsqrt(ms_o + RMS_EPS * GLA_DK)
            gate = gate_all[chunk_rows(c), vcols[hd]].astype(jnp.float32)
            z_ref[chunk_rows(c), vcols[hd]] = (
                (o * factor) * (gate * _sigmoid(gate))).astype(jnp.bfloat16)

    upd_next = increments(0)
    reads_prev = None
    for c in range(n_chunks):
        upd = upd_next
        if c + 1 < n_chunks:
            upd_next = increments(c + 1)
        if c % 2 == 0 and filler:
            filler.pop(0)()
        reads = []
        for hd in range(GLA_HEADS):
            state = decay_t[kcols[hd], c:c + 1] * state_ref[hd] + upd[hd]
            state_ref[hd] = state
            reads.append(jnp.dot(q[chunk_rows(c), kcols[hd]],
                                 state.astype(jnp.bfloat16),
                                 preferred_element_type=jnp.float32))
        if reads_prev is not None:
            finish(c - 1, reads_prev)
        reads_prev = reads
    finish(n_chunks - 1, reads_prev)
    while filler:
        filler.pop(0)()


def _front(x2, g_pre, w_main, w_vt, w_gk, up_pad, gk_bias, seq):
    n = x2.shape[0]
    const = lambda i: (0, 0)
    resident = pl.Buffered(1)
    return pl.pallas_call(
        functools.partial(_front_kernel, steps_per_seq=seq // FRONT_ROWS),
        grid=(n // FRONT_ROWS,),
        in_specs=[
            pl.BlockSpec((FRONT_ROWS, D_MODEL), lambda i: (i, 0)),
            pl.BlockSpec((1, D_MODEL), const),
            pl.BlockSpec((D_MODEL, W_COLS), const, pipeline_mode=resident),
            pl.BlockSpec((ATT_WIDTH, D_MODEL), const, pipeline_mode=resident),
            pl.BlockSpec((D_MODEL, GK_PAD), const, pipeline_mode=resident),
            pl.BlockSpec((GK_PAD, GLA_K_WIDTH), const, pipeline_mode=resident),
            pl.BlockSpec((1, GLA_K_WIDTH), const),
        ],
        out_specs=[
            pl.BlockSpec((FRONT_ROWS, PROJ_COLS), lambda i: (i, 0)),
            pl.BlockSpec((ATT_WIDTH, FRONT_ROWS), lambda i: (0, i)),
            pl.BlockSpec((FRONT_ROWS, GLA_V_WIDTH), lambda i: (i, 0)),
        ],
        out_shape=[
            jax.ShapeDtypeStruct((n, PROJ_COLS), jnp.bfloat16),
            jax.ShapeDtypeStruct((ATT_WIDTH, n), jnp.bfloat16),
            jax.ShapeDtypeStruct((n, GLA_V_WIDTH), jnp.bfloat16),
        ],
        scratch_shapes=[pltpu.VMEM((GLA_HEADS, GLA_DK, GLA_DV), jnp.float32)],
        compiler_params=pltpu.CompilerParams(
            dimension_semantics=("arbitrary",),
            vmem_limit_bytes=VMEM_LIMIT,
        ),
        name="front",
    )(x2, g_pre, w_main, w_vt, w_gk, up_pad, gk_bias)


def _key_window(g):
    lead = LEFT_CHUNKS // GROUP
    k_start = max(g - lead, 0) * GROUP_ROWS
    return k_start, (g + 1) * GROUP_ROWS - k_start


def _attn_scores(q_ref, k_ref, tbl_ref, g):
    k_start, n_keys = _key_window(g)
    q2 = q_ref[g * GROUP_ROWS:(g + 1) * GROUP_ROWS, :]
    lane = lax.broadcasted_iota(jnp.int32, q2.shape, 1)
    first = lane < ATT_DH
    zero = jnp.zeros_like(q2)
    qs = jnp.concatenate(
        [jnp.where(first, q2, zero), jnp.where(first, zero, q2)], axis=0)
    kw = k_ref[k_start:k_start + n_keys, :]
    s = lax.dot_general(kw, qs, _NT, preferred_element_type=jnp.float32)
    s = s + tbl_ref[0, WIN_ROWS - n_keys:, :]
    return s, jnp.max(s, axis=0, keepdims=True)


def _attn_values(vt_ref, p, g):
    k_start, n_keys = _key_window(g)
    vt_aug = jnp.concatenate(
        [vt_ref[:, k_start:k_start + n_keys],
         jnp.ones((BF16_SUBLANES, n_keys), jnp.bfloat16)], axis=0)
    ot = jnp.dot(vt_aug, p, preferred_element_type=jnp.float32)
    r = 1.0 / ot[LANES:LANES + 1, :]
    return jnp.concatenate(
        [ot[:ATT_DH, :GROUP_ROWS] * r[:, :GROUP_ROWS],
         ot[ATT_DH:LANES, GROUP_ROWS:] * r[:, GROUP_ROWS:]], axis=0)


def _attn_kernel(q_ref, k_ref, vt_ref, g_ref, tbl_ref, z_ref, *, seq):
    n_stages = seq // (STAGE_GROUPS * GROUP_ROWS)

    def scores(t):
        return [_attn_scores(q_ref, k_ref, tbl_ref, STAGE_GROUPS * t + j)
                for j in range(STAGE_GROUPS)]

    cur = scores(0)
    for t in range(n_stages):
        if t + 1 < n_stages:
            nxt = scores(t + 1)
        probs = [jnp.exp2(s - m).astype(jnp.bfloat16) for s, m in cur]
        ot = jnp.concatenate(
            [_attn_values(vt_ref, p, STAGE_GROUPS * t + j) for j, p in enumerate(probs)],
            axis=1)
        rows = slice(t * STAGE_GROUPS * GROUP_ROWS, (t + 1) * STAGE_GROUPS * GROUP_ROWS)
        gate = g_ref[rows, :].astype(jnp.float32)
        z_ref[rows, :] = (ot.T * gate * _sigmoid(gate)).astype(jnp.bfloat16)
        if t + 1 < n_stages:
            cur = nxt


def _attn(proj, v_t, table, batch, seq):
    n = proj.shape[0]
    pairs = ATT_HEADS // HEADS_PER_STEP
    col_map = lambda off: (lambda p, b: (b, off // LANES + p))
    return pl.pallas_call(
        functools.partial(_attn_kernel, seq=seq),
        grid=(pairs, batch),
        in_specs=[
            pl.BlockSpec((seq, LANES), col_map(QA_OFF)),
            pl.BlockSpec((seq, LANES), col_map(KA_OFF)),
            pl.BlockSpec((LANES, seq), lambda p, b: (p, b)),
            pl.BlockSpec((seq, LANES), col_map(GA_OFF)),
            pl.BlockSpec((1, WIN_ROWS, PAIR_COLS), lambda p, b: (p, 0, 0)),
        ],
        out_specs=pl.BlockSpec((seq, LANES), lambda p, b: (b, p)),
        out_shape=jax.ShapeDtypeStruct((n, ATT_WIDTH), jnp.bfloat16),
        compiler_params=pltpu.CompilerParams(
            dimension_semantics=("arbitrary", "arbitrary"),
            vmem_limit_bytes=VMEM_LIMIT,
        ),
        name="attn",
    )(proj, proj, v_t, proj, table)


def _bias_table(rel_bias):
    heads = rel_bias.shape[0]
    pairs = heads // HEADS_PER_STEP
    t = np.arange(TABLE_PERIOD)
    dist = np.where(t < GROUP_ROWS, t, t - TABLE_PERIOD) + (WIN_ROWS - GROUP_ROWS)
    rel_idx = np.clip(dist, -MAX_REL, MAX_REL) + MAX_REL
    w = rel_bias[:, rel_idx].astype(jnp.float32) * LOG2_E
    return pl.pallas_call(
        _bias_table_kernel,
        grid=(pairs,),
        in_specs=[pl.BlockSpec((HEADS_PER_STEP, 1, TABLE_PERIOD), lambda p: (p, 0, 0))],
        out_specs=pl.BlockSpec((1, WIN_ROWS, PAIR_COLS), lambda p: (p, 0, 0)),
        out_shape=jax.ShapeDtypeStruct((pairs, WIN_ROWS, PAIR_COLS), jnp.float32),
        compiler_params=pltpu.CompilerParams(
            dimension_semantics=("arbitrary",), vmem_limit_bytes=VMEM_LIMIT),
        name="bias_table",
    )(w.reshape(heads, 1, TABLE_PERIOD))


def _bias_table_kernel(w_ref, o_ref):
    j = lax.broadcasted_iota(jnp.int32, (WIN_ROWS, GROUP_ROWS), 0)
    i = lax.broadcasted_iota(jnp.int32, (WIN_ROWS, GROUP_ROWS), 1)
    dc = j // CHUNK - i // CHUNK
    valid = (dc >= 0) & (dc <= LEFT_CHUNKS)
    for hh in range(HEADS_PER_STEP):
        rows = jnp.broadcast_to(w_ref[hh], (WIN_ROWS, TABLE_PERIOD))
        toep = pltpu.roll(rows, 0, axis=1, stride=1, stride_axis=0)[:, :GROUP_ROWS]
        o_ref[0, :, hh * GROUP_ROWS:(hh + 1) * GROUP_ROWS] = jnp.where(valid, toep, NEG_INF)


def _merge_kernel(zg_ref, za_ref, lg_ref, la_ref, x_ref, wog_ref, woa_ref,
                  wout_ref, mb_ref, gp_ref, o_ref):
    yg = jnp.dot(zg_ref[...], wog_ref[...], preferred_element_type=jnp.float32)
    ya = jnp.dot(za_ref[...], woa_ref[...], preferred_element_type=jnp.float32)
    gate_g = _sigmoid(lg_ref[...].astype(jnp.float32) + mb_ref[0:1, :])
    gate_a = _sigmoid(la_ref[...].astype(jnp.float32) + mb_ref[1:2, :])
    merged = (gate_g * yg + gate_a * ya).astype(jnp.bfloat16)
    y = jnp.dot(merged, wout_ref[...], preferred_element_type=jnp.float32)
    ms = jnp.mean(y * y, axis=-1, keepdims=True)
    o_ref[...] = x_ref[...] + y * lax.rsqrt(ms + RMS_EPS) * gp_ref[...]


def _merge(z_gla, z_att, proj, x2, w_o_gla, w_o_att, w_out, merge_bias2, g_post):
    n = x2.shape[0]
    row = lambda i: (i, 0)
    const = lambda i: (0, 0)
    return pl.pallas_call(
        _merge_kernel,
        grid=(n // MERGE_TM,),
        in_specs=[
            pl.BlockSpec((MERGE_TM, D_MODEL), row),
            pl.BlockSpec((MERGE_TM, D_MODEL), row),
            pl.BlockSpec((MERGE_TM, D_MODEL), lambda i: (i, GATE_OFF // D_MODEL)),
            pl.BlockSpec((MERGE_TM, D_MODEL), lambda i: (i, GATE_OFF // D_MODEL + 1)),
            pl.BlockSpec((MERGE_TM, D_MODEL), row),
            pl.BlockSpec((D_MODEL, D_MODEL), const),
            pl.BlockSpec((D_MODEL, D_MODEL), const),
            pl.BlockSpec((D_MODEL, D_MODEL), const),
            pl.BlockSpec((2, D_MODEL), const),
            pl.BlockSpec((1, D_MODEL), const),
        ],
        out_specs=pl.BlockSpec((MERGE_TM, D_MODEL), row),
        out_shape=jax.ShapeDtypeStruct((n, D_MODEL), jnp.float32),
        compiler_params=pltpu.CompilerParams(
            dimension_semantics=("arbitrary",),
            vmem_limit_bytes=VMEM_LIMIT,
        ),
        name="merge",
    )(z_gla, z_att, proj, proj, x2, w_o_gla, w_o_att, w_out, merge_bias2, g_post)


def kernel(x, norm_pre_g, w_in, gk_up, gk_bias, gla_norm_g, rel_bias, w_o_gla,
           w_o_att, merge_bias, w_out, norm_post_g):
    batch, seq, d = x.shape
    assert d == D_MODEL and seq % FRONT_ROWS == 0 and seq % GROUP_ROWS == 0
    assert (batch * seq) % MERGE_TM == 0
    assert w_in.shape == (D_MODEL, IN_COLS)
    assert rel_bias.shape == (ATT_HEADS, 2 * MAX_REL + 1)
    bf16 = jnp.bfloat16

    w_main, w_vt, w_gk = _wprep(w_in)
    up_pad = jnp.pad(gk_up, ((0, GK_PAD - GK_RANK), (0, 0))).astype(bf16)

    x2 = x.reshape(batch * seq, D_MODEL)
    proj, v_t, z_gla = _front(x2, norm_pre_g.reshape(1, D_MODEL), w_main, w_vt, w_gk,
                              up_pad, gk_bias.reshape(1, GLA_K_WIDTH), seq)
    z_att = _attn(proj, v_t, _bias_table(rel_bias), batch, seq)
    w_og = (jnp.tile(gla_norm_g, GLA_HEADS)[:, None] * w_o_gla).astype(bf16)
    out = _merge(z_gla, z_att, proj, x2, w_og, w_o_att.astype(bf16),
                 w_out.astype(bf16), merge_bias.reshape(2, D_MODEL),
                 norm_post_g.reshape(1, D_MODEL))
    return out.reshape(batch, seq, D_MODEL)
```

```python
import functools

import numpy as np
import jax
import jax.numpy as jnp
from jax import lax
from jax.experimental import pallas as pl
from jax.experimental.pallas import tpu as pltpu

D_MODEL = 1024
CHUNK = 64
GLA_HEADS = 4
GLA_DK = 128
GLA_DV = 256
GLA_K_WIDTH = GLA_HEADS * GLA_DK
GLA_V_WIDTH = GLA_HEADS * GLA_DV
GK_RANK = 16
GATE_NORMALIZER = 16.0
ATT_HEADS = 16
ATT_DH = 64
ATT_WIDTH = ATT_HEADS * ATT_DH
LEFT_CHUNKS = 8
MAX_REL = 256
RMS_EPS = 1e-6
NEG_INF = -1e30

LANES = 128
BF16_SUBLANES = 16
LOG2_E = float(np.log2(np.e))
LN_2 = float(np.log(2.0))

QA_OFF = 0
KA_OFF = 1024
GA_OFF = 2048
GATE_OFF = 3072
PROJ_COLS = 5120
VG_OFF = 5120
GG_OFF = 6144
QG_OFF = 7168
KG_OFF = 7680
W_COLS = 8192
GK_PAD = LANES

_SRC_QG, _SRC_KG, _SRC_VG, _SRC_GG, _SRC_CODE, _SRC_QA, _SRC_KA, _SRC_VA, _SRC_GA, _SRC_GATE = (
    int(v) for v in np.cumsum([0, GLA_K_WIDTH, GLA_K_WIDTH, GLA_V_WIDTH, GLA_V_WIDTH, GK_RANK,
                               ATT_WIDTH, ATT_WIDTH, ATT_WIDTH, ATT_WIDTH]))
IN_COLS = _SRC_GATE + 2 * D_MODEL
REGROUP = (
    (QA_OFF, _SRC_QA, ATT_WIDTH), (KA_OFF, _SRC_KA, ATT_WIDTH), (GA_OFF, _SRC_GA, ATT_WIDTH),
    (GATE_OFF, _SRC_GATE, 2 * D_MODEL), (VG_OFF, _SRC_VG, GLA_V_WIDTH),
    (GG_OFF, _SRC_GG, GLA_V_WIDTH), (QG_OFF, _SRC_QG, GLA_K_WIDTH), (KG_OFF, _SRC_KG, GLA_K_WIDTH),
)
WPREP_ROWS = 128

GROUP = 2
STAGE_GROUPS = 2
GROUP_ROWS = GROUP * CHUNK
WIN_CHUNKS = GROUP + LEFT_CHUNKS
WIN_ROWS = WIN_CHUNKS * CHUNK
HEADS_PER_STEP = LANES // ATT_DH
PAIR_COLS = HEADS_PER_STEP * GROUP_ROWS
TABLE_PERIOD = GROUP_ROWS + WIN_ROWS
ATTN_SEQS = 4

FRONT_ROWS = 512
MERGE_TM = 512
VMEM_LIMIT = 56 * 1024 * 1024

_NT = (((1,), (1,)), ((), ()))


def _sigmoid(x):
    return 1.0 / (1.0 + jnp.exp2(x * (-LOG2_E)))


def _wprep_kernel(wt_ref, main_ref, vt_ref, gk_ref):
    for dst, src, width in REGROUP:
        blk = wt_ref[src:src + width, :].T
        if dst == QA_OFF:
            blk = blk * (ATT_DH ** -0.5 * LOG2_E)
        main_ref[:, dst:dst + width] = blk.astype(jnp.bfloat16)
    vt_ref[...] = wt_ref[_SRC_VA:_SRC_VA + ATT_WIDTH, :].astype(jnp.bfloat16)
    code = wt_ref[_SRC_CODE:_SRC_CODE + GK_PAD, :].T
    lane = lax.broadcasted_iota(jnp.int32, code.shape, 1)
    gk_ref[...] = jnp.where(lane < GK_RANK, code, 0.0).astype(jnp.bfloat16)


def _wprep(w_in):
    k = w_in.shape[0]
    return pl.pallas_call(
        _wprep_kernel,
        grid=(k // WPREP_ROWS,),
        in_specs=[pl.BlockSpec((IN_COLS, WPREP_ROWS), lambda i: (0, i))],
        out_specs=[
            pl.BlockSpec((WPREP_ROWS, W_COLS), lambda i: (i, 0)),
            pl.BlockSpec((ATT_WIDTH, WPREP_ROWS), lambda i: (0, i)),
            pl.BlockSpec((WPREP_ROWS, GK_PAD), lambda i: (i, 0)),
        ],
        out_shape=[
            jax.ShapeDtypeStruct((k, W_COLS), jnp.bfloat16),
            jax.ShapeDtypeStruct((ATT_WIDTH, k), jnp.bfloat16),
            jax.ShapeDtypeStruct((k, GK_PAD), jnp.bfloat16),
        ],
        compiler_params=pltpu.CompilerParams(
            dimension_semantics=("arbitrary",), vmem_limit_bytes=VMEM_LIMIT),
        name="wprep",
    )(w_in.T)


def _chunk_cumsum(x):
    n_chunks = x.shape[0] // CHUNK
    hi = x.astype(jnp.bfloat16)
    lo = (x - hi.astype(jnp.float32)).astype(jnp.bfloat16)
    r = lax.broadcasted_iota(jnp.int32, (CHUNK, 2 * CHUNK), 0)
    c = lax.broadcasted_iota(jnp.int32, (CHUNK, 2 * CHUNK), 1)
    tri = (r >= jnp.bitwise_and(c, CHUNK - 1)).astype(jnp.bfloat16)
    out = []
    for n in range(n_chunks):
        rows = slice(n * CHUNK, (n + 1) * CHUNK)
        terms = jnp.concatenate([hi[rows], lo[rows]], axis=0)
        out.append(jnp.dot(tri, terms, preferred_element_type=jnp.float32))
    return jnp.stack(out, axis=0)


def _front_kernel(x_ref, g_ref, w_ref, wvt_ref, wgk_ref, up_ref, bias_ref,
                  proj_ref, vt_ref, z_ref, state_ref, *, steps_per_seq):
    n_chunks = FRONT_ROWS // CHUNK

    @pl.when(pl.program_id(0) % steps_per_seq == 0)
    def _():
        state_ref[...] = jnp.zeros_like(state_ref)

    x = x_ref[...]
    ms = jnp.mean(x * x, axis=-1, keepdims=True)
    h = (x * lax.rsqrt(ms + RMS_EPS) * g_ref[...]).astype(jnp.bfloat16)

    def project(off, width):
        return jnp.dot(h, w_ref[:, off:off + width],
                       preferred_element_type=jnp.float32).astype(jnp.bfloat16)

    def emit_proj(off):
        def run():
            proj_ref[:, off:off + D_MODEL] = project(off, D_MODEL)
        return run

    def emit_vt():
        vt_ref[...] = lax.dot_general(
            wvt_ref[...], h, _NT, preferred_element_type=jnp.float32
        ).astype(jnp.bfloat16)

    filler = [emit_proj(off) for off in range(0, PROJ_COLS, D_MODEL)] + [emit_vt]

    code = jnp.dot(h, wgk_ref[...], preferred_element_type=jnp.float32).astype(jnp.bfloat16)
    v = project(VG_OFF, GLA_V_WIDTH)
    qk = project(QG_OFF, 2 * GLA_K_WIDTH)
    q, k = qk[:, :GLA_K_WIDTH], qk[:, GLA_K_WIDTH:]
    logits = jnp.dot(code, up_ref[...],
                     preferred_element_type=jnp.float32) + bias_ref[...]
    gate_all = project(GG_OFF, GLA_V_WIDTH)
    gk = (jnp.minimum(logits, 0.0) * (1.0 / GATE_NORMALIZER)
          - jnp.log2(1.0 + jnp.exp2(jnp.abs(logits) * (-LOG2_E)))
          * (LN_2 / GATE_NORMALIZER))
    a_cum3 = _chunk_cumsum(gk)
    filler.pop(0)()
    a_end = a_cum3[:, CHUNK - 1, :]
    k_dec = (k.astype(jnp.float32).reshape(n_chunks, CHUNK, GLA_K_WIDTH)
             * jnp.exp(a_end[:, None, :] - a_cum3)).astype(jnp.bfloat16)
    decay_t = jnp.exp(a_end).T
    kcols = [slice(hd * GLA_DK, (hd + 1) * GLA_DK) for hd in range(GLA_HEADS)]
    vcols = [slice(hd * GLA_DV, (hd + 1) * GLA_DV) for hd in range(GLA_HEADS)]

    def chunk_rows(c):
        return slice(c * CHUNK, (c + 1) * CHUNK)

    def increments(c):
        return [lax.dot_general(
            k_dec[c][:, kcols[hd]], v[chunk_rows(c), vcols[hd]],
            (((0,), (0,)), ((), ())), preferred_element_type=jnp.float32)
            for hd in range(GLA_HEADS)]

    def finish(c, reads):
        for hd in range(GLA_HEADS):
            o = reads[hd]
            ms_o = jnp.mean(o * o, axis=-1, keepdims=True)
            factor = lax.rsqrt(ms_o + RMS_EPS * GLA_DK)
            gate = gate_all[chunk_rows(c), vcols[hd]].astype(jnp.float32)
            z_ref[chunk_rows(c), vcols[hd]] = (
                (o * factor) * (gate * _sigmoid(gate))).astype(jnp.bfloat16)

    upd_next = increments(0)
    reads_prev = None
    for c in range(n_chunks):
        upd = upd_next
        if c + 1 < n_chunks:
            upd_next = increments(c + 1)
        if c % 2 == 0 and filler:
            filler.pop(0)()
        reads = []
        for hd in range(GLA_HEADS):
            state = decay_t[kcols[hd], c:c + 1] * state_ref[hd] + upd[hd]
            state_ref[hd] = state
            reads.append(jnp.dot(q[chunk_rows(c), kcols[hd]],
                                 state.astype(jnp.bfloat16),
                                 preferred_element_type=jnp.float32))
        if reads_prev is not None:
            finish(c - 1, reads_prev)
        reads_prev = reads
    finish(n_chunks - 1, reads_prev)
    while filler:
        filler.pop(0)()


def _front(x2, g_pre, w_main, w_vt, w_gk, up_pad, gk_bias, seq):
    n = x2.shape[0]
    const = lambda i: (0, 0)
    resident = pl.Buffered(1)
    return pl.pallas_call(
        functools.partial(_front_kernel, steps_per_seq=seq // FRONT_ROWS),
        grid=(n // FRONT_ROWS,),
        in_specs=[
            pl.BlockSpec((FRONT_ROWS, D_MODEL), lambda i: (i, 0)),
            pl.BlockSpec((1, D_MODEL), const),
            pl.BlockSpec((D_MODEL, W_COLS), const, pipeline_mode=resident),
            pl.BlockSpec((ATT_WIDTH, D_MODEL), const, pipeline_mode=resident),
            pl.BlockSpec((D_MODEL, GK_PAD), const, pipeline_mode=resident),
            pl.BlockSpec((GK_PAD, GLA_K_WIDTH), const, pipeline_mode=resident),
            pl.BlockSpec((1, GLA_K_WIDTH), const),
        ],
        out_specs=[
            pl.BlockSpec((FRONT_ROWS, PROJ_COLS), lambda i: (i, 0)),
            pl.BlockSpec((ATT_WIDTH, FRONT_ROWS), lambda i: (0, i)),
            pl.BlockSpec((FRONT_ROWS, GLA_V_WIDTH), lambda i: (i, 0)),
        ],
        out_shape=[
            jax.ShapeDtypeStruct((n, PROJ_COLS), jnp.bfloat16),
            jax.ShapeDtypeStruct((ATT_WIDTH, n), jnp.bfloat16),
            jax.ShapeDtypeStruct((n, GLA_V_WIDTH), jnp.bfloat16),
        ],
        scratch_shapes=[pltpu.VMEM((GLA_HEADS, GLA_DK, GLA_DV), jnp.float32)],
        compiler_params=pltpu.CompilerParams(
            dimension_semantics=("arbitrary",),
            vmem_limit_bytes=VMEM_LIMIT,
        ),
        name="front",
    )(x2, g_pre, w_main, w_vt, w_gk, up_pad, gk_bias)


def _key_window(g):
    lead = LEFT_CHUNKS // GROUP
    k_start = max(g - lead, 0) * GROUP_ROWS
    return k_start, (g + 1) * GROUP_ROWS - k_start


def _attn_scores(q_ref, k_ref, tbl_ref, base, g):
    k_start, n_keys = _key_window(g)
    q2 = q_ref[base + g * GROUP_ROWS:base + (g + 1) * GROUP_ROWS, :]
    lane = lax.broadcasted_iota(jnp.int32, q2.shape, 1)
    first = lane < ATT_DH
    zero = jnp.zeros_like(q2)
    qs = jnp.concatenate(
        [jnp.where(first, q2, zero), jnp.where(first, zero, q2)], axis=0)
    kw = k_ref[base + k_start:base + k_start + n_keys, :]
    s = lax.dot_general(kw, qs, _NT, preferred_element_type=jnp.float32)
    s = s + tbl_ref[0, WIN_ROWS - n_keys:, :]
    return s, jnp.max(s, axis=0, keepdims=True)


def _attn_values(vt_ref, p, base, g):
    k_start, n_keys = _key_window(g)
    vt_aug = jnp.concatenate(
        [vt_ref[:, base + k_start:base + k_start + n_keys],
         jnp.ones((BF16_SUBLANES, n_keys), jnp.bfloat16)], axis=0)
    ot = jnp.dot(vt_aug, p, preferred_element_type=jnp.float32)
    r = 1.0 / ot[LANES:LANES + 1, :]
    return jnp.concatenate(
        [ot[:ATT_DH, :GROUP_ROWS] * r[:, :GROUP_ROWS],
         ot[ATT_DH:LANES, GROUP_ROWS:] * r[:, GROUP_ROWS:]], axis=0)


def _attn_kernel(q_ref, k_ref, vt_ref, g_ref, tbl_ref, z_ref, *, seq):
    stage_rows = STAGE_GROUPS * GROUP_ROWS
    stages = [(b * seq, t) for b in range(ATTN_SEQS) for t in range(seq // stage_rows)]

    def scores(base, t):
        return [_attn_scores(q_ref, k_ref, tbl_ref, base, STAGE_GROUPS * t + j)
                for j in range(STAGE_GROUPS)]

    cur = scores(*stages[0])
    for n, (base, t) in enumerate(stages):
        if n + 1 < len(stages):
            nxt = scores(*stages[n + 1])
        probs = [jnp.exp2(s - m).astype(jnp.bfloat16) for s, m in cur]
        ot = jnp.concatenate(
            [_attn_values(vt_ref, p, base, STAGE_GROUPS * t + j) for j, p in enumerate(probs)],
            axis=1)
        rows = slice(base + t * stage_rows, base + (t + 1) * stage_rows)
        gate = g_ref[rows, :].astype(jnp.float32)
        z_ref[rows, :] = (ot.T * gate * _sigmoid(gate)).astype(jnp.bfloat16)
        if n + 1 < len(stages):
            cur = nxt


def _attn(proj, v_t, table, batch, seq):
    n = proj.shape[0]
    pairs = ATT_HEADS // HEADS_PER_STEP
    rows = ATTN_SEQS * seq
    col_map = lambda off: (lambda p, b: (b, off // LANES + p))
    return pl.pallas_call(
        functools.partial(_attn_kernel, seq=seq),
        grid=(pairs, batch // ATTN_SEQS),
        in_specs=[
            pl.BlockSpec((rows, LANES), col_map(QA_OFF)),
            pl.BlockSpec((rows, LANES), col_map(KA_OFF)),
            pl.BlockSpec((LANES, rows), lambda p, b: (p, b)),
            pl.BlockSpec((rows, LANES), col_map(GA_OFF)),
            pl.BlockSpec((1, WIN_ROWS, PAIR_COLS), lambda p, b: (p, 0, 0)),
        ],
        out_specs=pl.BlockSpec((rows, LANES), lambda p, b: (b, p)),
        out_shape=jax.ShapeDtypeStruct((n, ATT_WIDTH), jnp.bfloat16),
        compiler_params=pltpu.CompilerParams(
            dimension_semantics=("arbitrary", "arbitrary"),
            vmem_limit_bytes=VMEM_LIMIT,
        ),
        name="attn",
    )(proj, proj, v_t, proj, table)


def _bias_table(rel_bias):
    heads = rel_bias.shape[0]
    pairs = heads // HEADS_PER_STEP
    t = np.arange(TABLE_PERIOD)
    dist = np.where(t < GROUP_ROWS, t, t - TABLE_PERIOD) + (WIN_ROWS - GROUP_ROWS)
    rel_idx = np.clip(dist, -MAX_REL, MAX_REL) + MAX_REL
    w = rel_bias[:, rel_idx].astype(jnp.float32) * LOG2_E
    return pl.pallas_call(
        _bias_table_kernel,
        grid=(pairs,),
        in_specs=[pl.BlockSpec((HEADS_PER_STEP, 1, TABLE_PERIOD), lambda p: (p, 0, 0))],
        out_specs=pl.BlockSpec((1, WIN_ROWS, PAIR_COLS), lambda p: (p, 0, 0)),
        out_shape=jax.ShapeDtypeStruct((pairs, WIN_ROWS, PAIR_COLS), jnp.float32),
        compiler_params=pltpu.CompilerParams(
            dimension_semantics=("arbitrary",), vmem_limit_bytes=VMEM_LIMIT),
        name="bias_table",
    )(w.reshape(heads, 1, TABLE_PERIOD))


def _bias_table_kernel(w_ref, o_ref):
    j = lax.broadcasted_iota(jnp.int32, (WIN_ROWS, GROUP_ROWS), 0)
    i = lax.broadcasted_iota(jnp.int32, (WIN_ROWS, GROUP_ROWS), 1)
    dc = j // CHUNK - i // CHUNK
    valid = (dc >= 0) & (dc <= LEFT_CHUNKS)
    for hh in range(HEADS_PER_STEP):
        rows = jnp.broadcast_to(w_ref[hh], (WIN_ROWS, TABLE_PERIOD))
        toep = pltpu.roll(rows, 0, axis=1, stride=1, stride_axis=0)[:, :GROUP_ROWS]
        o_ref[0, :, hh * GROUP_ROWS:(hh + 1) * GROUP_ROWS] = jnp.where(valid, toep, NEG_INF)


def _merge_kernel(zg_ref, za_ref, lg_ref, la_ref, x_ref, wog_ref, woa_ref,
                  wout_ref, mb_ref, gp_ref, o_ref):
    yg = jnp.dot(zg_ref[...], wog_ref[...], preferred_element_type=jnp.float32)
    ya = jnp.dot(za_ref[...], woa_ref[...], preferred_element_type=jnp.float32)
    gate_g = _sigmoid(lg_ref[...].astype(jnp.float32) + mb_ref[0:1, :])
    gate_a = _sigmoid(la_ref[...].astype(jnp.float32) + mb_ref[1:2, :])
    merged = (gate_g * yg + gate_a * ya).astype(jnp.bfloat16)
    y = jnp.dot(merged, wout_ref[...], preferred_element_type=jnp.float32)
    ms = jnp.mean(y * y, axis=-1, keepdims=True)
    o_ref[...] = x_ref[...] + y * lax.rsqrt(ms + RMS_EPS) * gp_ref[...]


def _merge(z_gla, z_att, proj, x2, w_o_gla, w_o_att, w_out, merge_bias2, g_post):
    n = x2.shape[0]
    row = lambda i: (i, 0)
    const = lambda i: (0, 0)
    return pl.pallas_call(
        _merge_kernel,
        grid=(n // MERGE_TM,),
        in_specs=[
            pl.BlockSpec((MERGE_TM, D_MODEL), row),
            pl.BlockSpec((MERGE_TM, D_MODEL), row),
            pl.BlockSpec((MERGE_TM, D_MODEL), lambda i: (i, GATE_OFF // D_MODEL)),
            pl.BlockSpec((MERGE_TM, D_MODEL), lambda i: (i, GATE_OFF // D_MODEL + 1)),
            pl.BlockSpec((MERGE_TM, D_MODEL), row),
            pl.BlockSpec((D_MODEL, D_MODEL), const),
            pl.BlockSpec((D_MODEL, D_MODEL), const),
            pl.BlockSpec((D_MODEL, D_MODEL), const),
            pl.BlockSpec((2, D_MODEL), const),
            pl.BlockSpec((1, D_MODEL), const),
        ],
        out_specs=pl.BlockSpec((MERGE_TM, D_MODEL), row),
        out_shape=jax.ShapeDtypeStruct((n, D_MODEL), jnp.float32),
        compiler_params=pltpu.CompilerParams(
            dimension_semantics=("arbitrary",),
            vmem_limit_bytes=VMEM_LIMIT,
        ),
        name="merge",
    )(z_gla, z_att, proj, proj, x2, w_o_gla, w_o_att, w_out, merge_bias2, g_post)


def kernel(x, norm_pre_g, w_in, gk_up, gk_bias, gla_norm_g, rel_bias, w_o_gla,
           w_o_att, merge_bias, w_out, norm_post_g):
    batch, seq, d = x.shape
    assert d == D_MODEL and seq % FRONT_ROWS == 0 and seq % GROUP_ROWS == 0
    assert (batch * seq) % MERGE_TM == 0 and batch % ATTN_SEQS == 0
    assert w_in.shape == (D_MODEL, IN_COLS)
    assert rel_bias.shape == (ATT_HEADS, 2 * MAX_REL + 1)
    bf16 = jnp.bfloat16

    w_main, w_vt, w_gk = _wprep(w_in)
    up_pad = jnp.pad(gk_up, ((0, GK_PAD - GK_RANK), (0, 0))).astype(bf16)

    x2 = x.reshape(batch * seq, D_MODEL)
    proj, v_t, z_gla = _front(x2, norm_pre_g.reshape(1, D_MODEL), w_main, w_vt, w_gk,
                              up_pad, gk_bias.reshape(1, GLA_K_WIDTH), seq)
    z_att = _attn(proj, v_t, _bias_table(rel_bias), batch, seq)
    w_og = (jnp.tile(gla_norm_g, GLA_HEADS)[:, None] * w_o_gla).astype(bf16)
    out = _merge(z_gla, z_att, proj, x2, w_og, w_o_att.astype(bf16),
                 w_out.astype(bf16), merge_bias.reshape(2, D_MODEL),
                 norm_post_g.reshape(1, D_MODEL))
    return out.reshape(batch, seq, D_MODEL)
```

```python
import functools

import numpy as np
import jax
import jax.numpy as jnp
from jax import lax
from jax.experimental import pallas as pl
from jax.experimental.pallas import tpu as pltpu

D_MODEL = 1024
CHUNK = 64
GLA_HEADS = 4
GLA_DK = 128
GLA_DV = 256
GLA_K_WIDTH = GLA_HEADS * GLA_DK
GLA_V_WIDTH = GLA_HEADS * GLA_DV
GK_RANK = 16
GATE_NORMALIZER = 16.0
ATT_HEADS = 16
ATT_DH = 64
ATT_WIDTH = ATT_HEADS * ATT_DH
LEFT_CHUNKS = 8
MAX_REL = 256
RMS_EPS = 1e-6
NEG_INF = -1e30

LANES = 128
BF16_SUBLANES = 16
LOG2_E = float(np.log2(np.e))
LN_2 = float(np.log(2.0))

GATE_OFF = 0
QA_OFF = 2048
KA_OFF = 3072
GA_OFF = 4096
PROJ_COLS = 5120
VG_OFF = 5120
GG_OFF = 6144
QG_OFF = 7168
KG_OFF = 7680
W_COLS = 8192
GK_PAD = LANES

_SRC_QG, _SRC_KG, _SRC_VG, _SRC_GG, _SRC_CODE, _SRC_QA, _SRC_KA, _SRC_VA, _SRC_GA, _SRC_GATE = (
    int(v) for v in np.cumsum([0, GLA_K_WIDTH, GLA_K_WIDTH, GLA_V_WIDTH, GLA_V_WIDTH, GK_RANK,
                               ATT_WIDTH, ATT_WIDTH, ATT_WIDTH, ATT_WIDTH]))
IN_COLS = _SRC_GATE + 2 * D_MODEL
REGROUP = (
    (QA_OFF, _SRC_QA, ATT_WIDTH), (KA_OFF, _SRC_KA, ATT_WIDTH), (GA_OFF, _SRC_GA, ATT_WIDTH),
    (GATE_OFF, _SRC_GATE, 2 * D_MODEL), (VG_OFF, _SRC_VG, GLA_V_WIDTH),
    (GG_OFF, _SRC_GG, GLA_V_WIDTH), (QG_OFF, _SRC_QG, GLA_K_WIDTH), (KG_OFF, _SRC_KG, GLA_K_WIDTH),
)
WPREP_ROWS = 128

GROUP = 2
STAGE_GROUPS = 2
GROUP_ROWS = GROUP * CHUNK
WIN_CHUNKS = GROUP + LEFT_CHUNKS
WIN_ROWS = WIN_CHUNKS * CHUNK
HEADS_PER_STEP = LANES // ATT_DH
PAIR_COLS = HEADS_PER_STEP * GROUP_ROWS
TABLE_PERIOD = GROUP_ROWS + WIN_ROWS
ZERO_LO = (GROUP - 1) * CHUNK
ZERO_HI = WIN_ROWS - GROUP_ROWS - MAX_REL

FRONT_ROWS = 512
MERGE_TM = 512
VMEM_LIMIT = 56 * 1024 * 1024

_NT = (((1,), (1,)), ((), ()))


def _sigmoid(x):
    return 1.0 / (1.0 + jnp.exp2(x * (-LOG2_E)))


def _wprep_kernel(wt_ref, main_ref, vt_ref, gk_ref):
    for dst, src, width in REGROUP:
        blk = wt_ref[src:src + width, :].T
        if dst == QA_OFF:
            blk = blk * (ATT_DH ** -0.5 * LOG2_E)
        main_ref[:, dst:dst + width] = blk.astype(jnp.bfloat16)
    vt_ref[...] = wt_ref[_SRC_VA:_SRC_VA + ATT_WIDTH, :].astype(jnp.bfloat16)
    code = wt_ref[_SRC_CODE:_SRC_CODE + GK_PAD, :].T
    lane = lax.broadcasted_iota(jnp.int32, code.shape, 1)
    gk_ref[...] = jnp.where(lane < GK_RANK, code, 0.0).astype(jnp.bfloat16)


def _wprep(w_in):
    k = w_in.shape[0]
    return pl.pallas_call(
        _wprep_kernel,
        grid=(k // WPREP_ROWS,),
        in_specs=[pl.BlockSpec((IN_COLS, WPREP_ROWS), lambda i: (0, i))],
        out_specs=[
            pl.BlockSpec((WPREP_ROWS, W_COLS), lambda i: (i, 0)),
            pl.BlockSpec((ATT_WIDTH, WPREP_ROWS), lambda i: (0, i)),
            pl.BlockSpec((WPREP_ROWS, GK_PAD), lambda i: (i, 0)),
        ],
        out_shape=[
            jax.ShapeDtypeStruct((k, W_COLS), jnp.bfloat16),
            jax.ShapeDtypeStruct((ATT_WIDTH, k), jnp.bfloat16),
            jax.ShapeDtypeStruct((k, GK_PAD), jnp.bfloat16),
        ],
        compiler_params=pltpu.CompilerParams(
            dimension_semantics=("arbitrary",), vmem_limit_bytes=VMEM_LIMIT),
        name="wprep",
    )(w_in.T)


def _chunk_cumsum(x):
    n_chunks = x.shape[0] // CHUNK
    hi = x.astype(jnp.bfloat16)
    lo = (x - hi.astype(jnp.float32)).astype(jnp.bfloat16)
    r = lax.broadcasted_iota(jnp.int32, (CHUNK, 2 * CHUNK), 0)
    c = lax.broadcasted_iota(jnp.int32, (CHUNK, 2 * CHUNK), 1)
    tri = (r >= jnp.bitwise_and(c, CHUNK - 1)).astype(jnp.bfloat16)
    out = []
    for n in range(n_chunks):
        rows = slice(n * CHUNK, (n + 1) * CHUNK)
        terms = jnp.concatenate([hi[rows], lo[rows]], axis=0)
        out.append(jnp.dot(tri, terms, preferred_element_type=jnp.float32))
    return jnp.stack(out, axis=0)


def _front_kernel(x_ref, g_ref, w_ref, wvt_ref, wgk_ref, up_ref, bias_ref,
                  proj_ref, vt_ref, z_ref, state_ref, *, steps_per_seq):
    n_chunks = FRONT_ROWS // CHUNK

    @pl.when(pl.program_id(0) % steps_per_seq == 0)
    def _():
        state_ref[...] = jnp.zeros_like(state_ref)

    x = x_ref[...]
    ms = jnp.mean(x * x, axis=-1, keepdims=True)
    h = (x * lax.rsqrt(ms + RMS_EPS) * g_ref[...]).astype(jnp.bfloat16)

    def project(off, width):
        return jnp.dot(h, w_ref[:, off:off + width],
                       preferred_element_type=jnp.float32).astype(jnp.bfloat16)

    def emit_proj(off):
        def run():
            proj_ref[:, off:off + D_MODEL] = project(off, D_MODEL)
        return run

    def emit_vt():
        vt_ref[...] = lax.dot_general(
            wvt_ref[...], h, _NT, preferred_element_type=jnp.float32
        ).astype(jnp.bfloat16)

    filler = [emit_proj(off) for off in range(0, PROJ_COLS, D_MODEL)] + [emit_vt]

    code = jnp.dot(h, wgk_ref[...], preferred_element_type=jnp.float32).astype(jnp.bfloat16)
    v = project(VG_OFF, GLA_V_WIDTH)
    qk = project(QG_OFF, 2 * GLA_K_WIDTH)
    q, k = qk[:, :GLA_K_WIDTH], qk[:, GLA_K_WIDTH:]
    logits = jnp.dot(code, up_ref[...],
                     preferred_element_type=jnp.float32) + bias_ref[...]
    gate_all = project(GG_OFF, GLA_V_WIDTH)
    gk = (jnp.minimum(logits, 0.0) * (1.0 / GATE_NORMALIZER)
          - jnp.log2(1.0 + jnp.exp2(jnp.abs(logits) * (-LOG2_E)))
          * (LN_2 / GATE_NORMALIZER))
    a_cum3 = _chunk_cumsum(gk)
    filler.pop(0)()
    a_end = a_cum3[:, CHUNK - 1, :]
    k_dec = (k.astype(jnp.float32).reshape(n_chunks, CHUNK, GLA_K_WIDTH)
             * jnp.exp(a_end[:, None, :] - a_cum3)).astype(jnp.bfloat16)
    decay_t = jnp.exp(a_end).T
    kcols = [slice(hd * GLA_DK, (hd + 1) * GLA_DK) for hd in range(GLA_HEADS)]
    vcols = [slice(hd * GLA_DV, (hd + 1) * GLA_DV) for hd in range(GLA_HEADS)]

    def chunk_rows(c):
        return slice(c * CHUNK, (c + 1) * CHUNK)

    def increments(c):
        return [lax.dot_general(
            k_dec[c][:, kcols[hd]], v[chunk_rows(c), vcols[hd]],
            (((0,), (0,)), ((), ())), preferred_element_type=jnp.float32)
            for hd in range(GLA_HEADS)]

    def finish(c, reads):
        for hd in range(GLA_HEADS):
            o = reads[hd]
            ms_o = jnp.mean(o * o, axis=-1, keepdims=True)
            factor = lax.rsqrt(ms_o + RMS_EPS * GLA_DK)
            gate = gate_all[chunk_rows(c), vcols[hd]].astype(jnp.float32)
            z_ref[chunk_rows(c), vcols[hd]] = (
                (o * factor) * (gate * _sigmoid(gate))).astype(jnp.bfloat16)

    upd_next = increments(0)
    reads_prev = None
    for c in range(n_chunks):
        upd = upd_next
        if c + 1 < n_chunks:
            upd_next = increments(c + 1)
        if c % 2 == 0 and filler:
            filler.pop(0)()
        reads = []
        for hd in range(GLA_HEADS):
            state = decay_t[kcols[hd], c:c + 1] * state_ref[hd] + upd[hd]
            state_ref[hd] = state
            reads.append(jnp.dot(q[chunk_rows(c), kcols[hd]],
                                 state.astype(jnp.bfloat16),
                                 preferred_element_type=jnp.float32))
        if reads_prev is not None:
            finish(c - 1, reads_prev)
        reads_prev = reads
    finish(n_chunks - 1, reads_prev)
    while filler:
        filler.pop(0)()


def _front(x2, g_pre, w_main, w_vt, w_gk, up_pad, gk_bias, seq):
    n = x2.shape[0]
    const = lambda i: (0, 0)
    resident = pl.Buffered(1)
    return pl.pallas_call(
        functools.partial(_front_kernel, steps_per_seq=seq // FRONT_ROWS),
        grid=(n // FRONT_ROWS,),
        in_specs=[
            pl.BlockSpec((FRONT_ROWS, D_MODEL), lambda i: (i, 0)),
            pl.BlockSpec((1, D_MODEL), const),
            pl.BlockSpec((D_MODEL, W_COLS), const, pipeline_mode=resident),
            pl.BlockSpec((ATT_WIDTH, D_MODEL), const, pipeline_mode=resident),
            pl.BlockSpec((D_MODEL, GK_PAD), const, pipeline_mode=resident),
            pl.BlockSpec((GK_PAD, GLA_K_WIDTH), const, pipeline_mode=resident),
            pl.BlockSpec((1, GLA_K_WIDTH), const),
        ],
        out_specs=[
            pl.BlockSpec((FRONT_ROWS, PROJ_COLS), lambda i: (i, 0)),
            pl.BlockSpec((ATT_WIDTH, FRONT_ROWS), lambda i: (0, i)),
            pl.BlockSpec((FRONT_ROWS, GLA_V_WIDTH), lambda i: (i, 0)),
        ],
        out_shape=[
            jax.ShapeDtypeStruct((n, PROJ_COLS), jnp.bfloat16),
            jax.ShapeDtypeStruct((ATT_WIDTH, n), jnp.bfloat16),
            jax.ShapeDtypeStruct((n, GLA_V_WIDTH), jnp.bfloat16),
        ],
        scratch_shapes=[pltpu.VMEM((GLA_HEADS, GLA_DK, GLA_DV), jnp.float32)],
        compiler_params=pltpu.CompilerParams(
            dimension_semantics=("arbitrary",),
            vmem_limit_bytes=VMEM_LIMIT,
        ),
        name="front",
    )(x2, g_pre, w_main, w_vt, w_gk, up_pad, gk_bias)


def _key_window(g):
    lead = LEFT_CHUNKS // GROUP
    k_start = max(g - lead, 0) * GROUP_ROWS
    return k_start, (g + 1) * GROUP_ROWS - k_start


def _attn_scores(q_ref, k_ref, tbl_ref, g):
    k_start, n_keys = _key_window(g)
    q2 = q_ref[g * GROUP_ROWS:(g + 1) * GROUP_ROWS, :]
    lane = lax.broadcasted_iota(jnp.int32, q2.shape, 1)
    first = lane < ATT_DH
    zero = jnp.zeros_like(q2)
    qs = jnp.concatenate(
        [jnp.where(first, q2, zero), jnp.where(first, zero, q2)], axis=0)
    kw = k_ref[k_start:k_start + n_keys, :]
    s = lax.dot_general(kw, qs, _NT, preferred_element_type=jnp.float32)
    lo = WIN_ROWS - n_keys
    z0, z1 = max(ZERO_LO, lo) - lo, max(ZERO_HI, lo) - lo
    parts = []
    if z0 > 0:
        parts.append(s[:z0] + tbl_ref[0, lo:lo + z0, :])
    if z1 > z0:
        parts.append(s[z0:z1])
    parts.append(s[z1:] + tbl_ref[0, lo + z1:, :])
    s = jnp.concatenate(parts, axis=0)
    return s, jnp.max(s, axis=0, keepdims=True)


def _attn_values(vt_ref, p, g):
    k_start, n_keys = _key_window(g)
    vt_aug = jnp.concatenate(
        [vt_ref[:, k_start:k_start + n_keys],
         jnp.ones((BF16_SUBLANES, n_keys), jnp.bfloat16)], axis=0)
    ot = jnp.dot(vt_aug, p, preferred_element_type=jnp.float32)
    r = 1.0 / ot[LANES:LANES + 1, :]
    return jnp.concatenate(
        [ot[:ATT_DH, :GROUP_ROWS] * r[:, :GROUP_ROWS],
         ot[ATT_DH:LANES, GROUP_ROWS:] * r[:, GROUP_ROWS:]], axis=0)


def _attn_kernel(q_ref, k_ref, vt_ref, g_ref, tbl_ref, z_ref, *, seq):
    n_stages = seq // (STAGE_GROUPS * GROUP_ROWS)

    def scores(t):
        return [_attn_scores(q_ref, k_ref, tbl_ref, STAGE_GROUPS * t + j)
                for j in range(STAGE_GROUPS)]

    cur = scores(0)
    for t in range(n_stages):
        if t + 1 < n_stages:
            nxt = scores(t + 1)
        probs = [jnp.exp2(s - m).astype(jnp.bfloat16) for s, m in cur]
        ot = jnp.concatenate(
            [_attn_values(vt_ref, p, STAGE_GROUPS * t + j) for j, p in enumerate(probs)],
            axis=1)
        rows = slice(t * STAGE_GROUPS * GROUP_ROWS, (t + 1) * STAGE_GROUPS * GROUP_ROWS)
        gate = g_ref[rows, :].astype(jnp.float32)
        z_ref[rows, :] = (ot.T * gate * _sigmoid(gate)).astype(jnp.bfloat16)
        if t + 1 < n_stages:
            cur = nxt


def _attn(proj, v_t, table, batch, seq):
    n = proj.shape[0]
    pairs = ATT_HEADS // HEADS_PER_STEP
    col_map = lambda off: (lambda p, b: (b, off // LANES + p))
    return pl.pallas_call(
        functools.partial(_attn_kernel, seq=seq),
        grid=(pairs, batch),
        in_specs=[
            pl.BlockSpec((seq, LANES), col_map(QA_OFF)),
            pl.BlockSpec((seq, LANES), col_map(KA_OFF)),
            pl.BlockSpec((LANES, seq), lambda p, b: (p, b)),
            pl.BlockSpec((seq, LANES), col_map(GA_OFF)),
            pl.BlockSpec((1, WIN_ROWS, PAIR_COLS), lambda p, b: (p, 0, 0)),
        ],
        out_specs=pl.BlockSpec((seq, LANES), lambda p, b: (b, p)),
        out_shape=jax.ShapeDtypeStruct((n, ATT_WIDTH), jnp.bfloat16),
        compiler_params=pltpu.CompilerParams(
            dimension_semantics=("arbitrary", "arbitrary"),
            vmem_limit_bytes=VMEM_LIMIT,
        ),
        name="attn",
    )(proj, proj, v_t, proj, table)


def _bias_table(rel_bias):
    heads = rel_bias.shape[0]
    pairs = heads // HEADS_PER_STEP
    t = np.arange(TABLE_PERIOD)
    dist = np.where(t < GROUP_ROWS, t, t - TABLE_PERIOD) + (WIN_ROWS - GROUP_ROWS)
    rel_idx = np.clip(dist, -MAX_REL, MAX_REL) + MAX_REL
    w = (rel_bias[:, rel_idx] - rel_bias[:, 2 * MAX_REL:]).astype(jnp.float32) * LOG2_E
    return pl.pallas_call(
        _bias_table_kernel,
        grid=(pairs,),
        in_specs=[pl.BlockSpec((HEADS_PER_STEP, 1, TABLE_PERIOD), lambda p: (p, 0, 0))],
        out_specs=pl.BlockSpec((1, WIN_ROWS, PAIR_COLS), lambda p: (p, 0, 0)),
        out_shape=jax.ShapeDtypeStruct((pairs, WIN_ROWS, PAIR_COLS), jnp.float32),
        compiler_params=pltpu.CompilerParams(
            dimension_semantics=("arbitrary",), vmem_limit_bytes=VMEM_LIMIT),
        name="bias_table",
    )(w.reshape(heads, 1, TABLE_PERIOD))


def _bias_table_kernel(w_ref, o_ref):
    j = lax.broadcasted_iota(jnp.int32, (WIN_ROWS, GROUP_ROWS), 0)
    i = lax.broadcasted_iota(jnp.int32, (WIN_ROWS, GROUP_ROWS), 1)
    dc = j // CHUNK - i // CHUNK
    valid = (dc >= 0) & (dc <= LEFT_CHUNKS)
    for hh in range(HEADS_PER_STEP):
        rows = jnp.broadcast_to(w_ref[hh], (WIN_ROWS, TABLE_PERIOD))
        toep = pltpu.roll(rows, 0, axis=1, stride=1, stride_axis=0)[:, :GROUP_ROWS]
        o_ref[0, :, hh * GROUP_ROWS:(hh + 1) * GROUP_ROWS] = jnp.where(valid, toep, NEG_INF)


def _merge_kernel(zg_ref, za_ref, l_ref, x_ref, wog_ref, woa_ref,
                  wout_ref, mb_ref, gp_ref, o_ref):
    yg = jnp.dot(zg_ref[...], wog_ref[...], preferred_element_type=jnp.float32)
    ya = jnp.dot(za_ref[...], woa_ref[...], preferred_element_type=jnp.float32)
    gate_g = _sigmoid(l_ref[:, :D_MODEL].astype(jnp.float32) + mb_ref[0:1, :])
    gate_a = _sigmoid(l_ref[:, D_MODEL:].astype(jnp.float32) + mb_ref[1:2, :])
    merged = (gate_g * yg + gate_a * ya).astype(jnp.bfloat16)
    y = jnp.dot(merged, wout_ref[...], preferred_element_type=jnp.float32)
    ms = jnp.mean(y * y, axis=-1, keepdims=True)
    o_ref[...] = x_ref[...] + y * lax.rsqrt(ms + RMS_EPS) * gp_ref[...]


def _merge(z_gla, z_att, proj, x2, w_o_gla, w_o_att, w_out, merge_bias2, g_post):
    n = x2.shape[0]
    row = lambda i: (i, 0)
    const = lambda i: (0, 0)
    return pl.pallas_call(
        _merge_kernel,
        grid=(n // MERGE_TM,),
        in_specs=[
            pl.BlockSpec((MERGE_TM, D_MODEL), row),
            pl.BlockSpec((MERGE_TM, D_MODEL), row),
            pl.BlockSpec((MERGE_TM, 2 * D_MODEL), lambda i: (i, GATE_OFF // (2 * D_MODEL))),
            pl.BlockSpec((MERGE_TM, D_MODEL), row),
            pl.BlockSpec((D_MODEL, D_MODEL), const),
            pl.BlockSpec((D_MODEL, D_MODEL), const),
            pl.BlockSpec((D_MODEL, D_MODEL), const),
            pl.BlockSpec((2, D_MODEL), const),
            pl.BlockSpec((1, D_MODEL), const),
        ],
        out_specs=pl.BlockSpec((MERGE_TM, D_MODEL), row),
        out_shape=jax.ShapeDtypeStruct((n, D_MODEL), jnp.float32),
        compiler_params=pltpu.CompilerParams(
            dimension_semantics=("arbitrary",),
            vmem_limit_bytes=VMEM_LIMIT,
        ),
        name="merge",
    )(z_gla, z_att, proj, x2, w_o_gla, w_o_att, w_out, merge_bias2, g_post)


def kernel(x, norm_pre_g, w_in, gk_up, gk_bias, gla_norm_g, rel_bias, w_o_gla,
           w_o_att, merge_bias, w_out, norm_post_g):
    batch, seq, d = x.shape
    assert d == D_MODEL and seq % FRONT_ROWS == 0 and seq % GROUP_ROWS == 0
    assert (batch * seq) % MERGE_TM == 0
    assert w_in.shape == (D_MODEL, IN_COLS)
    assert rel_bias.shape == (ATT_HEADS, 2 * MAX_REL + 1)
    bf16 = jnp.bfloat16

    w_main, w_vt, w_gk = _wprep(w_in)
    up_pad = jnp.pad(gk_up, ((0, GK_PAD - GK_RANK), (0, 0))).astype(bf16)

    x2 = x.reshape(batch * seq, D_MODEL)
    proj, v_t, z_gla = _front(x2, norm_pre_g.reshape(1, D_MODEL), w_main, w_vt, w_gk,
                              up_pad, gk_bias.reshape(1, GLA_K_WIDTH), seq)
    z_att = _attn(proj, v_t, _bias_table(rel_bias), batch, seq)
    w_og = (jnp.tile(gla_norm_g, GLA_HEADS)[:, None] * w_o_gla).astype(bf16)
    out = _merge(z_gla, z_att, proj, x2, w_og, w_o_att.astype(bf16),
                 w_out.astype(bf16), merge_bias.reshape(2, D_MODEL),
                 norm_post_g.reshape(1, D_MODEL))
    return out.reshape(batch, seq, D_MODEL)
```

```python
import functools

import numpy as np
import jax
import jax.numpy as jnp
from jax import lax
from jax.experimental import pallas as pl
from jax.experimental.pallas import tpu as pltpu

D_MODEL = 1024
CHUNK = 64
GLA_HEADS = 4
GLA_DK = 128
GLA_DV = 256
GLA_K_WIDTH = GLA_HEADS * GLA_DK
GLA_V_WIDTH = GLA_HEADS * GLA_DV
GK_RANK = 16
GATE_NORMALIZER = 16.0
ATT_HEADS = 16
ATT_DH = 64
ATT_WIDTH = ATT_HEADS * ATT_DH
LEFT_CHUNKS = 8
MAX_REL = 256
RMS_EPS = 1e-6
NEG_INF = -1e30

LANES = 128
BF16_SUBLANES = 16
LOG2_E = float(np.log2(np.e))
LN_2 = float(np.log(2.0))

GATE_OFF = 0
QA_OFF = 2048
KA_OFF = 3072
GA_OFF = 4096
PROJ_COLS = 5120
VG_OFF = 5120
GG_OFF = 6144
QG_OFF = 7168
KG_OFF = 7680
W_COLS = 8192
GK_PAD = LANES

_SRC_QG, _SRC_KG, _SRC_VG, _SRC_GG, _SRC_CODE, _SRC_QA, _SRC_KA, _SRC_VA, _SRC_GA, _SRC_GATE = (
    int(v) for v in np.cumsum([0, GLA_K_WIDTH, GLA_K_WIDTH, GLA_V_WIDTH, GLA_V_WIDTH, GK_RANK,
                               ATT_WIDTH, ATT_WIDTH, ATT_WIDTH, ATT_WIDTH]))
IN_COLS = _SRC_GATE + 2 * D_MODEL
REGROUP = (
    (QA_OFF, _SRC_QA, ATT_WIDTH), (KA_OFF, _SRC_KA, ATT_WIDTH), (GA_OFF, _SRC_GA, ATT_WIDTH),
    (GATE_OFF, _SRC_GATE, 2 * D_MODEL), (VG_OFF, _SRC_VG, GLA_V_WIDTH),
    (GG_OFF, _SRC_GG, GLA_V_WIDTH), (QG_OFF, _SRC_QG, GLA_K_WIDTH), (KG_OFF, _SRC_KG, GLA_K_WIDTH),
)
WPREP_ROWS = 128

GROUP = 2
STAGE_GROUPS = 2
GROUP_ROWS = GROUP * CHUNK
WIN_CHUNKS = GROUP + LEFT_CHUNKS
WIN_ROWS = WIN_CHUNKS * CHUNK
HEADS_PER_STEP = LANES // ATT_DH
PAIR_COLS = HEADS_PER_STEP * GROUP_ROWS
TABLE_PERIOD = GROUP_ROWS + WIN_ROWS
ZERO_LO = (GROUP - 1) * CHUNK
ZERO_HI = WIN_ROWS - GROUP_ROWS - MAX_REL

FRONT_ROWS = 512
FILL_COLS = 512
MERGE_TM = 512
VMEM_LIMIT = 56 * 1024 * 1024

_NT = (((1,), (1,)), ((), ()))


def _sigmoid(x):
    return 1.0 / (1.0 + jnp.exp2(x * (-LOG2_E)))


def _wprep_kernel(wt_ref, main_ref, vt_ref, gk_ref):
    for dst, src, width in REGROUP:
        blk = wt_ref[src:src + width, :].T
        if dst == QA_OFF:
            blk = blk * (ATT_DH ** -0.5 * LOG2_E)
        main_ref[:, dst:dst + width] = blk.astype(jnp.bfloat16)
    vt_ref[...] = wt_ref[_SRC_VA:_SRC_VA + ATT_WIDTH, :].astype(jnp.bfloat16)
    code = wt_ref[_SRC_CODE:_SRC_CODE + GK_PAD, :].T
    lane = lax.broadcasted_iota(jnp.int32, code.shape, 1)
    gk_ref[...] = jnp.where(lane < GK_RANK, code, 0.0).astype(jnp.bfloat16)


def _wprep(w_in):
    k = w_in.shape[0]
    return pl.pallas_call(
        _wprep_kernel,
        grid=(k // WPREP_ROWS,),
        in_specs=[pl.BlockSpec((IN_COLS, WPREP_ROWS), lambda i: (0, i))],
        out_specs=[
            pl.BlockSpec((WPREP_ROWS, W_COLS), lambda i: (i, 0)),
            pl.BlockSpec((ATT_WIDTH, WPREP_ROWS), lambda i: (0, i)),
            pl.BlockSpec((WPREP_ROWS, GK_PAD), lambda i: (i, 0)),
        ],
        out_shape=[
            jax.ShapeDtypeStruct((k, W_COLS), jnp.bfloat16),
            jax.ShapeDtypeStruct((ATT_WIDTH, k), jnp.bfloat16),
            jax.ShapeDtypeStruct((k, GK_PAD), jnp.bfloat16),
        ],
        compiler_params=pltpu.CompilerParams(
            dimension_semantics=("arbitrary",), vmem_limit_bytes=VMEM_LIMIT),
        name="wprep",
    )(w_in.T)


def _chunk_cumsum(x):
    n_chunks = x.shape[0] // CHUNK
    hi = x.astype(jnp.bfloat16)
    lo = (x - hi.astype(jnp.float32)).astype(jnp.bfloat16)
    r = lax.broadcasted_iota(jnp.int32, (CHUNK, 2 * CHUNK), 0)
    c = lax.broadcasted_iota(jnp.int32, (CHUNK, 2 * CHUNK), 1)
    tri = (r >= jnp.bitwise_and(c, CHUNK - 1)).astype(jnp.bfloat16)
    out = []
    for n in range(n_chunks):
        rows = slice(n * CHUNK, (n + 1) * CHUNK)
        terms = jnp.concatenate([hi[rows], lo[rows]], axis=0)
        out.append(jnp.dot(tri, terms, preferred_element_type=jnp.float32))
    return jnp.stack(out, axis=0)


def _front_kernel(x_ref, g_ref, w_ref, wvt_ref, wgk_ref, up_ref, bias_ref,
                  proj_ref, vt_ref, z_ref, state_ref, *, steps_per_seq):
    n_chunks = FRONT_ROWS // CHUNK

    @pl.when(pl.program_id(0) % steps_per_seq == 0)
    def _():
        state_ref[...] = jnp.zeros_like(state_ref)

    x = x_ref[...]
    ms = jnp.mean(x * x, axis=-1, keepdims=True)
    h = (x * lax.rsqrt(ms + RMS_EPS) * g_ref[...]).astype(jnp.bfloat16)

    def project(off, width):
        return jnp.dot(h, w_ref[:, off:off + width],
                       preferred_element_type=jnp.float32).astype(jnp.bfloat16)

    def emit_proj(off):
        def run():
            proj_ref[:, off:off + FILL_COLS] = project(off, FILL_COLS)
        return run

    def emit_vt():
        vt_ref[...] = lax.dot_general(
            wvt_ref[...], h, _NT, preferred_element_type=jnp.float32
        ).astype(jnp.bfloat16)

    filler = [emit_proj(off) for off in range(0, PROJ_COLS, FILL_COLS)] + [emit_vt]

    code = jnp.dot(h, wgk_ref[...], preferred_element_type=jnp.float32).astype(jnp.bfloat16)
    v = project(VG_OFF, GLA_V_WIDTH)
    qk = project(QG_OFF, 2 * GLA_K_WIDTH)
    q, k = qk[:, :GLA_K_WIDTH], qk[:, GLA_K_WIDTH:]
    logits = jnp.dot(code, up_ref[...],
                     preferred_element_type=jnp.float32) + bias_ref[...]
    gate_all = project(GG_OFF, GLA_V_WIDTH)
    gk = (jnp.minimum(logits, 0.0) * (1.0 / GATE_NORMALIZER)
          - jnp.log2(1.0 + jnp.exp2(jnp.abs(logits) * (-LOG2_E)))
          * (LN_2 / GATE_NORMALIZER))
    a_cum3 = _chunk_cumsum(gk)
    filler.pop(0)()
    a_end = a_cum3[:, CHUNK - 1, :]
    k_dec = (k.astype(jnp.float32).reshape(n_chunks, CHUNK, GLA_K_WIDTH)
             * jnp.exp(a_end[:, None, :] - a_cum3)).astype(jnp.bfloat16)
    decay_t = jnp.exp(a_end).T
    kcols = [slice(hd * GLA_DK, (hd + 1) * GLA_DK) for hd in range(GLA_HEADS)]
    vcols = [slice(hd * GLA_DV, (hd + 1) * GLA_DV) for hd in range(GLA_HEADS)]

    def chunk_rows(c):
        return slice(c * CHUNK, (c + 1) * CHUNK)

    def increments(c):
        return [lax.dot_general(
            k_dec[c][:, kcols[hd]], v[chunk_rows(c), vcols[hd]],
            (((0,), (0,)), ((), ())), preferred_element_type=jnp.float32)
            for hd in range(GLA_HEADS)]

    def finish(c, reads):
        for hd in range(GLA_HEADS):
            o = reads[hd]
            ms_o = jnp.mean(o * o, axis=-1, keepdims=True)
            factor = lax.rsqrt(ms_o + RMS_EPS * GLA_DK)
            gate = gate_all[chunk_rows(c), vcols[hd]].astype(jnp.float32)
            z_ref[chunk_rows(c), vcols[hd]] = (
                (o * factor) * (gate * _sigmoid(gate))).astype(jnp.bfloat16)

    upd_next = increments(0)
    reads_prev = None
    for c in range(n_chunks):
        upd = upd_next
        if c + 1 < n_chunks:
            upd_next = increments(c + 1)
        if filler:
            filler.pop(0)()
        reads = []
        for hd in range(GLA_HEADS):
            state = decay_t[kcols[hd], c:c + 1] * state_ref[hd] + upd[hd]
            state_ref[hd] = state
            reads.append(jnp.dot(q[chunk_rows(c), kcols[hd]],
                                 state.astype(jnp.bfloat16),
                                 preferred_element_type=jnp.float32))
        if reads_prev is not None:
            finish(c - 1, reads_prev)
        reads_prev = reads
    finish(n_chunks - 1, reads_prev)
    while filler:
        filler.pop(0)()


def _front(x2, g_pre, w_main, w_vt, w_gk, up_pad, gk_bias, seq):
    n = x2.shape[0]
    const = lambda i: (0, 0)
    resident = pl.Buffered(1)
    return pl.pallas_call(
        functools.partial(_front_kernel, steps_per_seq=seq // FRONT_ROWS),
        grid=(n // FRONT_ROWS,),
        in_specs=[
            pl.BlockSpec((FRONT_ROWS, D_MODEL), lambda i: (i, 0)),
            pl.BlockSpec((1, D_MODEL), const),
            pl.BlockSpec((D_MODEL, W_COLS), const, pipeline_mode=resident),
            pl.BlockSpec((ATT_WIDTH, D_MODEL), const, pipeline_mode=resident),
            pl.BlockSpec((D_MODEL, GK_PAD), const, pipeline_mode=resident),
            pl.BlockSpec((GK_PAD, GLA_K_WIDTH), const, pipeline_mode=resident),
            pl.BlockSpec((1, GLA_K_WIDTH), const),
        ],
        out_specs=[
            pl.BlockSpec((FRONT_ROWS, PROJ_COLS), lambda i: (i, 0)),
            pl.BlockSpec((ATT_WIDTH, FRONT_ROWS), lambda i: (0, i)),
            pl.BlockSpec((FRONT_ROWS, GLA_V_WIDTH), lambda i: (i, 0)),
        ],
        out_shape=[
            jax.ShapeDtypeStruct((n, PROJ_COLS), jnp.bfloat16),
            jax.ShapeDtypeStruct((ATT_WIDTH, n), jnp.bfloat16),
            jax.ShapeDtypeStruct((n, GLA_V_WIDTH), jnp.bfloat16),
        ],
        scratch_shapes=[pltpu.VMEM((GLA_HEADS, GLA_DK, GLA_DV), jnp.float32)],
        compiler_params=pltpu.CompilerParams(
            dimension_semantics=("arbitrary",),
            vmem_limit_bytes=VMEM_LIMIT,
        ),
        name="front",
    )(x2, g_pre, w_main, w_vt, w_gk, up_pad, gk_bias)


def _key_window(g):
    lead = LEFT_CHUNKS // GROUP
    k_start = max(g - lead, 0) * GROUP_ROWS
    return k_start, (g + 1) * GROUP_ROWS - k_start


def _attn_scores(q_ref, k_ref, tbl_ref, g):
    k_start, n_keys = _key_window(g)
    q2 = q_ref[g * GROUP_ROWS:(g + 1) * GROUP_ROWS, :]
    lane = lax.broadcasted_iota(jnp.int32, q2.shape, 1)
    first = lane < ATT_DH
    zero = jnp.zeros_like(q2)
    qs = jnp.concatenate(
        [jnp.where(first, q2, zero), jnp.where(first, zero, q2)], axis=0)
    kw = k_ref[k_start:k_start + n_keys, :]
    s = lax.dot_general(kw, qs, _NT, preferred_element_type=jnp.float32)
    lo = WIN_ROWS - n_keys
    z0, z1 = max(ZERO_LO, lo) - lo, max(ZERO_HI, lo) - lo
    parts = []
    if z0 > 0:
        parts.append(s[:z0] + tbl_ref[0, lo:lo + z0, :])
    if z1 > z0:
        parts.append(s[z0:z1])
    parts.append(s[z1:] + tbl_ref[0, lo + z1:, :])
    s = jnp.concatenate(parts, axis=0)
    return s, jnp.max(s, axis=0, keepdims=True)


def _attn_values(vt_ref, p, g):
    k_start, n_keys = _key_window(g)
    vt_aug = jnp.concatenate(
        [vt_ref[:, k_start:k_start + n_keys],
         jnp.ones((BF16_SUBLANES, n_keys), jnp.bfloat16)], axis=0)
    ot = jnp.dot(vt_aug, p, preferred_element_type=jnp.float32)
    r = 1.0 / ot[LANES:LANES + 1, :]
    return jnp.concatenate(
        [ot[:ATT_DH, :GROUP_ROWS] * r[:, :GROUP_ROWS],
         ot[ATT_DH:LANES, GROUP_ROWS:] * r[:, GROUP_ROWS:]], axis=0)


def _attn_kernel(q_ref, k_ref, vt_ref, g_ref, tbl_ref, z_ref, *, seq):
    n_stages = seq // (STAGE_GROUPS * GROUP_ROWS)

    def scores(t):
        return [_attn_scores(q_ref, k_ref, tbl_ref, STAGE_GROUPS * t + j)
                for j in range(STAGE_GROUPS)]

    cur = scores(0)
    for t in range(n_stages):
        if t + 1 < n_stages:
            nxt = scores(t + 1)
        probs = [jnp.exp2(s - m).astype(jnp.bfloat16) for s, m in cur]
        ot = jnp.concatenate(
            [_attn_values(vt_ref, p, STAGE_GROUPS * t + j) for j, p in enumerate(probs)],
            axis=1)
        rows = slice(t * STAGE_GROUPS * GROUP_ROWS, (t + 1) * STAGE_GROUPS * GROUP_ROWS)
        gate = g_ref[rows, :].astype(jnp.float32)
        z_ref[rows, :] = (ot.T * gate * _sigmoid(gate)).astype(jnp.bfloat16)
        if t + 1 < n_stages:
            cur = nxt


def _attn(proj, v_t, table, batch, seq):
    n = proj.shape[0]
    pairs = ATT_HEADS // HEADS_PER_STEP
    col_map = lambda off: (lambda p, b: (b, off // LANES + p))
    return pl.pallas_call(
        functools.partial(_attn_kernel, seq=seq),
        grid=(pairs, batch),
        in_specs=[
            pl.BlockSpec((seq, LANES), col_map(QA_OFF)),
            pl.BlockSpec((seq, LANES), col_map(KA_OFF)),
            pl.BlockSpec((LANES, seq), lambda p, b: (p, b)),
            pl.BlockSpec((seq, LANES), col_map(GA_OFF)),
            pl.BlockSpec((1, WIN_ROWS, PAIR_COLS), lambda p, b: (p, 0, 0)),
        ],
        out_specs=pl.BlockSpec((seq, LANES), lambda p, b: (b, p)),
        out_shape=jax.ShapeDtypeStruct((n, ATT_WIDTH), jnp.bfloat16),
        compiler_params=pltpu.CompilerParams(
            dimension_semantics=("arbitrary", "arbitrary"),
            vmem_limit_bytes=VMEM_LIMIT,
        ),
        name="attn",
    )(proj, proj, v_t, proj, table)


def _bias_table(rel_bias):
    heads = rel_bias.shape[0]
    pairs = heads // HEADS_PER_STEP
    t = np.arange(TABLE_PERIOD)
    dist = np.where(t < GROUP_ROWS, t, t - TABLE_PERIOD) + (WIN_ROWS - GROUP_ROWS)
    rel_idx = np.clip(dist, -MAX_REL, MAX_REL) + MAX_REL
    w = (rel_bias[:, rel_idx] - rel_bias[:, 2 * MAX_REL:]).astype(jnp.float32) * LOG2_E
    return pl.pallas_call(
        _bias_table_kernel,
        grid=(pairs,),
        in_specs=[pl.BlockSpec((HEADS_PER_STEP, 1, TABLE_PERIOD), lambda p: (p, 0, 0))],
        out_specs=pl.BlockSpec((1, WIN_ROWS, PAIR_COLS), lambda p: (p, 0, 0)),
        out_shape=jax.ShapeDtypeStruct((pairs, WIN_ROWS, PAIR_COLS), jnp.float32),
        compiler_params=pltpu.CompilerParams(
            dimension_semantics=("arbitrary",), vmem_limit_bytes=VMEM_LIMIT),
        name="bias_table",
    )(w.reshape(heads, 1, TABLE_PERIOD))


def _bias_table_kernel(w_ref, o_ref):
    j = lax.broadcasted_iota(jnp.int32, (WIN_ROWS, GROUP_ROWS), 0)
    i = lax.broadcasted_iota(jnp.int32, (WIN_ROWS, GROUP_ROWS), 1)
    dc = j // CHUNK - i // CHUNK
    valid = (dc >= 0) & (dc <= LEFT_CHUNKS)
    for hh in range(HEADS_PER_STEP):
        rows = jnp.broadcast_to(w_ref[hh], (WIN_ROWS, TABLE_PERIOD))
        toep = pltpu.roll(rows, 0, axis=1, stride=1, stride_axis=0)[:, :GROUP_ROWS]
        o_ref[0, :, hh * GROUP_ROWS:(hh + 1) * GROUP_ROWS] = jnp.where(valid, toep, NEG_INF)


def _merge_kernel(zg_ref, za_ref, l_ref, x_ref, wog_ref, woa_ref,
                  wout_ref, mb_ref, gp_ref, o_ref):
    yg = jnp.dot(zg_ref[...], wog_ref[...], preferred_element_type=jnp.float32)
    ya = jnp.dot(za_ref[...], woa_ref[...], preferred_element_type=jnp.float32)
    gate_g = _sigmoid(l_ref[:, :D_MODEL].astype(jnp.float32) + mb_ref[0:1, :])
    gate_a = _sigmoid(l_ref[:, D_MODEL:].astype(jnp.float32) + mb_ref[1:2, :])
    merged = (gate_g * yg + gate_a * ya).astype(jnp.bfloat16)
    y = jnp.dot(merged, wout_ref[...], preferred_element_type=jnp.float32)
    ms = jnp.mean(y * y, axis=-1, keepdims=True)
    o_ref[...] = x_ref[...] + y * lax.rsqrt(ms + RMS_EPS) * gp_ref[...]


def _merge(z_gla, z_att, proj, x2, w_o_gla, w_o_att, w_out, merge_bias2, g_post):
    n = x2.shape[0]
    row = lambda i: (i, 0)
    const = lambda i: (0, 0)
    return pl.pallas_call(
        _merge_kernel,
        grid=(n // MERGE_TM,),
        in_specs=[
            pl.BlockSpec((MERGE_TM, D_MODEL), row),
            pl.BlockSpec((MERGE_TM, D_MODEL), row),
            pl.BlockSpec((MERGE_TM, 2 * D_MODEL), lambda i: (i, GATE_OFF // (2 * D_MODEL))),
            pl.BlockSpec((MERGE_TM, D_MODEL), row),
            pl.BlockSpec((D_MODEL, D_MODEL), const),
            pl.BlockSpec((D_MODEL, D_MODEL), const),
            pl.BlockSpec((D_MODEL, D_MODEL), const),
            pl.BlockSpec((2, D_MODEL), const),
            pl.BlockSpec((1, D_MODEL), const),
        ],
        out_specs=pl.BlockSpec((MERGE_TM, D_MODEL), row),
        out_shape=jax.ShapeDtypeStruct((n, D_MODEL), jnp.float32),
        compiler_params=pltpu.CompilerParams(
            dimension_semantics=("arbitrary",),
            vmem_limit_bytes=VMEM_LIMIT,
        ),
        name="merge",
    )(z_gla, z_att, proj, x2, w_o_gla, w_o_att, w_out, merge_bias2, g_post)


def kernel(x, norm_pre_g, w_in, gk_up, gk_bias, gla_norm_g, rel_bias, w_o_gla,
           w_o_att, merge_bias, w_out, norm_post_g):
    batch, seq, d = x.shape
    assert d == D_MODEL and seq % FRONT_ROWS == 0 and seq % GROUP_ROWS == 0
    assert (batch * seq) % MERGE_TM == 0
    assert w_in.shape == (D_MODEL, IN_COLS)
    assert rel_bias.shape == (ATT_HEADS, 2 * MAX_REL + 1)
    bf16 = jnp.bfloat16

    w_main, w_vt, w_gk = _wprep(w_in)
    up_pad = jnp.pad(gk_up, ((0, GK_PAD - GK_RANK), (0, 0))).astype(bf16)

    x2 = x.reshape(batch * seq, D_MODEL)
    proj, v_t, z_gla = _front(x2, norm_pre_g.reshape(1, D_MODEL), w_main, w_vt, w_gk,
                              up_pad, gk_bias.reshape(1, GLA_K_WIDTH), seq)
    z_att = _attn(proj, v_t, _bias_table(rel_bias), batch, seq)
    w_og = (jnp.tile(gla_norm_g, GLA_HEADS)[:, None] * w_o_gla).astype(bf16)
    out = _merge(z_gla, z_att, proj, x2, w_og, w_o_att.astype(bf16),
                 w_out.astype(bf16), merge_bias.reshape(2, D_MODEL),
                 norm_post_g.reshape(1, D_MODEL))
    return out.reshape(batch, seq, D_MODEL)
```

```python
import functools

import numpy as np
import jax
import jax.numpy as jnp
from jax import lax
from jax.experimental import pallas as pl
from jax.experimental.pallas import tpu as pltpu

D_MODEL = 1024
CHUNK = 64
GLA_HEADS = 4
GLA_DK = 128
GLA_DV = 256
GLA_K_WIDTH = GLA_HEADS * GLA_DK
GLA_V_WIDTH = GLA_HEADS * GLA_DV
GK_RANK = 16
GATE_NORMALIZER = 16.0
ATT_HEADS = 16
ATT_DH = 64
ATT_WIDTH = ATT_HEADS * ATT_DH
LEFT_CHUNKS = 8
MAX_REL = 256
RMS_EPS = 1e-6
NEG_INF = -1e30

LANES = 128
BF16_SUBLANES = 16
LOG2_E = float(np.log2(np.e))
LN_2 = float(np.log(2.0))

GATE_OFF = 0
QA_OFF = 2048
KA_OFF = 3072
GA_OFF = 4096
PROJ_COLS = 5120
VG_OFF = 5120
GG_OFF = 6144
QG_OFF = 7168
KG_OFF = 7680
W_COLS = 8192
GK_PAD = LANES

_SRC_QG, _SRC_KG, _SRC_VG, _SRC_GG, _SRC_CODE, _SRC_QA, _SRC_KA, _SRC_VA, _SRC_GA, _SRC_GATE = (
    int(v) for v in np.cumsum([0, GLA_K_WIDTH, GLA_K_WIDTH, GLA_V_WIDTH, GLA_V_WIDTH, GK_RANK,
                               ATT_WIDTH, ATT_WIDTH, ATT_WIDTH, ATT_WIDTH]))
IN_COLS = _SRC_GATE + 2 * D_MODEL
REGROUP = (
    (QA_OFF, _SRC_QA, ATT_WIDTH), (KA_OFF, _SRC_KA, ATT_WIDTH), (GA_OFF, _SRC_GA, ATT_WIDTH),
    (GATE_OFF, _SRC_GATE, 2 * D_MODEL), (VG_OFF, _SRC_VG, GLA_V_WIDTH),
    (GG_OFF, _SRC_GG, GLA_V_WIDTH), (QG_OFF, _SRC_QG, GLA_K_WIDTH), (KG_OFF, _SRC_KG, GLA_K_WIDTH),
)
WPREP_ROWS = 128

GROUP = 2
STAGE_GROUPS = 2
GROUP_ROWS = GROUP * CHUNK
WIN_CHUNKS = GROUP + LEFT_CHUNKS
WIN_ROWS = WIN_CHUNKS * CHUNK
HEADS_PER_STEP = LANES // ATT_DH
PAIR_COLS = HEADS_PER_STEP * GROUP_ROWS
TABLE_PERIOD = GROUP_ROWS + WIN_ROWS
ZERO_LO = (GROUP - 1) * CHUNK
ZERO_HI = WIN_ROWS - GROUP_ROWS - MAX_REL

FRONT_ROWS = 512
FILL_COLS = 512
MERGE_TM = 1024
VMEM_LIMIT = 56 * 1024 * 1024

_NT = (((1,), (1,)), ((), ()))


def _sigmoid(x):
    return 1.0 / (1.0 + jnp.exp2(x * (-LOG2_E)))


def _wprep_kernel(wt_ref, main_ref, vt_ref, gk_ref):
    for dst, src, width in REGROUP:
        blk = wt_ref[src:src + width, :].T
        if dst == QA_OFF:
            blk = blk * (ATT_DH ** -0.5 * LOG2_E)
        main_ref[:, dst:dst + width] = blk.astype(jnp.bfloat16)
    vt_ref[...] = wt_ref[_SRC_VA:_SRC_VA + ATT_WIDTH, :].astype(jnp.bfloat16)
    code = wt_ref[_SRC_CODE:_SRC_CODE + GK_PAD, :].T
    lane = lax.broadcasted_iota(jnp.int32, code.shape, 1)
    gk_ref[...] = jnp.where(lane < GK_RANK, code, 0.0).astype(jnp.bfloat16)


def _wprep(w_in):
    k = w_in.shape[0]
    return pl.pallas_call(
        _wprep_kernel,
        grid=(k // WPREP_ROWS,),
        in_specs=[pl.BlockSpec((IN_COLS, WPREP_ROWS), lambda i: (0, i))],
        out_specs=[
            pl.BlockSpec((WPREP_ROWS, W_COLS), lambda i: (i, 0)),
            pl.BlockSpec((ATT_WIDTH, WPREP_ROWS), lambda i: (0, i)),
            pl.BlockSpec((WPREP_ROWS, GK_PAD), lambda i: (i, 0)),
        ],
        out_shape=[
            jax.ShapeDtypeStruct((k, W_COLS), jnp.bfloat16),
            jax.ShapeDtypeStruct((ATT_WIDTH, k), jnp.bfloat16),
            jax.ShapeDtypeStruct((k, GK_PAD), jnp.bfloat16),
        ],
        compiler_params=pltpu.CompilerParams(
            dimension_semantics=("arbitrary",), vmem_limit_bytes=VMEM_LIMIT),
        name="wprep",
    )(w_in.T)


def _chunk_cumsum(x):
    n_chunks = x.shape[0] // CHUNK
    hi = x.astype(jnp.bfloat16)
    lo = (x - hi.astype(jnp.float32)).astype(jnp.bfloat16)
    r = lax.broadcasted_iota(jnp.int32, (CHUNK, 2 * CHUNK), 0)
    c = lax.broadcasted_iota(jnp.int32, (CHUNK, 2 * CHUNK), 1)
    tri = (r >= jnp.bitwise_and(c, CHUNK - 1)).astype(jnp.bfloat16)
    out = []
    for n in range(n_chunks):
        rows = slice(n * CHUNK, (n + 1) * CHUNK)
        terms = jnp.concatenate([hi[rows], lo[rows]], axis=0)
        out.append(jnp.dot(tri, terms, preferred_element_type=jnp.float32))
    return jnp.stack(out, axis=0)


def _front_kernel(x_ref, g_ref, w_ref, wvt_ref, wgk_ref, up_ref, bias_ref,
                  proj_ref, vt_ref, z_ref, state_ref, *, steps_per_seq):
    n_chunks = FRONT_ROWS // CHUNK

    @pl.when(pl.program_id(0) % steps_per_seq == 0)
    def _():
        state_ref[...] = jnp.zeros_like(state_ref)

    x = x_ref[...]
    ms = jnp.mean(x * x, axis=-1, keepdims=True)
    h = (x * lax.rsqrt(ms + RMS_EPS) * g_ref[...]).astype(jnp.bfloat16)

    def project(off, width):
        return jnp.dot(h, w_ref[:, off:off + width],
                       preferred_element_type=jnp.float32).astype(jnp.bfloat16)

    def emit_proj(off):
        def run():
            proj_ref[:, off:off + FILL_COLS] = project(off, FILL_COLS)
        return run

    def emit_vt():
        vt_ref[...] = lax.dot_general(
            wvt_ref[...], h, _NT, preferred_element_type=jnp.float32
        ).astype(jnp.bfloat16)

    filler = [emit_proj(off) for off in range(0, PROJ_COLS, FILL_COLS)] + [emit_vt]

    code = jnp.dot(h, wgk_ref[...], preferred_element_type=jnp.float32).astype(jnp.bfloat16)
    v = project(VG_OFF, GLA_V_WIDTH)
    qk = project(QG_OFF, 2 * GLA_K_WIDTH)
    q, k = qk[:, :GLA_K_WIDTH], qk[:, GLA_K_WIDTH:]
    logits = jnp.dot(code, up_ref[...],
                     preferred_element_type=jnp.float32) + bias_ref[...]
    gate_all = project(GG_OFF, GLA_V_WIDTH)
    gk = (jnp.minimum(logits, 0.0) * (1.0 / GATE_NORMALIZER)
          - jnp.log2(1.0 + jnp.exp2(jnp.abs(logits) * (-LOG2_E)))
          * (LN_2 / GATE_NORMALIZER))
    a_cum3 = _chunk_cumsum(gk)
    filler.pop(0)()
    a_end = a_cum3[:, CHUNK - 1, :]
    k_dec = (k.astype(jnp.float32).reshape(n_chunks, CHUNK, GLA_K_WIDTH)
             * jnp.exp(a_end[:, None, :] - a_cum3)).astype(jnp.bfloat16)
    decay_t = jnp.exp(a_end).T
    kcols = [slice(hd * GLA_DK, (hd + 1) * GLA_DK) for hd in range(GLA_HEADS)]
    vcols = [slice(hd * GLA_DV, (hd + 1) * GLA_DV) for hd in range(GLA_HEADS)]

    def chunk_rows(c):
        return slice(c * CHUNK, (c + 1) * CHUNK)

    def increments(c):
        return [lax.dot_general(
            k_dec[c][:, kcols[hd]], v[chunk_rows(c), vcols[hd]],
            (((0,), (0,)), ((), ())), preferred_element_type=jnp.float32)
            for hd in range(GLA_HEADS)]

    def finish(c, reads):
        for hd in range(GLA_HEADS):
            o = reads[hd]
            ms_o = jnp.mean(o * o, axis=-1, keepdims=True)
            factor = lax.rsqrt(ms_o + RMS_EPS * GLA_DK)
            gate = gate_all[chunk_rows(c), vcols[hd]].astype(jnp.float32)
            z_ref[chunk_rows(c), vcols[hd]] = (
                (o * factor) * (gate * _sigmoid(gate))).astype(jnp.bfloat16)

    upd_next = increments(0)
    reads_prev = None
    for c in range(n_chunks):
        upd = upd_next
        if c + 1 < n_chunks:
            upd_next = increments(c + 1)
        if filler:
            filler.pop(0)()
        reads = []
        for hd in range(GLA_HEADS):
            state = decay_t[kcols[hd], c:c + 1] * state_ref[hd] + upd[hd]
            state_ref[hd] = state
            reads.append(jnp.dot(q[chunk_rows(c), kcols[hd]],
                                 state.astype(jnp.bfloat16),
                                 preferred_element_type=jnp.float32))
        if reads_prev is not None:
            finish(c - 1, reads_prev)
        reads_prev = reads
    finish(n_chunks - 1, reads_prev)
    while filler:
        filler.pop(0)()


def _front(x2, g_pre, w_main, w_vt, w_gk, up_pad, gk_bias, seq):
    n = x2.shape[0]
    const = lambda i: (0, 0)
    resident = pl.Buffered(1)
    return pl.pallas_call(
        functools.partial(_front_kernel, steps_per_seq=seq // FRONT_ROWS),
        grid=(n // FRONT_ROWS,),
        in_specs=[
            pl.BlockSpec((FRONT_ROWS, D_MODEL), lambda i: (i, 0)),
            pl.BlockSpec((1, D_MODEL), const),
            pl.BlockSpec((D_MODEL, W_COLS), const, pipeline_mode=resident),
            pl.BlockSpec((ATT_WIDTH, D_MODEL), const, pipeline_mode=resident),
            pl.BlockSpec((D_MODEL, GK_PAD), const, pipeline_mode=resident),
            pl.BlockSpec((GK_PAD, GLA_K_WIDTH), const, pipeline_mode=resident),
            pl.BlockSpec((1, GLA_K_WIDTH), const),
        ],
        out_specs=[
            pl.BlockSpec((FRONT_ROWS, PROJ_COLS), lambda i: (i, 0)),
            pl.BlockSpec((ATT_WIDTH, FRONT_ROWS), lambda i: (0, i)),
            pl.BlockSpec((FRONT_ROWS, GLA_V_WIDTH), lambda i: (i, 0)),
        ],
        out_shape=[
            jax.ShapeDtypeStruct((n, PROJ_COLS), jnp.bfloat16),
            jax.ShapeDtypeStruct((ATT_WIDTH, n), jnp.bfloat16),
            jax.ShapeDtypeStruct((n, GLA_V_WIDTH), jnp.bfloat16),
        ],
        scratch_shapes=[pltpu.VMEM((GLA_HEADS, GLA_DK, GLA_DV), jnp.float32)],
        compiler_params=pltpu.CompilerParams(
            dimension_semantics=("arbitrary",),
            vmem_limit_bytes=VMEM_LIMIT,
        ),
        name="front",
    )(x2, g_pre, w_main, w_vt, w_gk, up_pad, gk_bias)


def _key_window(g):
    lead = LEFT_CHUNKS // GROUP
    k_start = max(g - lead, 0) * GROUP_ROWS
    return k_start, (g + 1) * GROUP_ROWS - k_start


def _attn_scores(q_ref, k_ref, tbl_ref, g):
    k_start, n_keys = _key_window(g)
    q2 = q_ref[g * GROUP_ROWS:(g + 1) * GROUP_ROWS, :]
    lane = lax.broadcasted_iota(jnp.int32, q2.shape, 1)
    first = lane < ATT_DH
    zero = jnp.zeros_like(q2)
    qs = jnp.concatenate(
        [jnp.where(first, q2, zero), jnp.where(first, zero, q2)], axis=0)
    kw = k_ref[k_start:k_start + n_keys, :]
    s = lax.dot_general(kw, qs, _NT, preferred_element_type=jnp.float32)
    lo = WIN_ROWS - n_keys
    z0, z1 = max(ZERO_LO, lo) - lo, max(ZERO_HI, lo) - lo
    parts = []
    if z0 > 0:
        parts.append(s[:z0] + tbl_ref[0, lo:lo + z0, :])
    if z1 > z0:
        parts.append(s[z0:z1])
    parts.append(s[z1:] + tbl_ref[0, lo + z1:, :])
    s = jnp.concatenate(parts, axis=0)
    return s, jnp.max(s, axis=0, keepdims=True)


def _attn_values(vt_ref, p, g):
    k_start, n_keys = _key_window(g)
    vt_aug = jnp.concatenate(
        [vt_ref[:, k_start:k_start + n_keys],
         jnp.ones((BF16_SUBLANES, n_keys), jnp.bfloat16)], axis=0)
    ot = jnp.dot(vt_aug, p, preferred_element_type=jnp.float32)
    r = 1.0 / ot[LANES:LANES + 1, :]
    return jnp.concatenate(
        [ot[:ATT_DH, :GROUP_ROWS] * r[:, :GROUP_ROWS],
         ot[ATT_DH:LANES, GROUP_ROWS:] * r[:, GROUP_ROWS:]], axis=0)


def _attn_kernel(q_ref, k_ref, vt_ref, g_ref, tbl_ref, z_ref, *, seq):
    n_stages = seq // (STAGE_GROUPS * GROUP_ROWS)

    def scores(t):
        return [_attn_scores(q_ref, k_ref, tbl_ref, STAGE_GROUPS * t + j)
                for j in range(STAGE_GROUPS)]

    cur = scores(0)
    for t in range(n_stages):
        if t + 1 < n_stages:
            nxt = scores(t + 1)
        probs = [jnp.exp2(s - m).astype(jnp.bfloat16) for s, m in cur]
        ot = jnp.concatenate(
            [_attn_values(vt_ref, p, STAGE_GROUPS * t + j) for j, p in enumerate(probs)],
            axis=1)
        rows = slice(t * STAGE_GROUPS * GROUP_ROWS, (t + 1) * STAGE_GROUPS * GROUP_ROWS)
        gate = g_ref[rows, :].astype(jnp.float32)
        z_ref[rows, :] = (ot.T * gate * _sigmoid(gate)).astype(jnp.bfloat16)
        if t + 1 < n_stages:
            cur = nxt


def _attn(proj, v_t, table, batch, seq):
    n = proj.shape[0]
    pairs = ATT_HEADS // HEADS_PER_STEP
    col_map = lambda off: (lambda p, b: (b, off // LANES + p))
    return pl.pallas_call(
        functools.partial(_attn_kernel, seq=seq),
        grid=(pairs, batch),
        in_specs=[
            pl.BlockSpec((seq, LANES), col_map(QA_OFF)),
            pl.BlockSpec((seq, LANES), col_map(KA_OFF)),
            pl.BlockSpec((LANES, seq), lambda p, b: (p, b)),
            pl.BlockSpec((seq, LANES), col_map(GA_OFF)),
            pl.BlockSpec((1, WIN_ROWS, PAIR_COLS), lambda p, b: (p, 0, 0)),
        ],
        out_specs=pl.BlockSpec((seq, LANES), lambda p, b: (b, p)),
        out_shape=jax.ShapeDtypeStruct((n, ATT_WIDTH), jnp.bfloat16),
        compiler_params=pltpu.CompilerParams(
            dimension_semantics=("arbitrary", "arbitrary"),
            vmem_limit_bytes=VMEM_LIMIT,
        ),
        name="attn",
    )(proj, proj, v_t, proj, table)


def _bias_table(rel_bias):
    heads = rel_bias.shape[0]
    pairs = heads // HEADS_PER_STEP
    t = np.arange(TABLE_PERIOD)
    dist = np.where(t < GROUP_ROWS, t, t - TABLE_PERIOD) + (WIN_ROWS - GROUP_ROWS)
    rel_idx = np.clip(dist, -MAX_REL, MAX_REL) + MAX_REL
    w = (rel_bias[:, rel_idx] - rel_bias[:, 2 * MAX_REL:]).astype(jnp.float32) * LOG2_E
    return pl.pallas_call(
        _bias_table_kernel,
        grid=(pairs,),
        in_specs=[pl.BlockSpec((HEADS_PER_STEP, 1, TABLE_PERIOD), lambda p: (p, 0, 0))],
        out_specs=pl.BlockSpec((1, WIN_ROWS, PAIR_COLS), lambda p: (p, 0, 0)),
        out_shape=jax.ShapeDtypeStruct((pairs, WIN_ROWS, PAIR_COLS), jnp.float32),
        compiler_params=pltpu.CompilerParams(
            dimension_semantics=("arbitrary",), vmem_limit_bytes=VMEM_LIMIT),
        name="bias_table",
    )(w.reshape(heads, 1, TABLE_PERIOD))


def _bias_table_kernel(w_ref, o_ref):
    j = lax.broadcasted_iota(jnp.int32, (WIN_ROWS, GROUP_ROWS), 0)
    i = lax.broadcasted_iota(jnp.int32, (WIN_ROWS, GROUP_ROWS), 1)
    dc = j // CHUNK - i // CHUNK
    valid = (dc >= 0) & (dc <= LEFT_CHUNKS)
    for hh in range(HEADS_PER_STEP):
        rows = jnp.broadcast_to(w_ref[hh], (WIN_ROWS, TABLE_PERIOD))
        toep = pltpu.roll(rows, 0, axis=1, stride=1, stride_axis=0)[:, :GROUP_ROWS]
        o_ref[0, :, hh * GROUP_ROWS:(hh + 1) * GROUP_ROWS] = jnp.where(valid, toep, NEG_INF)


def _merge_kernel(zg_ref, za_ref, l_ref, x_ref, wog_ref, woa_ref,
                  wout_ref, mb_ref, gp_ref, o_ref):
    yg = jnp.dot(zg_ref[...], wog_ref[...], preferred_element_type=jnp.float32)
    ya = jnp.dot(za_ref[...], woa_ref[...], preferred_element_type=jnp.float32)
    gate_g = _sigmoid(l_ref[:, :D_MODEL].astype(jnp.float32) + mb_ref[0:1, :])
    gate_a = _sigmoid(l_ref[:, D_MODEL:].astype(jnp.float32) + mb_ref[1:2, :])
    merged = (gate_g * yg + gate_a * ya).astype(jnp.bfloat16)
    y = jnp.dot(merged, wout_ref[...], preferred_element_type=jnp.float32)
    ms = jnp.mean(y * y, axis=-1, keepdims=True)
    o_ref[...] = x_ref[...] + y * lax.rsqrt(ms + RMS_EPS) * gp_ref[...]


def _merge(z_gla, z_att, proj, x2, w_o_gla, w_o_att, w_out, merge_bias2, g_post):
    n = x2.shape[0]
    row = lambda i: (i, 0)
    const = lambda i: (0, 0)
    return pl.pallas_call(
        _merge_kernel,
        grid=(n // MERGE_TM,),
        in_specs=[
            pl.BlockSpec((MERGE_TM, D_MODEL), row),
            pl.BlockSpec((MERGE_TM, D_MODEL), row),
            pl.BlockSpec((MERGE_TM, 2 * D_MODEL), lambda i: (i, GATE_OFF // (2 * D_MODEL))),
            pl.BlockSpec((MERGE_TM, D_MODEL), row),
            pl.BlockSpec((D_MODEL, D_MODEL), const),
            pl.BlockSpec((D_MODEL, D_MODEL), const),
            pl.BlockSpec((D_MODEL, D_MODEL), const),
            pl.BlockSpec((2, D_MODEL), const),
            pl.BlockSpec((1, D_MODEL), const),
        ],
        out_specs=pl.BlockSpec((MERGE_TM, D_MODEL), row),
        out_shape=jax.ShapeDtypeStruct((n, D_MODEL), jnp.float32),
        compiler_params=pltpu.CompilerParams(
            dimension_semantics=("arbitrary",),
            vmem_limit_bytes=VMEM_LIMIT,
        ),
        name="merge",
    )(z_gla, z_att, proj, x2, w_o_gla, w_o_att, w_out, merge_bias2, g_post)


def kernel(x, norm_pre_g, w_in, gk_up, gk_bias, gla_norm_g, rel_bias, w_o_gla,
           w_o_att, merge_bias, w_out, norm_post_g):
    batch, seq, d = x.shape
    assert d == D_MODEL and seq % FRONT_ROWS == 0 and seq % GROUP_ROWS == 0
    assert (batch * seq) % MERGE_TM == 0
    assert w_in.shape == (D_MODEL, IN_COLS)
    assert rel_bias.shape == (ATT_HEADS, 2 * MAX_REL + 1)
    bf16 = jnp.bfloat16

    w_main, w_vt, w_gk = _wprep(w_in)
    up_pad = jnp.pad(gk_up, ((0, GK_PAD - GK_RANK), (0, 0))).astype(bf16)

    x2 = x.reshape(batch * seq, D_MODEL)
    proj, v_t, z_gla = _front(x2, norm_pre_g.reshape(1, D_MODEL), w_main, w_vt, w_gk,
                              up_pad, gk_bias.reshape(1, GLA_K_WIDTH), seq)
    z_att = _attn(proj, v_t, _bias_table(rel_bias), batch, seq)
    w_og = (jnp.tile(gla_norm_g, GLA_HEADS)[:, None] * w_o_gla).astype(bf16)
    out = _merge(z_gla, z_att, proj, x2, w_og, w_o_att.astype(bf16),
                 w_out.astype(bf16), merge_bias.reshape(2, D_MODEL),
                 norm_post_g.reshape(1, D_MODEL))
    return out.reshape(batch, seq, D_MODEL)
```

```python
import functools

import numpy as np
import jax
import jax.numpy as jnp
from jax import lax
from jax.experimental import pallas as pl
from jax.experimental.pallas import tpu as pltpu

D_MODEL = 1024
CHUNK = 64
GLA_HEADS = 4
GLA_DK = 128
GLA_DV = 256
GLA_K_WIDTH = GLA_HEADS * GLA_DK
GLA_V_WIDTH = GLA_HEADS * GLA_DV
GK_RANK = 16
GATE_NORMALIZER = 16.0
ATT_HEADS = 16
ATT_DH = 64
ATT_WIDTH = ATT_HEADS * ATT_DH
LEFT_CHUNKS = 8
MAX_REL = 256
RMS_EPS = 1e-6
NEG_INF = -1e30

LANES = 128
BF16_SUBLANES = 16
LOG2_E = float(np.log2(np.e))
LN_2 = float(np.log(2.0))

GATE_OFF = 0
QA_OFF = 2048
KA_OFF = 3072
GA_OFF = 4096
PROJ_COLS = 5120
VG_OFF = 5120
GG_OFF = 6144
QG_OFF = 7168
KG_OFF = 7680
W_COLS = 8192
GK_PAD = LANES

_SRC_QG, _SRC_KG, _SRC_VG, _SRC_GG, _SRC_CODE, _SRC_QA, _SRC_KA, _SRC_VA, _SRC_GA, _SRC_GATE = (
    int(v) for v in np.cumsum([0, GLA_K_WIDTH, GLA_K_WIDTH, GLA_V_WIDTH, GLA_V_WIDTH, GK_RANK,
                               ATT_WIDTH, ATT_WIDTH, ATT_WIDTH, ATT_WIDTH]))
IN_COLS = _SRC_GATE + 2 * D_MODEL
REGROUP = (
    (QA_OFF, _SRC_QA, ATT_WIDTH), (KA_OFF, _SRC_KA, ATT_WIDTH), (GA_OFF, _SRC_GA, ATT_WIDTH),
    (GATE_OFF, _SRC_GATE, 2 * D_MODEL), (VG_OFF, _SRC_VG, GLA_V_WIDTH),
    (GG_OFF, _SRC_GG, GLA_V_WIDTH), (QG_OFF, _SRC_QG, GLA_K_WIDTH), (KG_OFF, _SRC_KG, GLA_K_WIDTH),
)
WPREP_ROWS = 128

GROUP = 2
STAGE_GROUPS = 2
GROUP_ROWS = GROUP * CHUNK
WIN_CHUNKS = GROUP + LEFT_CHUNKS
WIN_ROWS = WIN_CHUNKS * CHUNK
HEADS_PER_STEP = LANES // ATT_DH
PAIR_COLS = HEADS_PER_STEP * GROUP_ROWS
TABLE_PERIOD = GROUP_ROWS + WIN_ROWS
ZERO_LO = (GROUP - 1) * CHUNK
ZERO_HI = WIN_ROWS - GROUP_ROWS - MAX_REL

FRONT_ROWS = 512
FILL_COLS = 512
MERGE_TM = 1024
VMEM_LIMIT = 56 * 1024 * 1024

_NT = (((1,), (1,)), ((), ()))


def _sigmoid(x):
    return 1.0 / (1.0 + jnp.exp2(x * (-LOG2_E)))


def _wprep_kernel(wt_ref, main_ref, vt_ref, gk_ref):
    for dst, src, width in REGROUP:
        blk = wt_ref[src:src + width, :].T
        if dst == QA_OFF:
            blk = blk * (ATT_DH ** -0.5 * LOG2_E)
        main_ref[:, dst:dst + width] = blk.astype(jnp.bfloat16)
    vt_ref[...] = wt_ref[_SRC_VA:_SRC_VA + ATT_WIDTH, :].astype(jnp.bfloat16)
    code = wt_ref[_SRC_CODE:_SRC_CODE + GK_PAD, :].T
    lane = lax.broadcasted_iota(jnp.int32, code.shape, 1)
    gk_ref[...] = jnp.where(lane < GK_RANK, code, 0.0).astype(jnp.bfloat16)


def _wprep(w_in):
    k = w_in.shape[0]
    return pl.pallas_call(
        _wprep_kernel,
        grid=(k // WPREP_ROWS,),
        in_specs=[pl.BlockSpec((IN_COLS, WPREP_ROWS), lambda i: (0, i))],
        out_specs=[
            pl.BlockSpec((WPREP_ROWS, W_COLS), lambda i: (i, 0)),
            pl.BlockSpec((ATT_WIDTH, WPREP_ROWS), lambda i: (0, i)),
            pl.BlockSpec((WPREP_ROWS, GK_PAD), lambda i: (i, 0)),
        ],
        out_shape=[
            jax.ShapeDtypeStruct((k, W_COLS), jnp.bfloat16),
            jax.ShapeDtypeStruct((ATT_WIDTH, k), jnp.bfloat16),
            jax.ShapeDtypeStruct((k, GK_PAD), jnp.bfloat16),
        ],
        compiler_params=pltpu.CompilerParams(
            dimension_semantics=("arbitrary",), vmem_limit_bytes=VMEM_LIMIT),
        name="wprep",
    )(w_in.T)


def _chunk_cumsum(x):
    n_chunks = x.shape[0] // CHUNK
    hi = x.astype(jnp.bfloat16)
    lo = (x - hi.astype(jnp.float32)).astype(jnp.bfloat16)
    r = lax.broadcasted_iota(jnp.int32, (CHUNK, 2 * CHUNK), 0)
    c = lax.broadcasted_iota(jnp.int32, (CHUNK, 2 * CHUNK), 1)
    tri = (r >= jnp.bitwise_and(c, CHUNK - 1)).astype(jnp.bfloat16)
    out = []
    for n in range(n_chunks):
        rows = slice(n * CHUNK, (n + 1) * CHUNK)
        terms = jnp.concatenate([hi[rows], lo[rows]], axis=0)
        out.append(jnp.dot(tri, terms, preferred_element_type=jnp.float32))
    return jnp.stack(out, axis=0)


def _front_kernel(x_ref, g_ref, w_ref, wvt_ref, wgk_ref, up_ref, bias_ref,
                  proj_ref, vt_ref, z_ref, state_ref, *, steps_per_seq):
    n_chunks = FRONT_ROWS // CHUNK

    @pl.when(pl.program_id(0) % steps_per_seq == 0)
    def _():
        state_ref[...] = jnp.zeros_like(state_ref)

    x = x_ref[...]
    ms = jnp.mean(x * x, axis=-1, keepdims=True)
    h = (x * lax.rsqrt(ms + RMS_EPS) * g_ref[...]).astype(jnp.bfloat16)

    def project(off, width):
        return jnp.dot(h, w_ref[:, off:off + width],
                       preferred_element_type=jnp.float32).astype(jnp.bfloat16)

    def emit_proj(off):
        def run():
            proj_ref[:, off:off + FILL_COLS] = project(off, FILL_COLS)
        return run

    def emit_vt():
        vt_ref[...] = lax.dot_general(
            wvt_ref[...], h, _NT, preferred_element_type=jnp.float32
        ).astype(jnp.bfloat16)

    filler = [emit_proj(off) for off in range(0, PROJ_COLS, FILL_COLS)] + [emit_vt]

    code = jnp.dot(h, wgk_ref[...], preferred_element_type=jnp.float32).astype(jnp.bfloat16)
    v = project(VG_OFF, GLA_V_WIDTH)
    qk = project(QG_OFF, 2 * GLA_K_WIDTH)
    q, k = qk[:, :GLA_K_WIDTH], qk[:, GLA_K_WIDTH:]
    logits = jnp.dot(code, up_ref[...],
                     preferred_element_type=jnp.float32) + bias_ref[...]
    gate_all = project(GG_OFF, GLA_V_WIDTH)
    gk = (jnp.minimum(logits, 0.0) * (1.0 / GATE_NORMALIZER)
          - jnp.log2(1.0 + jnp.exp2(jnp.abs(logits) * (-LOG2_E)))
          * (LN_2 / GATE_NORMALIZER))
    a_cum3 = _chunk_cumsum(gk)
    filler.pop(0)()
    a_end = a_cum3[:, CHUNK - 1, :]
    k_dec = (k.astype(jnp.float32).reshape(n_chunks, CHUNK, GLA_K_WIDTH)
             * jnp.exp(a_end[:, None, :] - a_cum3)).astype(jnp.bfloat16)
    decay_t = jnp.exp(a_end).T
    kcols = [slice(hd * GLA_DK, (hd + 1) * GLA_DK) for hd in range(GLA_HEADS)]
    vcols = [slice(hd * GLA_DV, (hd + 1) * GLA_DV) for hd in range(GLA_HEADS)]

    def chunk_rows(c):
        return slice(c * CHUNK, (c + 1) * CHUNK)

    def increments(c):
        return [lax.dot_general(
            k_dec[c][:, kcols[hd]], v[chunk_rows(c), vcols[hd]],
            (((0,), (0,)), ((), ())), preferred_element_type=jnp.float32)
            for hd in range(GLA_HEADS)]

    def finish(c, reads):
        for hd in range(GLA_HEADS):
            o = reads[hd]
            ms_o = jnp.mean(o * o, axis=-1, keepdims=True)
            factor = lax.rsqrt(ms_o + RMS_EPS * GLA_DK)
            gate = gate_all[chunk_rows(c), vcols[hd]].astype(jnp.float32)
            z_ref[chunk_rows(c), vcols[hd]] = (
                (o * factor) * (gate * _sigmoid(gate))).astype(jnp.bfloat16)

    upd_next = increments(0)
    reads_prev = None
    for c in range(n_chunks):
        upd = upd_next
        if c + 1 < n_chunks:
            upd_next = increments(c + 1)
        if filler:
            filler.pop(0)()
        reads = []
        for hd in range(GLA_HEADS):
            state = decay_t[kcols[hd], c:c + 1] * state_ref[hd] + upd[hd]
            state_ref[hd] = state
            reads.append(jnp.dot(q[chunk_rows(c), kcols[hd]],
                                 state.astype(jnp.bfloat16),
                                 preferred_element_type=jnp.float32))
        if reads_prev is not None:
            finish(c - 1, reads_prev)
        reads_prev = reads
    finish(n_chunks - 1, reads_prev)
    while filler:
        filler.pop(0)()


def _front(x2, g_pre, w_main, w_vt, w_gk, up_pad, gk_bias, seq):
    n = x2.shape[0]
    const = lambda i: (0, 0)
    resident = pl.Buffered(1)
    return pl.pallas_call(
        functools.partial(_front_kernel, steps_per_seq=seq // FRONT_ROWS),
        grid=(n // FRONT_ROWS,),
        in_specs=[
            pl.BlockSpec((FRONT_ROWS, D_MODEL), lambda i: (i, 0)),
            pl.BlockSpec((1, D_MODEL), const),
            pl.BlockSpec((D_MODEL, W_COLS), const, pipeline_mode=resident),
            pl.BlockSpec((ATT_WIDTH, D_MODEL), const, pipeline_mode=resident),
            pl.BlockSpec((D_MODEL, GK_PAD), const, pipeline_mode=resident),
            pl.BlockSpec((GK_PAD, GLA_K_WIDTH), const, pipeline_mode=resident),
            pl.BlockSpec((1, GLA_K_WIDTH), const),
        ],
        out_specs=[
            pl.BlockSpec((FRONT_ROWS, PROJ_COLS), lambda i: (i, 0)),
            pl.BlockSpec((ATT_WIDTH, FRONT_ROWS), lambda i: (0, i)),
            pl.BlockSpec((FRONT_ROWS, GLA_V_WIDTH), lambda i: (i, 0)),
        ],
        out_shape=[
            jax.ShapeDtypeStruct((n, PROJ_COLS), jnp.bfloat16),
            jax.ShapeDtypeStruct((ATT_WIDTH, n), jnp.bfloat16),
            jax.ShapeDtypeStruct((n, GLA_V_WIDTH), jnp.bfloat16),
        ],
        scratch_shapes=[pltpu.VMEM((GLA_HEADS, GLA_DK, GLA_DV), jnp.float32)],
        compiler_params=pltpu.CompilerParams(
            dimension_semantics=("arbitrary",),
            vmem_limit_bytes=VMEM_LIMIT,
        ),
        name="front",
    )(x2, g_pre, w_main, w_vt, w_gk, up_pad, gk_bias)


def _key_window(g):
    lead = LEFT_CHUNKS // GROUP
    k_start = max(g - lead, 0) * GROUP_ROWS
    return k_start, (g + 1) * GROUP_ROWS - k_start


def _attn_scores(q_ref, k_ref, tbl_ref, g):
    k_start, n_keys = _key_window(g)
    q2 = q_ref[g * GROUP_ROWS:(g + 1) * GROUP_ROWS, :]
    lane = lax.broadcasted_iota(jnp.int32, q2.shape, 1)
    first = lane < ATT_DH
    zero = jnp.zeros_like(q2)
    qs = jnp.concatenate(
        [jnp.where(first, q2, zero), jnp.where(first, zero, q2)], axis=0)
    kw = k_ref[k_start:k_start + n_keys, :]
    s = lax.dot_general(kw, qs, _NT, preferred_element_type=jnp.float32)
    lo = WIN_ROWS - n_keys
    z0, z1 = max(ZERO_LO, lo) - lo, max(ZERO_HI, lo) - lo
    parts = []
    if z0 > 0:
        parts.append(s[:z0] + tbl_ref[0, lo:lo + z0, :])
    if z1 > z0:
        parts.append(s[z0:z1])
    parts.append(s[z1:] + tbl_ref[0, lo + z1:, :])
    s = jnp.concatenate(parts, axis=0)
    return s, jnp.max(s, axis=0, keepdims=True)


def _attn_values(vt_ref, p, g):
    k_start, n_keys = _key_window(g)
    vt_aug = jnp.concatenate(
        [vt_ref[:, k_start:k_start + n_keys],
         jnp.ones((BF16_SUBLANES, n_keys), jnp.bfloat16)], axis=0)
    ot = jnp.dot(vt_aug, p, preferred_element_type=jnp.float32)
    r = 1.0 / ot[LANES:LANES + 1, :]
    return jnp.concatenate(
        [ot[:ATT_DH, :GROUP_ROWS] * r[:, :GROUP_ROWS],
         ot[ATT_DH:LANES, GROUP_ROWS:] * r[:, GROUP_ROWS:]], axis=0)


def _attn_kernel(q_ref, k_ref, vt_ref, g_ref, tbl_ref, z_ref, *, seq):
    n_stages = seq // (STAGE_GROUPS * GROUP_ROWS)

    def scores(t):
        return [_attn_scores(q_ref, k_ref, tbl_ref, STAGE_GROUPS * t + j)
                for j in range(STAGE_GROUPS)]

    cur = scores(0)
    for t in range(n_stages):
        if t + 1 < n_stages:
            nxt = scores(t + 1)
        probs = [jnp.exp2((s - m).astype(jnp.bfloat16)) for s, m in cur]
        ot = jnp.concatenate(
            [_attn_values(vt_ref, p, STAGE_GROUPS * t + j) for j, p in enumerate(probs)],
            axis=1)
        rows = slice(t * STAGE_GROUPS * GROUP_ROWS, (t + 1) * STAGE_GROUPS * GROUP_ROWS)
        gate = g_ref[rows, :].astype(jnp.float32)
        z_ref[rows, :] = (ot.T * gate * _sigmoid(gate)).astype(jnp.bfloat16)
        if t + 1 < n_stages:
            cur = nxt


def _attn(proj, v_t, table, batch, seq):
    n = proj.shape[0]
    pairs = ATT_HEADS // HEADS_PER_STEP
    col_map = lambda off: (lambda p, b: (b, off // LANES + p))
    return pl.pallas_call(
        functools.partial(_attn_kernel, seq=seq),
        grid=(pairs, batch),
        in_specs=[
            pl.BlockSpec((seq, LANES), col_map(QA_OFF)),
            pl.BlockSpec((seq, LANES), col_map(KA_OFF)),
            pl.BlockSpec((LANES, seq), lambda p, b: (p, b)),
            pl.BlockSpec((seq, LANES), col_map(GA_OFF)),
            pl.BlockSpec((1, WIN_ROWS, PAIR_COLS), lambda p, b: (p, 0, 0)),
        ],
        out_specs=pl.BlockSpec((seq, LANES), lambda p, b: (b, p)),
        out_shape=jax.ShapeDtypeStruct((n, ATT_WIDTH), jnp.bfloat16),
        compiler_params=pltpu.CompilerParams(
            dimension_semantics=("arbitrary", "arbitrary"),
            vmem_limit_bytes=VMEM_LIMIT,
        ),
        name="attn",
    )(proj, proj, v_t, proj, table)


def _bias_table(rel_bias):
    heads = rel_bias.shape[0]
    pairs = heads // HEADS_PER_STEP
    t = np.arange(TABLE_PERIOD)
    dist = np.where(t < GROUP_ROWS, t, t - TABLE_PERIOD) + (WIN_ROWS - GROUP_ROWS)
    rel_idx = np.clip(dist, -MAX_REL, MAX_REL) + MAX_REL
    w = (rel_bias[:, rel_idx] - rel_bias[:, 2 * MAX_REL:]).astype(jnp.float32) * LOG2_E
    return pl.pallas_call(
        _bias_table_kernel,
        grid=(pairs,),
        in_specs=[pl.BlockSpec((HEADS_PER_STEP, 1, TABLE_PERIOD), lambda p: (p, 0, 0))],
        out_specs=pl.BlockSpec((1, WIN_ROWS, PAIR_COLS), lambda p: (p, 0, 0)),
        out_shape=jax.ShapeDtypeStruct((pairs, WIN_ROWS, PAIR_COLS), jnp.float32),
        compiler_params=pltpu.CompilerParams(
            dimension_semantics=("arbitrary",), vmem_limit_bytes=VMEM_LIMIT),
        name="bias_table",
    )(w.reshape(heads, 1, TABLE_PERIOD))


def _bias_table_kernel(w_ref, o_ref):
    j = lax.broadcasted_iota(jnp.int32, (WIN_ROWS, GROUP_ROWS), 0)
    i = lax.broadcasted_iota(jnp.int32, (WIN_ROWS, GROUP_ROWS), 1)
    dc = j // CHUNK - i // CHUNK
    valid = (dc >= 0) & (dc <= LEFT_CHUNKS)
    for hh in range(HEADS_PER_STEP):
        rows = jnp.broadcast_to(w_ref[hh], (WIN_ROWS, TABLE_PERIOD))
        toep = pltpu.roll(rows, 0, axis=1, stride=1, stride_axis=0)[:, :GROUP_ROWS]
        o_ref[0, :, hh * GROUP_ROWS:(hh + 1) * GROUP_ROWS] = jnp.where(valid, toep, NEG_INF)


def _merge_kernel(zg_ref, za_ref, l_ref, x_ref, wog_ref, woa_ref,
                  wout_ref, mb_ref, gp_ref, o_ref):
    yg = jnp.dot(zg_ref[...], wog_ref[...], preferred_element_type=jnp.float32)
    ya = jnp.dot(za_ref[...], woa_ref[...], preferred_element_type=jnp.float32)
    gate_g = _sigmoid(l_ref[:, :D_MODEL].astype(jnp.float32) + mb_ref[0:1, :])
    gate_a = _sigmoid(l_ref[:, D_MODEL:].astype(jnp.float32) + mb_ref[1:2, :])
    merged = (gate_g * yg + gate_a * ya).astype(jnp.bfloat16)
    y = jnp.dot(merged, wout_ref[...], preferred_element_type=jnp.float32)
    ms = jnp.mean(y * y, axis=-1, keepdims=True)
    o_ref[...] = x_ref[...] + y * lax.rsqrt(ms + RMS_EPS) * gp_ref[...]


def _merge(z_gla, z_att, proj, x2, w_o_gla, w_o_att, w_out, merge_bias2, g_post):
    n = x2.shape[0]
    row = lambda i: (i, 0)
    const = lambda i: (0, 0)
    return pl.pallas_call(
        _merge_kernel,
        grid=(n // MERGE_TM,),
        in_specs=[
            pl.BlockSpec((MERGE_TM, D_MODEL), row),
            pl.BlockSpec((MERGE_TM, D_MODEL), row),
            pl.BlockSpec((MERGE_TM, 2 * D_MODEL), lambda i: (i, GATE_OFF // (2 * D_MODEL))),
            pl.BlockSpec((MERGE_TM, D_MODEL), row),
            pl.BlockSpec((D_MODEL, D_MODEL), const),
            pl.BlockSpec((D_MODEL, D_MODEL), const),
            pl.BlockSpec((D_MODEL, D_MODEL), const),
            pl.BlockSpec((2, D_MODEL), const),
            pl.BlockSpec((1, D_MODEL), const),
        ],
        out_specs=pl.BlockSpec((MERGE_TM, D_MODEL), row),
        out_shape=jax.ShapeDtypeStruct((n, D_MODEL), jnp.float32),
        compiler_params=pltpu.CompilerParams(
            dimension_semantics=("arbitrary",),
            vmem_limit_bytes=VMEM_LIMIT,
        ),
        name="merge",
    )(z_gla, z_att, proj, x2, w_o_gla, w_o_att, w_out, merge_bias2, g_post)


def kernel(x, norm_pre_g, w_in, gk_up, gk_bias, gla_norm_g, rel_bias, w_o_gla,
           w_o_att, merge_bias, w_out, norm_post_g):
    batch, seq, d = x.shape
    assert d == D_MODEL and seq % FRONT_ROWS == 0 and seq % GROUP_ROWS == 0
    assert (batch * seq) % MERGE_TM == 0
    assert w_in.shape == (D_MODEL, IN_COLS)
    assert rel_bias.shape == (ATT_HEADS, 2 * MAX_REL + 1)
    bf16 = jnp.bfloat16

    w_main, w_vt, w_gk = _wprep(w_in)
    up_pad = jnp.pad(gk_up, ((0, GK_PAD - GK_RANK), (0, 0))).astype(bf16)

    x2 = x.reshape(batch * seq, D_MODEL)
    proj, v_t, z_gla = _front(x2, norm_pre_g.reshape(1, D_MODEL), w_main, w_vt, w_gk,
                              up_pad, gk_bias.reshape(1, GLA_K_WIDTH), seq)
    z_att = _attn(proj, v_t, _bias_table(rel_bias), batch, seq)
    w_og = (jnp.tile(gla_norm_g, GLA_HEADS)[:, None] * w_o_gla).astype(bf16)
    out = _merge(z_gla, z_att, proj, x2, w_og, w_o_att.astype(bf16),
                 w_out.astype(bf16), merge_bias.reshape(2, D_MODEL),
                 norm_post_g.reshape(1, D_MODEL))
    return out.reshape(batch, seq, D_MODEL)
```

```python
import functools

import numpy as np
import jax
import jax.numpy as jnp
from jax import lax
from jax.experimental import pallas as pl
from jax.experimental.pallas import tpu as pltpu

D_MODEL = 1024
CHUNK = 64
GLA_HEADS = 4
GLA_DK = 128
GLA_DV = 256
GLA_K_WIDTH = GLA_HEADS * GLA_DK
GLA_V_WIDTH = GLA_HEADS * GLA_DV
GK_RANK = 16
GATE_NORMALIZER = 16.0
ATT_HEADS = 16
ATT_DH = 64
ATT_WIDTH = ATT_HEADS * ATT_DH
LEFT_CHUNKS = 8
MAX_REL = 256
RMS_EPS = 1e-6
NEG_INF = -1e30

LANES = 128
BF16_SUBLANES = 16
LOG2_E = float(np.log2(np.e))
LN_2 = float(np.log(2.0))

GATE_OFF = 0
QA_OFF = 2048
KA_OFF = 3072
GA_OFF = 4096
PROJ_COLS = 5120
VG_OFF = 5120
GG_OFF = 6144
QG_OFF = 7168
KG_OFF = 7680
W_COLS = 8192
GK_PAD = LANES

_SRC_QG, _SRC_KG, _SRC_VG, _SRC_GG, _SRC_CODE, _SRC_QA, _SRC_KA, _SRC_VA, _SRC_GA, _SRC_GATE = (
    int(v) for v in np.cumsum([0, GLA_K_WIDTH, GLA_K_WIDTH, GLA_V_WIDTH, GLA_V_WIDTH, GK_RANK,
                               ATT_WIDTH, ATT_WIDTH, ATT_WIDTH, ATT_WIDTH]))
IN_COLS = _SRC_GATE + 2 * D_MODEL
REGROUP = (
    (QA_OFF, _SRC_QA, ATT_WIDTH), (KA_OFF, _SRC_KA, ATT_WIDTH), (GA_OFF, _SRC_GA, ATT_WIDTH),
    (GATE_OFF, _SRC_GATE, 2 * D_MODEL), (VG_OFF, _SRC_VG, GLA_V_WIDTH),
    (GG_OFF, _SRC_GG, GLA_V_WIDTH), (QG_OFF, _SRC_QG, GLA_K_WIDTH), (KG_OFF, _SRC_KG, GLA_K_WIDTH),
)
WPREP_ROWS = 128

GROUP = 2
STAGE_GROUPS = 2
ATTN_SEQS = 4
GROUP_ROWS = GROUP * CHUNK
WIN_CHUNKS = GROUP + LEFT_CHUNKS
WIN_ROWS = WIN_CHUNKS * CHUNK
HEADS_PER_STEP = LANES // ATT_DH
PAIR_COLS = HEADS_PER_STEP * GROUP_ROWS
TABLE_PERIOD = GROUP_ROWS + WIN_ROWS
ZERO_LO = (GROUP - 1) * CHUNK
ZERO_HI = WIN_ROWS - GROUP_ROWS - MAX_REL

FRONT_ROWS = 512
FILL_COLS = 512
MERGE_TM = 1024
VMEM_LIMIT = 56 * 1024 * 1024

_NT = (((1,), (1,)), ((), ()))


def _sigmoid(x):
    return 1.0 / (1.0 + jnp.exp2(x * (-LOG2_E)))


def _wprep_kernel(wt_ref, main_ref, vt_ref, gk_ref):
    for dst, src, width in REGROUP:
        blk = wt_ref[src:src + width, :].T
        if dst == QA_OFF:
            blk = blk * (ATT_DH ** -0.5 * LOG2_E)
        main_ref[:, dst:dst + width] = blk.astype(jnp.bfloat16)
    vt_ref[...] = wt_ref[_SRC_VA:_SRC_VA + ATT_WIDTH, :].astype(jnp.bfloat16)
    code = wt_ref[_SRC_CODE:_SRC_CODE + GK_PAD, :].T
    lane = lax.broadcasted_iota(jnp.int32, code.shape, 1)
    gk_ref[...] = jnp.where(lane < GK_RANK, code, 0.0).astype(jnp.bfloat16)


def _wprep(w_in):
    k = w_in.shape[0]
    return pl.pallas_call(
        _wprep_kernel,
        grid=(k // WPREP_ROWS,),
        in_specs=[pl.BlockSpec((IN_COLS, WPREP_ROWS), lambda i: (0, i))],
        out_specs=[
            pl.BlockSpec((WPREP_ROWS, W_COLS), lambda i: (i, 0)),
            pl.BlockSpec((ATT_WIDTH, WPREP_ROWS), lambda i: (0, i)),
            pl.BlockSpec((WPREP_ROWS, GK_PAD), lambda i: (i, 0)),
        ],
        out_shape=[
            jax.ShapeDtypeStruct((k, W_COLS), jnp.bfloat16),
            jax.ShapeDtypeStruct((ATT_WIDTH, k), jnp.bfloat16),
            jax.ShapeDtypeStruct((k, GK_PAD), jnp.bfloat16),
        ],
        compiler_params=pltpu.CompilerParams(
            dimension_semantics=("arbitrary",), vmem_limit_bytes=VMEM_LIMIT),
        name="wprep",
    )(w_in.T)


def _chunk_cumsum(x):
    n_chunks = x.shape[0] // CHUNK
    hi = x.astype(jnp.bfloat16)
    lo = (x - hi.astype(jnp.float32)).astype(jnp.bfloat16)
    r = lax.broadcasted_iota(jnp.int32, (CHUNK, 2 * CHUNK), 0)
    c = lax.broadcasted_iota(jnp.int32, (CHUNK, 2 * CHUNK), 1)
    tri = (r >= jnp.bitwise_and(c, CHUNK - 1)).astype(jnp.bfloat16)
    out = []
    for n in range(n_chunks):
        rows = slice(n * CHUNK, (n + 1) * CHUNK)
        terms = jnp.concatenate([hi[rows], lo[rows]], axis=0)
        out.append(jnp.dot(tri, terms, preferred_element_type=jnp.float32))
    return jnp.stack(out, axis=0)


def _front_kernel(x_ref, g_ref, w_ref, wvt_ref, wgk_ref, up_ref, bias_ref,
                  proj_ref, vt_ref, z_ref, state_ref, *, steps_per_seq):
    n_chunks = FRONT_ROWS // CHUNK

    @pl.when(pl.program_id(0) % steps_per_seq == 0)
    def _():
        state_ref[...] = jnp.zeros_like(state_ref)

    x = x_ref[...]
    ms = jnp.mean(x * x, axis=-1, keepdims=True)
    h = (x * lax.rsqrt(ms + RMS_EPS) * g_ref[...]).astype(jnp.bfloat16)

    def project(off, width):
        return jnp.dot(h, w_ref[:, off:off + width],
                       preferred_element_type=jnp.float32).astype(jnp.bfloat16)

    def emit_proj(off):
        def run():
            proj_ref[:, off:off + FILL_COLS] = project(off, FILL_COLS)
        return run

    def emit_vt():
        vt_ref[...] = lax.dot_general(
            wvt_ref[...], h, _NT, preferred_element_type=jnp.float32
        ).astype(jnp.bfloat16)

    filler = [emit_proj(off) for off in range(0, PROJ_COLS, FILL_COLS)] + [emit_vt]

    code = jnp.dot(h, wgk_ref[...], preferred_element_type=jnp.float32).astype(jnp.bfloat16)
    v = project(VG_OFF, GLA_V_WIDTH)
    qk = project(QG_OFF, 2 * GLA_K_WIDTH)
    q, k = qk[:, :GLA_K_WIDTH], qk[:, GLA_K_WIDTH:]
    logits = jnp.dot(code, up_ref[...],
                     preferred_element_type=jnp.float32) + bias_ref[...]
    gate_all = project(GG_OFF, GLA_V_WIDTH)
    gk = (jnp.minimum(logits, 0.0) * (1.0 / GATE_NORMALIZER)
          - jnp.log2(1.0 + jnp.exp2(jnp.abs(logits) * (-LOG2_E)))
          * (LN_2 / GATE_NORMALIZER))
    a_cum3 = _chunk_cumsum(gk)
    filler.pop(0)()
    a_end = a_cum3[:, CHUNK - 1, :]
    k_dec = (k.astype(jnp.float32).reshape(n_chunks, CHUNK, GLA_K_WIDTH)
             * jnp.exp(a_end[:, None, :] - a_cum3)).astype(jnp.bfloat16)
    decay_t = jnp.exp(a_end).T
    kcols = [slice(hd * GLA_DK, (hd + 1) * GLA_DK) for hd in range(GLA_HEADS)]
    vcols = [slice(hd * GLA_DV, (hd + 1) * GLA_DV) for hd in range(GLA_HEADS)]

    def chunk_rows(c):
        return slice(c * CHUNK, (c + 1) * CHUNK)

    def increments(c):
        return [lax.dot_general(
            k_dec[c][:, kcols[hd]], v[chunk_rows(c), vcols[hd]],
            (((0,), (0,)), ((), ())), preferred_element_type=jnp.float32)
            for hd in range(GLA_HEADS)]

    def finish(c, reads):
        for hd in range(GLA_HEADS):
            o = reads[hd]
            ms_o = jnp.mean(o * o, axis=-1, keepdims=True)
            factor = lax.rsqrt(ms_o + RMS_EPS * GLA_DK)
            gate = gate_all[chunk_rows(c), vcols[hd]].astype(jnp.float32)
            z_ref[chunk_rows(c), vcols[hd]] = (
                (o * factor) * (gate * _sigmoid(gate))).astype(jnp.bfloat16)

    upd_next = increments(0)
    reads_prev = None
    for c in range(n_chunks):
        upd = upd_next
        if c + 1 < n_chunks:
            upd_next = increments(c + 1)
        if filler:
            filler.pop(0)()
        reads = []
        for hd in range(GLA_HEADS):
            state = decay_t[kcols[hd], c:c + 1] * state_ref[hd] + upd[hd]
            state_ref[hd] = state
            reads.append(jnp.dot(q[chunk_rows(c), kcols[hd]],
                                 state.astype(jnp.bfloat16),
                                 preferred_element_type=jnp.float32))
        if reads_prev is not None:
            finish(c - 1, reads_prev)
        reads_prev = reads
    finish(n_chunks - 1, reads_prev)
    while filler:
        filler.pop(0)()


def _front(x2, g_pre, w_main, w_vt, w_gk, up_pad, gk_bias, seq):
    n = x2.shape[0]
    const = lambda i: (0, 0)
    resident = pl.Buffered(1)
    return pl.pallas_call(
        functools.partial(_front_kernel, steps_per_seq=seq // FRONT_ROWS),
        grid=(n // FRONT_ROWS,),
        in_specs=[
            pl.BlockSpec((FRONT_ROWS, D_MODEL), lambda i: (i, 0)),
            pl.BlockSpec((1, D_MODEL), const),
            pl.BlockSpec((D_MODEL, W_COLS), const, pipeline_mode=resident),
            pl.BlockSpec((ATT_WIDTH, D_MODEL), const, pipeline_mode=resident),
            pl.BlockSpec((D_MODEL, GK_PAD), const, pipeline_mode=resident),
            pl.BlockSpec((GK_PAD, GLA_K_WIDTH), const, pipeline_mode=resident),
            pl.BlockSpec((1, GLA_K_WIDTH), const),
        ],
        out_specs=[
            pl.BlockSpec((FRONT_ROWS, PROJ_COLS), lambda i: (i, 0)),
            pl.BlockSpec((ATT_WIDTH, FRONT_ROWS), lambda i: (0, i)),
            pl.BlockSpec((FRONT_ROWS, GLA_V_WIDTH), lambda i: (i, 0)),
        ],
        out_shape=[
            jax.ShapeDtypeStruct((n, PROJ_COLS), jnp.bfloat16),
            jax.ShapeDtypeStruct((ATT_WIDTH, n), jnp.bfloat16),
            jax.ShapeDtypeStruct((n, GLA_V_WIDTH), jnp.bfloat16),
        ],
        scratch_shapes=[pltpu.VMEM((GLA_HEADS, GLA_DK, GLA_DV), jnp.float32)],
        compiler_params=pltpu.CompilerParams(
            dimension_semantics=("arbitrary",),
            vmem_limit_bytes=VMEM_LIMIT,
        ),
        name="front",
    )(x2, g_pre, w_main, w_vt, w_gk, up_pad, gk_bias)


def _key_window(g):
    lead = LEFT_CHUNKS // GROUP
    k_start = max(g - lead, 0) * GROUP_ROWS
    return k_start, (g + 1) * GROUP_ROWS - k_start


def _attn_scores(q_ref, k_ref, tbl_ref, base, g):
    k_start, n_keys = _key_window(g)
    q2 = q_ref[base + g * GROUP_ROWS:base + (g + 1) * GROUP_ROWS, :]
    lane = lax.broadcasted_iota(jnp.int32, q2.shape, 1)
    first = lane < ATT_DH
    zero = jnp.zeros_like(q2)
    qs = jnp.concatenate(
        [jnp.where(first, q2, zero), jnp.where(first, zero, q2)], axis=0)
    kw = k_ref[base + k_start:base + k_start + n_keys, :]
    s = lax.dot_general(kw, qs, _NT, preferred_element_type=jnp.float32)
    lo = WIN_ROWS - n_keys
    z0, z1 = max(ZERO_LO, lo) - lo, max(ZERO_HI, lo) - lo
    parts = []
    if z0 > 0:
        parts.append(s[:z0] + tbl_ref[0, lo:lo + z0, :])
    if z1 > z0:
        parts.append(s[z0:z1])
    parts.append(s[z1:] + tbl_ref[0, lo + z1:, :])
    s = jnp.concatenate(parts, axis=0)
    return s, jnp.max(s, axis=0, keepdims=True)


def _attn_values(vt_ref, p, base, g):
    k_start, n_keys = _key_window(g)
    vt_aug = jnp.concatenate(
        [vt_ref[:, base + k_start:base + k_start + n_keys],
         jnp.ones((BF16_SUBLANES, n_keys), jnp.bfloat16)], axis=0)
    ot = jnp.dot(vt_aug, p, preferred_element_type=jnp.float32)
    r = 1.0 / ot[LANES:LANES + 1, :]
    return jnp.concatenate(
        [ot[:ATT_DH, :GROUP_ROWS] * r[:, :GROUP_ROWS],
         ot[ATT_DH:LANES, GROUP_ROWS:] * r[:, GROUP_ROWS:]], axis=0)


def _attn_kernel(q_ref, k_ref, vt_ref, g_ref, tbl_ref, z_ref, *, seq):
    stage_rows = STAGE_GROUPS * GROUP_ROWS
    stages = [(u * seq, t) for u in range(ATTN_SEQS) for t in range(seq // stage_rows)]

    def scores(stage):
        base, t = stage
        return [_attn_scores(q_ref, k_ref, tbl_ref, base, STAGE_GROUPS * t + j)
                for j in range(STAGE_GROUPS)]

    cur = scores(stages[0])
    for i, (base, t) in enumerate(stages):
        if i + 1 < len(stages):
            nxt = scores(stages[i + 1])
        probs = [jnp.exp2((s - m).astype(jnp.bfloat16)) for s, m in cur]
        ot = jnp.concatenate(
            [_attn_values(vt_ref, p, base, STAGE_GROUPS * t + j) for j, p in enumerate(probs)],
            axis=1)
        rows = slice(base + t * stage_rows, base + (t + 1) * stage_rows)
        gate = g_ref[rows, :].astype(jnp.float32)
        z_ref[rows, :] = (ot.T * gate * _sigmoid(gate)).astype(jnp.bfloat16)
        if i + 1 < len(stages):
            cur = nxt


def _attn(proj, v_t, table, batch, seq):
    n = proj.shape[0]
    pairs = ATT_HEADS // HEADS_PER_STEP
    assert batch % ATTN_SEQS == 0, (batch, ATTN_SEQS)
    rows = ATTN_SEQS * seq
    col_map = lambda off: (lambda p, b: (b, off // LANES + p))
    return pl.pallas_call(
        functools.partial(_attn_kernel, seq=seq),
        grid=(pairs, batch // ATTN_SEQS),
        in_specs=[
            pl.BlockSpec((rows, LANES), col_map(QA_OFF)),
            pl.BlockSpec((rows, LANES), col_map(KA_OFF)),
            pl.BlockSpec((LANES, rows), lambda p, b: (p, b)),
            pl.BlockSpec((rows, LANES), col_map(GA_OFF)),
            pl.BlockSpec((1, WIN_ROWS, PAIR_COLS), lambda p, b: (p, 0, 0)),
        ],
        out_specs=pl.BlockSpec((rows, LANES), lambda p, b: (b, p)),
        out_shape=jax.ShapeDtypeStruct((n, ATT_WIDTH), jnp.bfloat16),
        compiler_params=pltpu.CompilerParams(
            dimension_semantics=("arbitrary", "arbitrary"),
            vmem_limit_bytes=VMEM_LIMIT,
        ),
        name="attn",
    )(proj, proj, v_t, proj, table)


def _bias_table(rel_bias):
    heads = rel_bias.shape[0]
    pairs = heads // HEADS_PER_STEP
    t = np.arange(TABLE_PERIOD)
    dist = np.where(t < GROUP_ROWS, t, t - TABLE_PERIOD) + (WIN_ROWS - GROUP_ROWS)
    rel_idx = np.clip(dist, -MAX_REL, MAX_REL) + MAX_REL
    w = (rel_bias[:, rel_idx] - rel_bias[:, 2 * MAX_REL:]).astype(jnp.float32) * LOG2_E
    return pl.pallas_call(
        _bias_table_kernel,
        grid=(pairs,),
        in_specs=[pl.BlockSpec((HEADS_PER_STEP, 1, TABLE_PERIOD), lambda p: (p, 0, 0))],
        out_specs=pl.BlockSpec((1, WIN_ROWS, PAIR_COLS), lambda p: (p, 0, 0)),
        out_shape=jax.ShapeDtypeStruct((pairs, WIN_ROWS, PAIR_COLS), jnp.float32),
        compiler_params=pltpu.CompilerParams(
            dimension_semantics=("arbitrary",), vmem_limit_bytes=VMEM_LIMIT),
        name="bias_table",
    )(w.reshape(heads, 1, TABLE_PERIOD))


def _bias_table_kernel(w_ref, o_ref):
    j = lax.broadcasted_iota(jnp.int32, (WIN_ROWS, GROUP_ROWS), 0)
    i = lax.broadcasted_iota(jnp.int32, (WIN_ROWS, GROUP_ROWS), 1)
    dc = j // CHUNK - i // CHUNK
    valid = (dc >= 0) & (dc <= LEFT_CHUNKS)
    for hh in range(HEADS_PER_STEP):
        rows = jnp.broadcast_to(w_ref[hh], (WIN_ROWS, TABLE_PERIOD))
        toep = pltpu.roll(rows, 0, axis=1, stride=1, stride_axis=0)[:, :GROUP_ROWS]
        o_ref[0, :, hh * GROUP_ROWS:(hh + 1) * GROUP_ROWS] = jnp.where(valid, toep, NEG_INF)


def _merge_kernel(zg_ref, za_ref, l_ref, x_ref, wog_ref, woa_ref,
                  wout_ref, mb_ref, gp_ref, o_ref):
    yg = jnp.dot(zg_ref[...], wog_ref[...], preferred_element_type=jnp.float32)
    ya = jnp.dot(za_ref[...], woa_ref[...], preferred_element_type=jnp.float32)
    gate_g = _sigmoid(l_ref[:, :D_MODEL].astype(jnp.float32) + mb_ref[0:1, :])
    gate_a = _sigmoid(l_ref[:, D_MODEL:].astype(jnp.float32) + mb_ref[1:2, :])
    merged = (gate_g * yg + gate_a * ya).astype(jnp.bfloat16)
    y = jnp.dot(merged, wout_ref[...], preferred_element_type=jnp.float32)
    ms = jnp.mean(y * y, axis=-1, keepdims=True)
    o_ref[...] = x_ref[...] + y * lax.rsqrt(ms + RMS_EPS) * gp_ref[...]


def _merge(z_gla, z_att, proj, x2, w_o_gla, w_o_att, w_out, merge_bias2, g_post):
    n = x2.shape[0]
    row = lambda i: (i, 0)
    const = lambda i: (0, 0)
    return pl.pallas_call(
        _merge_kernel,
        grid=(n // MERGE_TM,),
        in_specs=[
            pl.BlockSpec((MERGE_TM, D_MODEL), row),
            pl.BlockSpec((MERGE_TM, D_MODEL), row),
            pl.BlockSpec((MERGE_TM, 2 * D_MODEL), lambda i: (i, GATE_OFF // (2 * D_MODEL))),
            pl.BlockSpec((MERGE_TM, D_MODEL), row),
            pl.BlockSpec((D_MODEL, D_MODEL), const),
            pl.BlockSpec((D_MODEL, D_MODEL), const),
            pl.BlockSpec((D_MODEL, D_MODEL), const),
            pl.BlockSpec((2, D_MODEL), const),
            pl.BlockSpec((1, D_MODEL), const),
        ],
        out_specs=pl.BlockSpec((MERGE_TM, D_MODEL), row),
        out_shape=jax.ShapeDtypeStruct((n, D_MODEL), jnp.float32),
        compiler_params=pltpu.CompilerParams(
            dimension_semantics=("arbitrary",),
            vmem_limit_bytes=VMEM_LIMIT,
        ),
        name="merge",
    )(z_gla, z_att, proj, x2, w_o_gla, w_o_att, w_out, merge_bias2, g_post)


def kernel(x, norm_pre_g, w_in, gk_up, gk_bias, gla_norm_g, rel_bias, w_o_gla,
           w_o_att, merge_bias, w_out, norm_post_g):
    batch, seq, d = x.shape
    assert d == D_MODEL and seq % FRONT_ROWS == 0 and seq % GROUP_ROWS == 0
    assert (batch * seq) % MERGE_TM == 0
    assert w_in.shape == (D_MODEL, IN_COLS)
    assert rel_bias.shape == (ATT_HEADS, 2 * MAX_REL + 1)
    bf16 = jnp.bfloat16

    w_main, w_vt, w_gk = _wprep(w_in)
    up_pad = jnp.pad(gk_up, ((0, GK_PAD - GK_RANK), (0, 0))).astype(bf16)

    x2 = x.reshape(batch * seq, D_MODEL)
    proj, v_t, z_gla = _front(x2, norm_pre_g.reshape(1, D_MODEL), w_main, w_vt, w_gk,
                              up_pad, gk_bias.reshape(1, GLA_K_WIDTH), seq)
    z_att = _attn(proj, v_t, _bias_table(rel_bias), batch, seq)
    w_og = (jnp.tile(gla_norm_g, GLA_HEADS)[:, None] * w_o_gla).astype(bf16)
    out = _merge(z_gla, z_att, proj, x2, w_og, w_o_att.astype(bf16),
                 w_out.astype(bf16), merge_bias.reshape(2, D_MODEL),
                 norm_post_g.reshape(1, D_MODEL))
    return out.reshape(batch, seq, D_MODEL)
```

```python
import functools

import numpy as np
import jax
import jax.numpy as jnp
from jax import lax
from jax.experimental import pallas as pl
from jax.experimental.pallas import tpu as pltpu

D_MODEL = 1024
CHUNK = 64
GLA_HEADS = 4
GLA_DK = 128
GLA_DV = 256
GLA_K_WIDTH = GLA_HEADS * GLA_DK
GLA_V_WIDTH = GLA_HEADS * GLA_DV
GK_RANK = 16
GATE_NORMALIZER = 16.0
ATT_HEADS = 16
ATT_DH = 64
ATT_WIDTH = ATT_HEADS * ATT_DH
LEFT_CHUNKS = 8
MAX_REL = 256
RMS_EPS = 1e-6
NEG_INF = -1e30

LANES = 128
BF16_SUBLANES = 16
LOG2_E = float(np.log2(np.e))
LN_2 = float(np.log(2.0))

GATE_OFF = 0
QA_OFF = 2048
KA_OFF = 3072
GA_OFF = 4096
PROJ_COLS = 5120
VG_OFF = 5120
GG_OFF = 6144
QG_OFF = 7168
KG_OFF = 7680
W_COLS = 8192
GK_PAD = LANES

_SRC_QG, _SRC_KG, _SRC_VG, _SRC_GG, _SRC_CODE, _SRC_QA, _SRC_KA, _SRC_VA, _SRC_GA, _SRC_GATE = (
    int(v) for v in np.cumsum([0, GLA_K_WIDTH, GLA_K_WIDTH, GLA_V_WIDTH, GLA_V_WIDTH, GK_RANK,
                               ATT_WIDTH, ATT_WIDTH, ATT_WIDTH, ATT_WIDTH]))
IN_COLS = _SRC_GATE + 2 * D_MODEL
REGROUP = (
    (QA_OFF, _SRC_QA, ATT_WIDTH), (KA_OFF, _SRC_KA, ATT_WIDTH), (GA_OFF, _SRC_GA, ATT_WIDTH),
    (GATE_OFF, _SRC_GATE, 2 * D_MODEL), (VG_OFF, _SRC_VG, GLA_V_WIDTH),
    (GG_OFF, _SRC_GG, GLA_V_WIDTH), (QG_OFF, _SRC_QG, GLA_K_WIDTH), (KG_OFF, _SRC_KG, GLA_K_WIDTH),
)
WPREP_ROWS = 128

GROUP = 2
STAGE_GROUPS = 2
ATTN_SEQS = 4
GROUP_ROWS = GROUP * CHUNK
WIN_CHUNKS = GROUP + LEFT_CHUNKS
WIN_ROWS = WIN_CHUNKS * CHUNK
HEADS_PER_STEP = LANES // ATT_DH
PAIR_COLS = HEADS_PER_STEP * GROUP_ROWS
TABLE_PERIOD = GROUP_ROWS + WIN_ROWS
ZERO_LO = (GROUP - 1) * CHUNK
ZERO_HI = WIN_ROWS - GROUP_ROWS - MAX_REL

FRONT_ROWS = 512
FILL_COLS = 512
MERGE_TM = 1024
VMEM_LIMIT = 56 * 1024 * 1024

_NT = (((1,), (1,)), ((), ()))


def _sigmoid(x):
    return 1.0 / (1.0 + jnp.exp2(x * (-LOG2_E)))


def _wprep_kernel(wt_ref, main_ref, vt_ref, gk_ref):
    for dst, src, width in REGROUP:
        blk = wt_ref[src:src + width, :].T
        if dst == QA_OFF:
            blk = blk * (ATT_DH ** -0.5 * LOG2_E)
        main_ref[:, dst:dst + width] = blk.astype(jnp.bfloat16)
    vt_ref[...] = wt_ref[_SRC_VA:_SRC_VA + ATT_WIDTH, :].astype(jnp.bfloat16)
    code = wt_ref[_SRC_CODE:_SRC_CODE + GK_PAD, :].T
    lane = lax.broadcasted_iota(jnp.int32, code.shape, 1)
    gk_ref[...] = jnp.where(lane < GK_RANK, code, 0.0).astype(jnp.bfloat16)


def _wprep(w_in):
    k = w_in.shape[0]
    return pl.pallas_call(
        _wprep_kernel,
        grid=(k // WPREP_ROWS,),
        in_specs=[pl.BlockSpec((IN_COLS, WPREP_ROWS), lambda i: (0, i))],
        out_specs=[
            pl.BlockSpec((WPREP_ROWS, W_COLS), lambda i: (i, 0)),
            pl.BlockSpec((ATT_WIDTH, WPREP_ROWS), lambda i: (0, i)),
            pl.BlockSpec((WPREP_ROWS, GK_PAD), lambda i: (i, 0)),
        ],
        out_shape=[
            jax.ShapeDtypeStruct((k, W_COLS), jnp.bfloat16),
            jax.ShapeDtypeStruct((ATT_WIDTH, k), jnp.bfloat16),
            jax.ShapeDtypeStruct((k, GK_PAD), jnp.bfloat16),
        ],
        compiler_params=pltpu.CompilerParams(
            dimension_semantics=("arbitrary",), vmem_limit_bytes=VMEM_LIMIT),
        name="wprep",
    )(w_in.T)


def _chunk_cumsum(x):
    n_chunks = x.shape[0] // CHUNK
    hi = x.astype(jnp.bfloat16)
    lo = (x - hi.astype(jnp.float32)).astype(jnp.bfloat16)
    r = lax.broadcasted_iota(jnp.int32, (CHUNK, 2 * CHUNK), 0)
    c = lax.broadcasted_iota(jnp.int32, (CHUNK, 2 * CHUNK), 1)
    tri = (r >= jnp.bitwise_and(c, CHUNK - 1)).astype(jnp.bfloat16)
    out = []
    for n in range(n_chunks):
        rows = slice(n * CHUNK, (n + 1) * CHUNK)
        terms = jnp.concatenate([hi[rows], lo[rows]], axis=0)
        out.append(jnp.dot(tri, terms, preferred_element_type=jnp.float32))
    return jnp.stack(out, axis=0)


def _front_kernel(x_ref, g_ref, w_ref, wvt_ref, wgk_ref, up_ref, bias_ref,
                  proj_ref, vt_ref, z_ref, state_ref, *, steps_per_seq):
    n_chunks = FRONT_ROWS // CHUNK

    @pl.when(pl.program_id(0) % steps_per_seq == 0)
    def _():
        state_ref[...] = jnp.zeros_like(state_ref)

    x = x_ref[...]
    ms = jnp.mean(x * x, axis=-1, keepdims=True)
    h = (x * lax.rsqrt(ms + RMS_EPS) * g_ref[...]).astype(jnp.bfloat16)

    def project(off, width):
        return jnp.dot(h, w_ref[:, off:off + width],
                       preferred_element_type=jnp.float32).astype(jnp.bfloat16)

    def emit_proj(off):
        def run():
            proj_ref[:, off:off + FILL_COLS] = project(off, FILL_COLS)
        return run

    def emit_vt():
        vt_ref[...] = lax.dot_general(
            wvt_ref[...], h, _NT, preferred_element_type=jnp.float32
        ).astype(jnp.bfloat16)

    filler = [emit_proj(off) for off in range(0, PROJ_COLS, FILL_COLS)] + [emit_vt]

    code = jnp.dot(h, wgk_ref[...], preferred_element_type=jnp.float32).astype(jnp.bfloat16)
    v = project(VG_OFF, GLA_V_WIDTH)
    qk = project(QG_OFF, 2 * GLA_K_WIDTH)
    q, k = qk[:, :GLA_K_WIDTH], qk[:, GLA_K_WIDTH:]
    logits = jnp.dot(code, up_ref[...],
                     preferred_element_type=jnp.float32) + bias_ref[...]
    gate_all = project(GG_OFF, GLA_V_WIDTH)
    gk = (jnp.minimum(logits, 0.0) * (1.0 / GATE_NORMALIZER)
          - jnp.log2(1.0 + jnp.exp2(jnp.abs(logits) * (-LOG2_E)))
          * (LN_2 / GATE_NORMALIZER))
    a_cum3 = _chunk_cumsum(gk)
    filler.pop(0)()
    a_end = a_cum3[:, CHUNK - 1, :]
    k_dec = (k.astype(jnp.float32).reshape(n_chunks, CHUNK, GLA_K_WIDTH)
             * jnp.exp(a_end[:, None, :] - a_cum3)).astype(jnp.bfloat16)
    decay_t = jnp.exp(a_end).T
    kcols = [slice(hd * GLA_DK, (hd + 1) * GLA_DK) for hd in range(GLA_HEADS)]
    vcols = [slice(hd * GLA_DV, (hd + 1) * GLA_DV) for hd in range(GLA_HEADS)]

    def chunk_rows(c):
        return slice(c * CHUNK, (c + 1) * CHUNK)

    def increments(c):
        return [lax.dot_general(
            k_dec[c][:, kcols[hd]], v[chunk_rows(c), vcols[hd]],
            (((0,), (0,)), ((), ())), preferred_element_type=jnp.float32)
            for hd in range(GLA_HEADS)]

    def finish(c, reads):
        for hd in range(GLA_HEADS):
            o = reads[hd]
            ms_o = jnp.mean(o * o, axis=-1, keepdims=True)
            factor = lax.rsqrt(ms_o + RMS_EPS * GLA_DK)
            gate = gate_all[chunk_rows(c), vcols[hd]].astype(jnp.float32)
            z_ref[chunk_rows(c), vcols[hd]] = (
                (o * factor) * (gate * _sigmoid(gate))).astype(jnp.bfloat16)

    upd_next = increments(0)
    reads_prev = None
    for c in range(n_chunks):
        upd = upd_next
        if c + 1 < n_chunks:
            upd_next = increments(c + 1)
        if filler:
            filler.pop(0)()
        reads = []
        for hd in range(GLA_HEADS):
            state = decay_t[kcols[hd], c:c + 1] * state_ref[hd] + upd[hd]
            state_ref[hd] = state
            reads.append(jnp.dot(q[chunk_rows(c), kcols[hd]],
                                 state.astype(jnp.bfloat16),
                                 preferred_element_type=jnp.float32))
        if reads_prev is not None:
            finish(c - 1, reads_prev)
        reads_prev = reads
    finish(n_chunks - 1, reads_prev)
    while filler:
        filler.pop(0)()


def _front(x2, g_pre, w_main, w_vt, w_gk, up_pad, gk_bias, seq):
    n = x2.shape[0]
    const = lambda i: (0, 0)
    resident = pl.Buffered(1)
    return pl.pallas_call(
        functools.partial(_front_kernel, steps_per_seq=seq // FRONT_ROWS),
        grid=(n // FRONT_ROWS,),
        in_specs=[
            pl.BlockSpec((FRONT_ROWS, D_MODEL), lambda i: (i, 0)),
            pl.BlockSpec((1, D_MODEL), const),
            pl.BlockSpec((D_MODEL, W_COLS), const, pipeline_mode=resident),
            pl.BlockSpec((ATT_WIDTH, D_MODEL), const, pipeline_mode=resident),
            pl.BlockSpec((D_MODEL, GK_PAD), const, pipeline_mode=resident),
            pl.BlockSpec((GK_PAD, GLA_K_WIDTH), const, pipeline_mode=resident),
            pl.BlockSpec((1, GLA_K_WIDTH), const),
        ],
        out_specs=[
            pl.BlockSpec((FRONT_ROWS, PROJ_COLS), lambda i: (i, 0)),
            pl.BlockSpec((ATT_WIDTH, FRONT_ROWS), lambda i: (0, i)),
            pl.BlockSpec((FRONT_ROWS, GLA_V_WIDTH), lambda i: (i, 0)),
        ],
        out_shape=[
            jax.ShapeDtypeStruct((n, PROJ_COLS), jnp.bfloat16),
            jax.ShapeDtypeStruct((ATT_WIDTH, n), jnp.bfloat16),
            jax.ShapeDtypeStruct((n, GLA_V_WIDTH), jnp.bfloat16),
        ],
        scratch_shapes=[pltpu.VMEM((GLA_HEADS, GLA_DK, GLA_DV), jnp.float32)],
        compiler_params=pltpu.CompilerParams(
            dimension_semantics=("arbitrary",),
            vmem_limit_bytes=VMEM_LIMIT,
        ),
        name="front",
    )(x2, g_pre, w_main, w_vt, w_gk, up_pad, gk_bias)


def _key_window(g):
    lead = LEFT_CHUNKS // GROUP
    k_start = max(g - lead, 0) * GROUP_ROWS
    return k_start, (g + 1) * GROUP_ROWS - k_start


def _attn_scores(q_ref, k_ref, tbl_ref, base, g):
    k_start, n_keys = _key_window(g)
    q2 = q_ref[base + g * GROUP_ROWS:base + (g + 1) * GROUP_ROWS, :]
    lane = lax.broadcasted_iota(jnp.int32, q2.shape, 1)
    first = lane < ATT_DH
    zero = jnp.zeros_like(q2)
    qs = jnp.concatenate(
        [jnp.where(first, q2, zero), jnp.where(first, zero, q2)], axis=0)
    kw = k_ref[base + k_start:base + k_start + n_keys, :]
    s = lax.dot_general(kw, qs, _NT, preferred_element_type=jnp.float32)
    lo = WIN_ROWS - n_keys
    z0, z1 = max(ZERO_LO, lo) - lo, max(ZERO_HI, lo) - lo
    parts = []
    if z0 > 0:
        parts.append(s[:z0] + tbl_ref[0, lo:lo + z0, :])
    if z1 > z0:
        parts.append(s[z0:z1])
    parts.append(s[z1:] + tbl_ref[0, lo + z1:, :])
    s = jnp.concatenate(parts, axis=0)
    return s, jnp.max(s, axis=0, keepdims=True)


def _attn_values(vt_ref, p, base, g):
    k_start, n_keys = _key_window(g)
    vt_aug = jnp.concatenate(
        [vt_ref[:, base + k_start:base + k_start + n_keys],
         jnp.ones((BF16_SUBLANES, n_keys), jnp.bfloat16)], axis=0)
    ot = jnp.dot(vt_aug, p, preferred_element_type=jnp.float32)
    r = 1.0 / ot[LANES:LANES + 1, :]
    return jnp.concatenate(
        [ot[:ATT_DH, :GROUP_ROWS] * r[:, :GROUP_ROWS],
         ot[ATT_DH:LANES, GROUP_ROWS:] * r[:, GROUP_ROWS:]], axis=0)


def _attn_kernel(q_ref, k_ref, vt_ref, g_ref, tbl_ref, z_ref, *, seq):
    stage_rows = STAGE_GROUPS * GROUP_ROWS
    stages = [(u * seq, t) for u in range(ATTN_SEQS) for t in range(seq // stage_rows)]
    stages.append(stages.pop(len(stages) - seq // stage_rows))

    def scores(stage):
        base, t = stage
        return [_attn_scores(q_ref, k_ref, tbl_ref, base, STAGE_GROUPS * t + j)
                for j in range(STAGE_GROUPS)]

    cur = scores(stages[0])
    for i, (base, t) in enumerate(stages):
        if i + 1 < len(stages):
            nxt = scores(stages[i + 1])
        probs = [jnp.exp2((s - m).astype(jnp.bfloat16)) for s, m in cur]
        ot = jnp.concatenate(
            [_attn_values(vt_ref, p, base, STAGE_GROUPS * t + j) for j, p in enumerate(probs)],
            axis=1)
        rows = slice(base + t * stage_rows, base + (t + 1) * stage_rows)
        gate = g_ref[rows, :].astype(jnp.float32)
        z_ref[rows, :] = (ot.T * gate * _sigmoid(gate)).astype(jnp.bfloat16)
        if i + 1 < len(stages):
            cur = nxt


def _attn(proj, v_t, table, batch, seq):
    n = proj.shape[0]
    pairs = ATT_HEADS // HEADS_PER_STEP
    assert batch % ATTN_SEQS == 0, (batch, ATTN_SEQS)
    rows = ATTN_SEQS * seq
    col_map = lambda off: (lambda p, b: (b, off // LANES + p))
    return pl.pallas_call(
        functools.partial(_attn_kernel, seq=seq),
        grid=(pairs, batch // ATTN_SEQS),
        in_specs=[
            pl.BlockSpec((rows, LANES), col_map(QA_OFF)),
            pl.BlockSpec((rows, LANES), col_map(KA_OFF)),
            pl.BlockSpec((LANES, rows), lambda p, b: (p, b)),
            pl.BlockSpec((rows, LANES), col_map(GA_OFF)),
            pl.BlockSpec((1, WIN_ROWS, PAIR_COLS), lambda p, b: (p, 0, 0)),
        ],
        out_specs=pl.BlockSpec((rows, LANES), lambda p, b: (b, p)),
        out_shape=jax.ShapeDtypeStruct((n, ATT_WIDTH), jnp.bfloat16),
        compiler_params=pltpu.CompilerParams(
            dimension_semantics=("arbitrary", "arbitrary"),
            vmem_limit_bytes=VMEM_LIMIT,
        ),
        name="attn",
    )(proj, proj, v_t, proj, table)


def _bias_table(rel_bias):
    heads = rel_bias.shape[0]
    pairs = heads // HEADS_PER_STEP
    t = np.arange(TABLE_PERIOD)
    dist = np.where(t < GROUP_ROWS, t, t - TABLE_PERIOD) + (WIN_ROWS - GROUP_ROWS)
    rel_idx = np.clip(dist, -MAX_REL, MAX_REL) + MAX_REL
    w = (rel_bias[:, rel_idx] - rel_bias[:, 2 * MAX_REL:]).astype(jnp.float32) * LOG2_E
    return pl.pallas_call(
        _bias_table_kernel,
        grid=(pairs,),
        in_specs=[pl.BlockSpec((HEADS_PER_STEP, 1, TABLE_PERIOD), lambda p: (p, 0, 0))],
        out_specs=pl.BlockSpec((1, WIN_ROWS, PAIR_COLS), lambda p: (p, 0, 0)),
        out_shape=jax.ShapeDtypeStruct((pairs, WIN_ROWS, PAIR_COLS), jnp.float32),
        compiler_params=pltpu.CompilerParams(
            dimension_semantics=("arbitrary",), vmem_limit_bytes=VMEM_LIMIT),
        name="bias_table",
    )(w.reshape(heads, 1, TABLE_PERIOD))


def _bias_table_kernel(w_ref, o_ref):
    j = lax.broadcasted_iota(jnp.int32, (WIN_ROWS, GROUP_ROWS), 0)
    i = lax.broadcasted_iota(jnp.int32, (WIN_ROWS, GROUP_ROWS), 1)
    dc = j // CHUNK - i // CHUNK
    valid = (dc >= 0) & (dc <= LEFT_CHUNKS)
    for hh in range(HEADS_PER_STEP):
        rows = jnp.broadcast_to(w_ref[hh], (WIN_ROWS, TABLE_PERIOD))
        toep = pltpu.roll(rows, 0, axis=1, stride=1, stride_axis=0)[:, :GROUP_ROWS]
        o_ref[0, :, hh * GROUP_ROWS:(hh + 1) * GROUP_ROWS] = jnp.where(valid, toep, NEG_INF)


def _merge_kernel(zg_ref, za_ref, l_ref, x_ref, wog_ref, woa_ref,
                  wout_ref, mb_ref, gp_ref, o_ref):
    yg = jnp.dot(zg_ref[...], wog_ref[...], preferred_element_type=jnp.float32)
    ya = jnp.dot(za_ref[...], woa_ref[...], preferred_element_type=jnp.float32)
    gate_g = _sigmoid(l_ref[:, :D_MODEL].astype(jnp.float32) + mb_ref[0:1, :])
    gate_a = _sigmoid(l_ref[:, D_MODEL:].astype(jnp.float32) + mb_ref[1:2, :])
    merged = (gate_g * yg + gate_a * ya).astype(jnp.bfloat16)
    y = jnp.dot(merged, wout_ref[...], preferred_element_type=jnp.float32)
    ms = jnp.mean(y * y, axis=-1, keepdims=True)
    o_ref[...] = x_ref[...] + y * lax.rsqrt(ms + RMS_EPS) * gp_ref[...]


def _merge(z_gla, z_att, proj, x2, w_o_gla, w_o_att, w_out, merge_bias2, g_post):
    n = x2.shape[0]
    row = lambda i: (i, 0)
    const = lambda i: (0, 0)
    return pl.pallas_call(
        _merge_kernel,
        grid=(n // MERGE_TM,),
        in_specs=[
            pl.BlockSpec((MERGE_TM, D_MODEL), row),
            pl.BlockSpec((MERGE_TM, D_MODEL), row),
            pl.BlockSpec((MERGE_TM, 2 * D_MODEL), lambda i: (i, GATE_OFF // (2 * D_MODEL))),
            pl.BlockSpec((MERGE_TM, D_MODEL), row),
            pl.BlockSpec((D_MODEL, D_MODEL), const),
            pl.BlockSpec((D_MODEL, D_MODEL), const),
            pl.BlockSpec((D_MODEL, D_MODEL), const),
            pl.BlockSpec((2, D_MODEL), const),
            pl.BlockSpec((1, D_MODEL), const),
        ],
        out_specs=pl.BlockSpec((MERGE_TM, D_MODEL), row),
        out_shape=jax.ShapeDtypeStruct((n, D_MODEL), jnp.float32),
        compiler_params=pltpu.CompilerParams(
            dimension_semantics=("arbitrary",),
            vmem_limit_bytes=VMEM_LIMIT,
        ),
        name="merge",
    )(z_gla, z_att, proj, x2, w_o_gla, w_o_att, w_out, merge_bias2, g_post)


def kernel(x, norm_pre_g, w_in, gk_up, gk_bias, gla_norm_g, rel_bias, w_o_gla,
           w_o_att, merge_bias, w_out, norm_post_g):
    batch, seq, d = x.shape
    assert d == D_MODEL and seq % FRONT_ROWS == 0 and seq % GROUP_ROWS == 0
    assert (batch * seq) % MERGE_TM == 0
    assert w_in.shape == (D_MODEL, IN_COLS)
    assert rel_bias.shape == (ATT_HEADS, 2 * MAX_REL + 1)
    bf16 = jnp.bfloat16

    w_main, w_vt, w_gk = _wprep(w_in)
    up_pad = jnp.pad(gk_up, ((0, GK_PAD - GK_RANK), (0, 0))).astype(bf16)

    x2 = x.reshape(batch * seq, D_MODEL)
    proj, v_t, z_gla = _front(x2, norm_pre_g.reshape(1, D_MODEL), w_main, w_vt, w_gk,
                              up_pad, gk_bias.reshape(1, GLA_K_WIDTH), seq)
    z_att = _attn(proj, v_t, _bias_table(rel_bias), batch, seq)
    w_og = (jnp.tile(gla_norm_g, GLA_HEADS)[:, None] * w_o_gla).astype(bf16)
    out = _merge(z_gla, z_att, proj, x2, w_og, w_o_att.astype(bf16),
                 w_out.astype(bf16), merge_bias.reshape(2, D_MODEL),
                 norm_post_g.reshape(1, D_MODEL))
    return out.reshape(batch, seq, D_MODEL)
```

```python
import functools

import numpy as np
import jax
import jax.numpy as jnp
from jax import lax
from jax.experimental import pallas as pl
from jax.experimental.pallas import tpu as pltpu

D_MODEL = 1024
CHUNK = 64
GLA_HEADS = 4
GLA_DK = 128
GLA_DV = 256
GLA_K_WIDTH = GLA_HEADS * GLA_DK
GLA_V_WIDTH = GLA_HEADS * GLA_DV
GK_RANK = 16
GATE_NORMALIZER = 16.0
ATT_HEADS = 16
ATT_DH = 64
ATT_WIDTH = ATT_HEADS * ATT_DH
LEFT_CHUNKS = 8
MAX_REL = 256
RMS_EPS = 1e-6
NEG_INF = -1e30

LANES = 128
BF16_SUBLANES = 16
LOG2_E = float(np.log2(np.e))
LN_2 = float(np.log(2.0))

GATE_OFF = 0
QA_OFF = 2048
KA_OFF = 3072
GA_OFF = 4096
PROJ_COLS = 5120
VG_OFF = 5120
GG_OFF = 6144
QG_OFF = 7168
KG_OFF = 7680
W_COLS = 8192
GK_PAD = LANES

_SRC_QG, _SRC_KG, _SRC_VG, _SRC_GG, _SRC_CODE, _SRC_QA, _SRC_KA, _SRC_VA, _SRC_GA, _SRC_GATE = (
    int(v) for v in np.cumsum([0, GLA_K_WIDTH, GLA_K_WIDTH, GLA_V_WIDTH, GLA_V_WIDTH, GK_RANK,
                               ATT_WIDTH, ATT_WIDTH, ATT_WIDTH, ATT_WIDTH]))
IN_COLS = _SRC_GATE + 2 * D_MODEL
REGROUP = (
    (QA_OFF, _SRC_QA, ATT_WIDTH), (KA_OFF, _SRC_KA, ATT_WIDTH), (GA_OFF, _SRC_GA, ATT_WIDTH),
    (GATE_OFF, _SRC_GATE, 2 * D_MODEL), (VG_OFF, _SRC_VG, GLA_V_WIDTH),
    (GG_OFF, _SRC_GG, GLA_V_WIDTH), (QG_OFF, _SRC_QG, GLA_K_WIDTH), (KG_OFF, _SRC_KG, GLA_K_WIDTH),
)
WPREP_ROWS = 128

GROUP = 2
STAGE_GROUPS = 2
ATTN_SEQS = 2
GROUP_ROWS = GROUP * CHUNK
WIN_CHUNKS = GROUP + LEFT_CHUNKS
WIN_ROWS = WIN_CHUNKS * CHUNK
HEADS_PER_STEP = LANES // ATT_DH
PAIR_COLS = HEADS_PER_STEP * GROUP_ROWS
TABLE_PERIOD = GROUP_ROWS + WIN_ROWS
ZERO_LO = (GROUP - 1) * CHUNK
ZERO_HI = WIN_ROWS - GROUP_ROWS - MAX_REL

FRONT_ROWS = 512
FILL_COLS = 512
MERGE_TM = 1024
VMEM_LIMIT = 56 * 1024 * 1024

_NT = (((1,), (1,)), ((), ()))


def _sigmoid(x):
    return 1.0 / (1.0 + jnp.exp2(x * (-LOG2_E)))


def _wprep_kernel(wt_ref, main_ref, vt_ref, gk_ref):
    for dst, src, width in REGROUP:
        blk = wt_ref[src:src + width, :].T
        if dst == QA_OFF:
            blk = blk * (ATT_DH ** -0.5 * LOG2_E)
        main_ref[:, dst:dst + width] = blk.astype(jnp.bfloat16)
    vt_ref[...] = wt_ref[_SRC_VA:_SRC_VA + ATT_WIDTH, :].astype(jnp.bfloat16)
    code = wt_ref[_SRC_CODE:_SRC_CODE + GK_PAD, :].T
    lane = lax.broadcasted_iota(jnp.int32, code.shape, 1)
    gk_ref[...] = jnp.where(lane < GK_RANK, code, 0.0).astype(jnp.bfloat16)


def _wprep(w_in):
    k = w_in.shape[0]
    return pl.pallas_call(
        _wprep_kernel,
        grid=(k // WPREP_ROWS,),
        in_specs=[pl.BlockSpec((IN_COLS, WPREP_ROWS), lambda i: (0, i))],
        out_specs=[
            pl.BlockSpec((WPREP_ROWS, W_COLS), lambda i: (i, 0)),
            pl.BlockSpec((ATT_WIDTH, WPREP_ROWS), lambda i: (0, i)),
            pl.BlockSpec((WPREP_ROWS, GK_PAD), lambda i: (i, 0)),
        ],
        out_shape=[
            jax.ShapeDtypeStruct((k, W_COLS), jnp.bfloat16),
            jax.ShapeDtypeStruct((ATT_WIDTH, k), jnp.bfloat16),
            jax.ShapeDtypeStruct((k, GK_PAD), jnp.bfloat16),
        ],
        compiler_params=pltpu.CompilerParams(
            dimension_semantics=("arbitrary",), vmem_limit_bytes=VMEM_LIMIT),
        name="wprep",
    )(w_in.T)


def _chunk_cumsum(x):
    n_chunks = x.shape[0] // CHUNK
    hi = x.astype(jnp.bfloat16)
    lo = (x - hi.astype(jnp.float32)).astype(jnp.bfloat16)
    r = lax.broadcasted_iota(jnp.int32, (CHUNK, 2 * CHUNK), 0)
    c = lax.broadcasted_iota(jnp.int32, (CHUNK, 2 * CHUNK), 1)
    tri = (r >= jnp.bitwise_and(c, CHUNK - 1)).astype(jnp.bfloat16)
    out = []
    for n in range(n_chunks):
        rows = slice(n * CHUNK, (n + 1) * CHUNK)
        terms = jnp.concatenate([hi[rows], lo[rows]], axis=0)
        out.append(jnp.dot(tri, terms, preferred_element_type=jnp.float32))
    return jnp.stack(out, axis=0)


def _front_kernel(x_ref, g_ref, w_ref, wvt_ref, wgk_ref, up_ref, bias_ref,
                  proj_ref, vt_ref, z_ref, state_ref, *, steps_per_seq):
    n_chunks = FRONT_ROWS // CHUNK

    @pl.when(pl.program_id(0) % steps_per_seq == 0)
    def _():
        state_ref[...] = jnp.zeros_like(state_ref)

    x = x_ref[...]
    ms = jnp.mean(x * x, axis=-1, keepdims=True)
    h = (x * lax.rsqrt(ms + RMS_EPS) * g_ref[...]).astype(jnp.bfloat16)

    def project(off, width):
        return jnp.dot(h, w_ref[:, off:off + width],
                       preferred_element_type=jnp.float32).astype(jnp.bfloat16)

    def emit_proj(off):
        def run():
            proj_ref[:, off:off + FILL_COLS] = project(off, FILL_COLS)
        return run

    def emit_vt():
        vt_ref[...] = lax.dot_general(
            wvt_ref[...], h, _NT, preferred_element_type=jnp.float32
        ).astype(jnp.bfloat16)

    filler = [emit_proj(off) for off in range(0, PROJ_COLS, FILL_COLS)] + [emit_vt]

    code = jnp.dot(h, wgk_ref[...], preferred_element_type=jnp.float32).astype(jnp.bfloat16)
    v = project(VG_OFF, GLA_V_WIDTH)
    qk = project(QG_OFF, 2 * GLA_K_WIDTH)
    q, k = qk[:, :GLA_K_WIDTH], qk[:, GLA_K_WIDTH:]
    logits = jnp.dot(code, up_ref[...],
                     preferred_element_type=jnp.float32) + bias_ref[...]
    gate_all = project(GG_OFF, GLA_V_WIDTH)
    gk = (jnp.minimum(logits, 0.0) * (1.0 / GATE_NORMALIZER)
          - jnp.log2(1.0 + jnp.exp2(jnp.abs(logits) * (-LOG2_E)))
          * (LN_2 / GATE_NORMALIZER))
    a_cum3 = _chunk_cumsum(gk)
    filler.pop(0)()
    a_end = a_cum3[:, CHUNK - 1, :]
    k_dec = (k.astype(jnp.float32).reshape(n_chunks, CHUNK, GLA_K_WIDTH)
             * jnp.exp(a_end[:, None, :] - a_cum3)).astype(jnp.bfloat16)
    decay_t = jnp.exp(a_end).T
    kcols = [slice(hd * GLA_DK, (hd + 1) * GLA_DK) for hd in range(GLA_HEADS)]
    vcols = [slice(hd * GLA_DV, (hd + 1) * GLA_DV) for hd in range(GLA_HEADS)]

    def chunk_rows(c):
        return slice(c * CHUNK, (c + 1) * CHUNK)

    def increments(c):
        return [lax.dot_general(
            k_dec[c][:, kcols[hd]], v[chunk_rows(c), vcols[hd]],
            (((0,), (0,)), ((), ())), preferred_element_type=jnp.float32)
            for hd in range(GLA_HEADS)]

    def finish(c, reads):
        for hd in range(GLA_HEADS):
            o = reads[hd]
            ms_o = jnp.mean(o * o, axis=-1, keepdims=True)
            factor = lax.rsqrt(ms_o + RMS_EPS * GLA_DK)
            gate = gate_all[chunk_rows(c), vcols[hd]].astype(jnp.float32)
            z_ref[chunk_rows(c), vcols[hd]] = (
                (o * factor) * (gate * _sigmoid(gate))).astype(jnp.bfloat16)

    upd_next = increments(0)
    reads_prev = None
    for c in range(n_chunks):
        upd = upd_next
        if c + 1 < n_chunks:
            upd_next = increments(c + 1)
        if filler:
            filler.pop(0)()
        reads = []
        for hd in range(GLA_HEADS):
            state = decay_t[kcols[hd], c:c + 1] * state_ref[hd] + upd[hd]
            state_ref[hd] = state
            reads.append(jnp.dot(q[chunk_rows(c), kcols[hd]],
                                 state.astype(jnp.bfloat16),
                                 preferred_element_type=jnp.float32))
        if reads_prev is not None:
            finish(c - 1, reads_prev)
        reads_prev = reads
    finish(n_chunks - 1, reads_prev)
    while filler:
        filler.pop(0)()


def _front(x2, g_pre, w_main, w_vt, w_gk, up_pad, gk_bias, seq):
    n = x2.shape[0]
    const = lambda i: (0, 0)
    resident = pl.Buffered(1)
    return pl.pallas_call(
        functools.partial(_front_kernel, steps_per_seq=seq // FRONT_ROWS),
        grid=(n // FRONT_ROWS,),
        in_specs=[
            pl.BlockSpec((FRONT_ROWS, D_MODEL), lambda i: (i, 0)),
            pl.BlockSpec((1, D_MODEL), const),
            pl.BlockSpec((D_MODEL, W_COLS), const, pipeline_mode=resident),
            pl.BlockSpec((ATT_WIDTH, D_MODEL), const, pipeline_mode=resident),
            pl.BlockSpec((D_MODEL, GK_PAD), const, pipeline_mode=resident),
            pl.BlockSpec((GK_PAD, GLA_K_WIDTH), const, pipeline_mode=resident),
            pl.BlockSpec((1, GLA_K_WIDTH), const),
        ],
        out_specs=[
            pl.BlockSpec((FRONT_ROWS, PROJ_COLS), lambda i: (i, 0)),
            pl.BlockSpec((ATT_WIDTH, FRONT_ROWS), lambda i: (0, i)),
            pl.BlockSpec((FRONT_ROWS, GLA_V_WIDTH), lambda i: (i, 0)),
        ],
        out_shape=[
            jax.ShapeDtypeStruct((n, PROJ_COLS), jnp.bfloat16),
            jax.ShapeDtypeStruct((ATT_WIDTH, n), jnp.bfloat16),
            jax.ShapeDtypeStruct((n, GLA_V_WIDTH), jnp.bfloat16),
        ],
        scratch_shapes=[pltpu.VMEM((GLA_HEADS, GLA_DK, GLA_DV), jnp.float32)],
        compiler_params=pltpu.CompilerParams(
            dimension_semantics=("arbitrary",),
            vmem_limit_bytes=VMEM_LIMIT,
        ),
        name="front",
    )(x2, g_pre, w_main, w_vt, w_gk, up_pad, gk_bias)


def _key_window(g):
    lead = LEFT_CHUNKS // GROUP
    k_start = max(g - lead, 0) * GROUP_ROWS
    return k_start, (g + 1) * GROUP_ROWS - k_start


def _attn_scores(q_ref, k_ref, tbl_ref, base, g):
    k_start, n_keys = _key_window(g)
    q2 = q_ref[base + g * GROUP_ROWS:base + (g + 1) * GROUP_ROWS, :]
    lane = lax.broadcasted_iota(jnp.int32, q2.shape, 1)
    first = lane < ATT_DH
    zero = jnp.zeros_like(q2)
    qs = jnp.concatenate(
        [jnp.where(first, q2, zero), jnp.where(first, zero, q2)], axis=0)
    kw = k_ref[base + k_start:base + k_start + n_keys, :]
    s = lax.dot_general(kw, qs, _NT, preferred_element_type=jnp.float32)
    lo = WIN_ROWS - n_keys
    z0, z1 = max(ZERO_LO, lo) - lo, max(ZERO_HI, lo) - lo
    parts = []
    if z0 > 0:
        parts.append(s[:z0] + tbl_ref[0, lo:lo + z0, :])
    if z1 > z0:
        parts.append(s[z0:z1])
    parts.append(s[z1:] + tbl_ref[0, lo + z1:, :])
    s = jnp.concatenate(parts, axis=0)
    return s, jnp.max(s, axis=0, keepdims=True)


def _attn_values(vt_ref, p, base, g):
    k_start, n_keys = _key_window(g)
    vt_aug = jnp.concatenate(
        [vt_ref[:, base + k_start:base + k_start + n_keys],
         jnp.ones((BF16_SUBLANES, n_keys), jnp.bfloat16)], axis=0)
    ot = jnp.dot(vt_aug, p, preferred_element_type=jnp.float32)
    r = 1.0 / ot[LANES:LANES + 1, :]
    return jnp.concatenate(
        [ot[:ATT_DH, :GROUP_ROWS] * r[:, :GROUP_ROWS],
         ot[ATT_DH:LANES, GROUP_ROWS:] * r[:, GROUP_ROWS:]], axis=0)


def _attn_kernel(q_ref, k_ref, vt_ref, g_ref, tbl_ref, z_ref, *, seq):
    stage_rows = STAGE_GROUPS * GROUP_ROWS
    stages = [(u * seq, t) for u in range(ATTN_SEQS) for t in range(seq // stage_rows)]
    stages.append(stages.pop(len(stages) - seq // stage_rows))

    def scores(stage):
        base, t = stage
        return [_attn_scores(q_ref, k_ref, tbl_ref, base, STAGE_GROUPS * t + j)
                for j in range(STAGE_GROUPS)]

    cur = scores(stages[0])
    for i, (base, t) in enumerate(stages):
        if i + 1 < len(stages):
            nxt = scores(stages[i + 1])
        probs = [jnp.exp2((s - m).astype(jnp.bfloat16)) for s, m in cur]
        ot = jnp.concatenate(
            [_attn_values(vt_ref, p, base, STAGE_GROUPS * t + j) for j, p in enumerate(probs)],
            axis=1)
        rows = slice(base + t * stage_rows, base + (t + 1) * stage_rows)
        gate = g_ref[rows, :].astype(jnp.float32)
        z_ref[rows, :] = (ot.T * gate * _sigmoid(gate)).astype(jnp.bfloat16)
        if i + 1 < len(stages):
            cur = nxt


def _attn(proj, v_t, table, batch, seq):
    n = proj.shape[0]
    pairs = ATT_HEADS // HEADS_PER_STEP
    assert batch % ATTN_SEQS == 0, (batch, ATTN_SEQS)
    rows = ATTN_SEQS * seq
    col_map = lambda off: (lambda p, b: (b, off // LANES + p))
    return pl.pallas_call(
        functools.partial(_attn_kernel, seq=seq),
        grid=(pairs, batch // ATTN_SEQS),
        in_specs=[
            pl.BlockSpec((rows, LANES), col_map(QA_OFF)),
            pl.BlockSpec((rows, LANES), col_map(KA_OFF)),
            pl.BlockSpec((LANES, rows), lambda p, b: (p, b)),
            pl.BlockSpec((rows, LANES), col_map(GA_OFF)),
            pl.BlockSpec((1, WIN_ROWS, PAIR_COLS), lambda p, b: (p, 0, 0)),
        ],
        out_specs=pl.BlockSpec((rows, LANES), lambda p, b: (b, p)),
        out_shape=jax.ShapeDtypeStruct((n, ATT_WIDTH), jnp.bfloat16),
        compiler_params=pltpu.CompilerParams(
            dimension_semantics=("arbitrary", "arbitrary"),
            vmem_limit_bytes=VMEM_LIMIT,
        ),
        name="attn",
    )(proj, proj, v_t, proj, table)


def _bias_table(rel_bias):
    heads = rel_bias.shape[0]
    pairs = heads // HEADS_PER_STEP
    t = np.arange(TABLE_PERIOD)
    dist = np.where(t < GROUP_ROWS, t, t - TABLE_PERIOD) + (WIN_ROWS - GROUP_ROWS)
    rel_idx = np.clip(dist, -MAX_REL, MAX_REL) + MAX_REL
    w = (rel_bias[:, rel_idx] - rel_bias[:, 2 * MAX_REL:]).astype(jnp.float32) * LOG2_E
    return pl.pallas_call(
        _bias_table_kernel,
        grid=(pairs,),
        in_specs=[pl.BlockSpec((HEADS_PER_STEP, 1, TABLE_PERIOD), lambda p: (p, 0, 0))],
        out_specs=pl.BlockSpec((1, WIN_ROWS, PAIR_COLS), lambda p: (p, 0, 0)),
        out_shape=jax.ShapeDtypeStruct((pairs, WIN_ROWS, PAIR_COLS), jnp.float32),
        compiler_params=pltpu.CompilerParams(
            dimension_semantics=("arbitrary",), vmem_limit_bytes=VMEM_LIMIT),
        name="bias_table",
    )(w.reshape(heads, 1, TABLE_PERIOD))


def _bias_table_kernel(w_ref, o_ref):
    j = lax.broadcasted_iota(jnp.int32, (WIN_ROWS, GROUP_ROWS), 0)
    i = lax.broadcasted_iota(jnp.int32, (WIN_ROWS, GROUP_ROWS), 1)
    dc = j // CHUNK - i // CHUNK
    valid = (dc >= 0) & (dc <= LEFT_CHUNKS)
    for hh in range(HEADS_PER_STEP):
        rows = jnp.broadcast_to(w_ref[hh], (WIN_ROWS, TABLE_PERIOD))
        toep = pltpu.roll(rows, 0, axis=1, stride=1, stride_axis=0)[:, :GROUP_ROWS]
        o_ref[0, :, hh * GROUP_ROWS:(hh + 1) * GROUP_ROWS] = jnp.where(valid, toep, NEG_INF)


def _merge_kernel(zg_ref, za_ref, l_ref, x_ref, wog_ref, woa_ref,
                  wout_ref, mb_ref, gp_ref, o_ref):
    yg = jnp.dot(zg_ref[...], wog_ref[...], preferred_element_type=jnp.float32)
    ya = jnp.dot(za_ref[...], woa_ref[...], preferred_element_type=jnp.float32)
    gate_g = _sigmoid(l_ref[:, :D_MODEL].astype(jnp.float32) + mb_ref[0:1, :])
    gate_a = _sigmoid(l_ref[:, D_MODEL:].astype(jnp.float32) + mb_ref[1:2, :])
    merged = (gate_g * yg + gate_a * ya).astype(jnp.bfloat16)
    y = jnp.dot(merged, wout_ref[...], preferred_element_type=jnp.float32)
    ms = jnp.mean(y * y, axis=-1, keepdims=True)
    o_ref[...] = x_ref[...] + y * lax.rsqrt(ms + RMS_EPS) * gp_ref[...]


def _merge(z_gla, z_att, proj, x2, w_o_gla, w_o_att, w_out, merge_bias2, g_post):
    n = x2.shape[0]
    row = lambda i: (i, 0)
    const = lambda i: (0, 0)
    return pl.pallas_call(
        _merge_kernel,
        grid=(n // MERGE_TM,),
        in_specs=[
            pl.BlockSpec((MERGE_TM, D_MODEL), row),
            pl.BlockSpec((MERGE_TM, D_MODEL), row),
            pl.BlockSpec((MERGE_TM, 2 * D_MODEL), lambda i: (i, GATE_OFF // (2 * D_MODEL))),
            pl.BlockSpec((MERGE_TM, D_MODEL), row),
            pl.BlockSpec((D_MODEL, D_MODEL), const),
            pl.BlockSpec((D_MODEL, D_MODEL), const),
            pl.BlockSpec((D_MODEL, D_MODEL), const),
            pl.BlockSpec((2, D_MODEL), const),
            pl.BlockSpec((1, D_MODEL), const),
        ],
        out_specs=pl.BlockSpec((MERGE_TM, D_MODEL), row),
        out_shape=jax.ShapeDtypeStruct((n, D_MODEL), jnp.float32),
        compiler_params=pltpu.CompilerParams(
            dimension_semantics=("arbitrary",),
            vmem_limit_bytes=VMEM_LIMIT,
        ),
        name="merge",
    )(z_gla, z_att, proj, x2, w_o_gla, w_o_att, w_out, merge_bias2, g_post)


def kernel(x, norm_pre_g, w_in, gk_up, gk_bias, gla_norm_g, rel_bias, w_o_gla,
           w_o_att, merge_bias, w_out, norm_post_g):
    batch, seq, d = x.shape
    assert d == D_MODEL and seq % FRONT_ROWS == 0 and seq % GROUP_ROWS == 0
    assert (batch * seq) % MERGE_TM == 0
    assert w_in.shape == (D_MODEL, IN_COLS)
    assert rel_bias.shape == (ATT_HEADS, 2 * MAX_REL + 1)
    bf16 = jnp.bfloat16

    w_main, w_vt, w_gk = _wprep(w_in)
    up_pad = jnp.pad(gk_up, ((0, GK_PAD - GK_RANK), (0, 0))).astype(bf16)

    x2 = x.reshape(batch * seq, D_MODEL)
    proj, v_t, z_gla = _front(x2, norm_pre_g.reshape(1, D_MODEL), w_main, w_vt, w_gk,
                              up_pad, gk_bias.reshape(1, GLA_K_WIDTH), seq)
    z_att = _attn(proj, v_t, _bias_table(rel_bias), batch, seq)
    w_og = (jnp.tile(gla_norm_g, GLA_HEADS)[:, None] * w_o_gla).astype(bf16)
    out = _merge(z_gla, z_att, proj, x2, w_og, w_o_att.astype(bf16),
                 w_out.astype(bf16), merge_bias.reshape(2, D_MODEL),
                 norm_post_g.reshape(1, D_MODEL))
    return out.reshape(batch, seq, D_MODEL)
```
